```python
import math
import jax, jax.numpy as jnp
from jax import lax
import numpy as np

D_MODEL = 1024
BATCH = 2
SEQ = 8192
DEPTH = 2

N_EVEN = (DEPTH + 1) // 2
N_ODD = DEPTH // 2
N_SUB = 3
D_FF = 2816
EPS = 1e-6

POOL_WINDOWS = (2, 4, 8, 16)
POOL_GROUPS = len(POOL_WINDOWS)
POOL_WIDTH = D_MODEL // 2
POOL_GROUP_DIM = POOL_WIDTH // POOL_GROUPS

DIFF_HEADS = 4
DIFF_HEAD_DIM = 64
DIFF_V_DIM = 2 * DIFF_HEAD_DIM
DIFF_QK_WIDTH = DIFF_HEADS * 2 * DIFF_HEAD_DIM
DIFF_WIDTH = DIFF_HEADS * DIFF_V_DIM
Q_BLOCK = 128

EVEN_IN = POOL_WIDTH + 2 * DIFF_QK_WIDTH + DIFF_WIDTH
EVEN_MIX = POOL_WIDTH + DIFF_WIDTH

MLSTM_HEADS = 4
MLSTM_QK_DIM = 128
MLSTM_V_DIM = 256
MLSTM_CHUNK = 128
MLSTM_QK_WIDTH = MLSTM_HEADS * MLSTM_QK_DIM
MLSTM_V_WIDTH = MLSTM_HEADS * MLSTM_V_DIM
ODD_IN = 2 * MLSTM_QK_WIDTH + 2 * MLSTM_V_WIDTH + 2 * MLSTM_HEADS
ODD_MIX = MLSTM_V_WIDTH

kernel_name = "hybrid_pool_diffattn_mlstm_macaron_adaln"


def rmsnorm(x, g):
    xf = x.astype(jnp.float32)
    y = xf * lax.rsqrt(jnp.mean(xf * xf, axis=-1, keepdims=True) + EPS)
    return (y * g.astype(jnp.float32)).astype(x.dtype)


def modnorm(x, g, shift, scale):
    return rmsnorm(x, g) * (1 + scale[:, None, :]) + shift[:, None, :]


def swiglu(h, w_gate, w_up, w_down):
    return (jax.nn.silu(h @ w_gate) * (h @ w_up)) @ w_down


def pool_mixer(u, w_group, scale):
    B, S, _ = u.shape
    ug = u.astype(jnp.float32).reshape(B, S, POOL_GROUPS, POOL_GROUP_DIM)
    cs = jnp.cumsum(ug, axis=1)
    pos = jnp.arange(S)
    means = []
    for g, w in enumerate(POOL_WINDOWS):
        csg = cs[:, :, g]
        cs_pad = jnp.pad(csg, ((0, 0), (w, 0), (0, 0)))
        total = cs_pad[:, w:] - cs_pad[:, :S]
        count = jnp.minimum(pos + 1, w).astype(jnp.float32)
        means.append(total / count[None, :, None])
    pooled = jnp.stack(means, axis=2) - ug
    y = jnp.einsum('bsgc,gcd->bsgd', pooled, w_group.astype(jnp.float32))
    y = y.reshape(B, S, POOL_WIDTH) * scale.astype(jnp.float32)
    return y.astype(u.dtype)


def diff_attention(q, k, v, lam_params, subln_g, lambda_init):
    B, S = q.shape[0], q.shape[1]
    nb = S // Q_BLOCK
    lp = lam_params.astype(jnp.float32)
    lam = jnp.exp(jnp.sum(lp[0] * lp[1])) - jnp.exp(jnp.sum(lp[2] * lp[3])) + lambda_init
    kf = k.astype(jnp.float32)
    vf = v.astype(jnp.float32)
    qb = (q.astype(jnp.float32) * DIFF_HEAD_DIM ** -0.5).reshape(
        B, nb, Q_BLOCK, DIFF_HEADS, 2, DIFF_HEAD_DIM).transpose(1, 0, 2, 3, 4, 5)
    key_pos = jnp.arange(S)

    def block(args):
        i, qblk = args
        qpos = i * Q_BLOCK + jnp.arange(Q_BLOCK)
        s = jnp.einsum('bqhcd,bkhcd->bhcqk', qblk, kf)
        s = jnp.where(key_pos[None, :] <= qpos[:, None], s, -jnp.inf)
        p = jax.nn.softmax(s, axis=-1)
        a = p[:, :, 0] - lam * p[:, :, 1]
        return jnp.einsum('bhqk,bkhe->bqhe', a, vf)

    o = lax.map(block, (jnp.arange(nb), qb))
    o = o.transpose(1, 0, 2, 3, 4).reshape(B, S, DIFF_HEADS, DIFF_V_DIM)
    o = rmsnorm(o, subln_g) * (1.0 - lambda_init)
    return o.reshape(B, S, DIFF_WIDTH).astype(q.dtype)


def mlstm_chunkwise(q, k, v, i_pre, f_pre):
    B, H, S, dk = q.shape
    dv = v.shape[-1]
    L = MLSTM_CHUNK
    nc = S // L
    q = q.reshape(B, H, nc, L, dk)
    k = (k * dk ** -0.5).reshape(B, H, nc, L, dk)
    v = v.reshape(B, H, nc, L, dv)
    ig = i_pre.reshape(B, H, nc, L)
    logf = jax.nn.log_sigmoid(f_pre).reshape(B, H, nc, L)
    b = jnp.cumsum(logf, axis=-1)
    b_last = b[..., -1]

    a = b_last[..., None] - b + ig
    m_loc = jnp.max(a, axis=-1)
    w = jnp.exp(a - m_loc[..., None])
    C_loc = jnp.einsum('bhclv,bhclk->bhcvk', v * w[..., None], k)
    n_loc = jnp.einsum('bhcl,bhclk->bhck', w, k)

    def step(carry, inp):
        C, n, m = carry
        bl, ml, Cl, nl = inp
        m_new = jnp.maximum(bl + m, ml)
        s_old = jnp.exp(bl + m - m_new)
        s_loc = jnp.exp(ml - m_new)
        C_new = s_old[..., None, None] * C + s_loc[..., None, None] * Cl
        n_new = s_old[..., None] * n + s_loc[..., None] * nl
        return (C_new, n_new, m_new), (C, n, m)

    init = (jnp.zeros((B, H, dv, dk), jnp.float32),
            jnp.zeros((B, H, dk), jnp.float32),
            jnp.zeros((B, H), jnp.float32))
    xs = (jnp.moveaxis(b_last, 2, 0), jnp.moveaxis(m_loc, 2, 0),
          jnp.moveaxis(C_loc, 2, 0), jnp.moveaxis(n_loc, 2, 0))
    _, (C_prev, n_prev, m_prev) = lax.scan(step, init, xs)
    C_prev = jnp.moveaxis(C_prev, 0, 2)
    n_prev = jnp.moveaxis(n_prev, 0, 2)
    m_prev = jnp.moveaxis(m_prev, 0, 2)

    causal = jnp.tril(jnp.ones((L, L), dtype=bool))
    D = b[..., :, None] - b[..., None, :] + ig[..., None, :]
    D = jnp.where(causal, D, -jnp.inf)
    inter_log = b + m_prev[..., None]
    m_t = jnp.maximum(inter_log, jnp.max(D, axis=-1))
    Dw = jnp.exp(D - m_t[..., None])
    inter_w = jnp.exp(inter_log - m_t)
    s = jnp.einsum('bhctd,bhcsd->bhcts', q, k) * Dw
    num = jnp.einsum('bhcts,bhcsv->bhctv', s, v) + \
        inter_w[..., None] * jnp.einsum('bhctk,bhcvk->bhctv', q, C_prev)
    den = jnp.sum(s, axis=-1) + inter_w * jnp.einsum('bhctk,bhck->bhct', q, n_prev)
    h = num / jnp.maximum(jnp.abs(den), jnp.exp(-m_t))[..., None]
    return h.reshape(B, H, S, dv)


def even_mixer(h, w_in, w_pool, pool_scale, lam_params, subln_g, w_out, lambda_init):
    B, S, _ = h.shape
    p = h @ w_in
    u = p[..., :POOL_WIDTH]
    o1 = POOL_WIDTH
    q = p[..., o1:o1 + DIFF_QK_WIDTH].reshape(B, S, DIFF_HEADS, 2, DIFF_HEAD_DIM)
    o2 = o1 + DIFF_QK_WIDTH
    k = p[..., o2:o2 + DIFF_QK_WIDTH].reshape(B, S, DIFF_HEADS, 2, DIFF_HEAD_DIM)
    o3 = o2 + DIFF_QK_WIDTH
    v = p[..., o3:o3 + DIFF_WIDTH].reshape(B, S, DIFF_HEADS, DIFF_V_DIM)
    y_pool = pool_mixer(u, w_pool, pool_scale)
    y_diff = diff_attention(q, k, v, lam_params, subln_g, lambda_init)
    y = jnp.concatenate([y_pool, y_diff.astype(y_pool.dtype)], axis=-1)
    return y @ w_out


def odd_mixer(h, w_in, b_gates, norm_g, w_out):
    B, S, _ = h.shape
    p = h @ w_in
    nq = MLSTM_QK_WIDTH
    nv = MLSTM_V_WIDTH
    q = p[..., :nq]
    k = p[..., nq:2 * nq]
    v = p[..., 2 * nq:2 * nq + nv]
    o = p[..., 2 * nq + nv:2 * nq + 2 * nv]
    gates = p[..., 2 * nq + 2 * nv:].astype(jnp.float32).reshape(B, S, 2, MLSTM_HEADS) + \
        b_gates.astype(jnp.float32)
    i_pre = gates[:, :, 0].transpose(0, 2, 1)
    f_pre = gates[:, :, 1].transpose(0, 2, 1)

    def heads(t, d):
        return t.astype(jnp.float32).reshape(B, S, MLSTM_HEADS, d).transpose(0, 2, 1, 3)

    ht = mlstm_chunkwise(heads(q, MLSTM_QK_DIM), heads(k, MLSTM_QK_DIM),
                         heads(v, MLSTM_V_DIM), i_pre, f_pre)
    ht = ht.transpose(0, 2, 1, 3)
    ht = rmsnorm(ht, norm_g.reshape(MLSTM_HEADS, MLSTM_V_DIM))
    y = jax.nn.sigmoid(o.astype(jnp.float32)) * ht.reshape(B, S, nv)
    return y.astype(h.dtype) @ w_out


def setup_inputs(seed: int = 0) -> dict:
    key = jax.random.key(seed)
    ks = jax.random.split(key, 24)
    f32 = jnp.float32

    def nrm(k, shape, fan_in, gain=1.0):
        return jax.random.normal(k, shape, f32) * (gain * fan_in ** -0.5)

    def near_one(k, shape):
        return 1.0 + 0.02 * jax.random.normal(k, shape, f32)

    x = jax.random.normal(ks[0], (BATCH, SEQ, D_MODEL), f32)
    c = jax.random.normal(ks[1], (BATCH, D_MODEL), f32)
    w_mod = nrm(ks[2], (DEPTH, D_MODEL, N_SUB * 3 * D_MODEL), D_MODEL, 0.5)
    b_mod = 0.02 * jax.random.normal(ks[3], (DEPTH, N_SUB * 3 * D_MODEL), f32)
    norm_g = near_one(ks[4], (DEPTH, N_SUB, D_MODEL))
    w_ffn_gate = nrm(ks[5], (DEPTH, 2, D_MODEL, D_FF), D_MODEL)
    w_ffn_up = nrm(ks[6], (DEPTH, 2, D_MODEL, D_FF), D_MODEL)
    w_ffn_down = nrm(ks[7], (DEPTH, 2, D_FF, D_MODEL), D_FF)
    w_in_even = nrm(ks[8], (N_EVEN, D_MODEL, EVEN_IN), D_MODEL)
    w_pool = nrm(ks[9], (N_EVEN, POOL_GROUPS, POOL_GROUP_DIM, POOL_GROUP_DIM), POOL_GROUP_DIM)
    pool_scale = near_one(ks[10], (N_EVEN, POOL_WIDTH))
    diff_lambda = 0.1 * jax.random.normal(ks[11], (N_EVEN, 4, DIFF_HEAD_DIM), f32)
    diff_subln_g = near_one(ks[12], (N_EVEN, DIFF_V_DIM))
    w_out_even = nrm(ks[13], (N_EVEN, EVEN_MIX, D_MODEL), EVEN_MIX)
    w_in_odd = nrm(ks[14], (N_ODD, D_MODEL, ODD_IN), D_MODEL)
    b_i = 0.1 * jax.random.normal(ks[15], (N_ODD, MLSTM_HEADS), f32)
    b_f = jnp.linspace(3.0, 6.0, MLSTM_HEADS, dtype=f32)[None, :] + \
        0.1 * jax.random.normal(ks[16], (N_ODD, MLSTM_HEADS), f32)
    b_gates_odd = jnp.stack([b_i, b_f], axis=1)
    mlstm_norm_g = near_one(ks[17], (N_ODD, ODD_MIX))
    w_out_odd = nrm(ks[18], (N_ODD, ODD_MIX, D_MODEL), ODD_MIX)
    final_g = near_one(ks[19], (D_MODEL,))
    return {"x": x, "c": c, "w_mod": w_mod, "b_mod": b_mod, "norm_g": norm_g,
            "w_ffn_gate": w_ffn_gate, "w_ffn_up": w_ffn_up, "w_ffn_down": w_ffn_down,
            "w_in_even": w_in_even, "w_pool": w_pool, "pool_scale": pool_scale,
            "diff_lambda": diff_lambda, "diff_subln_g": diff_subln_g, "w_out_even": w_out_even,
            "w_in_odd": w_in_odd, "b_gates_odd": b_gates_odd, "mlstm_norm_g": mlstm_norm_g,
            "w_out_odd": w_out_odd, "final_g": final_g}


def reference(x, c, w_mod, b_mod, norm_g, w_ffn_gate, w_ffn_up, w_ffn_down,
              w_in_even, w_pool, pool_scale, diff_lambda, diff_subln_g, w_out_even,
              w_in_odd, b_gates_odd, mlstm_norm_g, w_out_odd, final_g):
    B = x.shape[0]
    c_act = jax.nn.silu(c)
    for l in range(DEPTH):
        mod = (c_act @ w_mod[l] + b_mod[l]).reshape(B, N_SUB, 3, D_MODEL)
        shift, scale, gate = mod[:, :, 0], mod[:, :, 1], mod[:, :, 2]
        h = modnorm(x, norm_g[l, 0], shift[:, 0], scale[:, 0])
        x = x + 0.5 * gate[:, 0, None, :] * swiglu(h, w_ffn_gate[l, 0], w_ffn_up[l, 0], w_ffn_down[l, 0])
        h = modnorm(x, norm_g[l, 1], shift[:, 1], scale[:, 1])
        if l % 2 == 0:
            e = l // 2
            lambda_init = 0.8 - 0.6 * math.exp(-0.3 * l)
            y = even_mixer(h, w_in_even[e], w_pool[e], pool_scale[e], diff_lambda[e],
                           diff_subln_g[e], w_out_even[e], lambda_init)
        else:
            o = l // 2
            y = odd_mixer(h, w_in_odd[o], b_gates_odd[o], mlstm_norm_g[o], w_out_odd[o])
        x = x + gate[:, 1, None, :] * y
        h = modnorm(x, norm_g[l, 2], shift[:, 2], scale[:, 2])
        x = x + 0.5 * gate[:, 2, None, :] * swiglu(h, w_ffn_gate[l, 1], w_ffn_up[l, 1], w_ffn_down[l, 1])
    return rmsnorm(x, final_g)
```

```python
import functools
import math

import jax
import jax.numpy as jnp
from jax import lax
from jax.experimental import pallas as pl
from jax.experimental.pallas import tpu as pltpu

F32 = jnp.float32
BF16 = jnp.bfloat16

EPS = 1e-6
N_SUB = 3
POOL_WINDOWS = (2, 4, 8, 16)
POOL_HALO = 16
DIFF_HEADS = 4
DIFF_HEAD_DIM = 64
MLSTM_HEADS = 4
MLSTM_QK_DIM = 128
MLSTM_V_DIM = 256
MLSTM_CHUNK = 128
LANES = 128

VMEM_LIMIT_BYTES = 56 * 1024 * 1024

FFN_ROWS = 512
PROJ_ROWS = 512
ATTN_Q_ROWS = 256
ATTN_K_ROWS = 256
MLSTM_CHUNKS_PER_STEP = 8


def _params(*semantics):
    return pltpu.CompilerParams(dimension_semantics=semantics,
                                vmem_limit_bytes=VMEM_LIMIT_BYTES)


def _resident(block_shape, index_map):
    return pl.BlockSpec(block_shape, index_map, pipeline_mode=pl.Buffered(1))


def _dot(a, b):
    return jnp.dot(a, b, preferred_element_type=F32)


def _modnorm(x, g, mod, sub):
    shift = mod[3 * sub:3 * sub + 1]
    scale = mod[3 * sub + 1:3 * sub + 2]
    y = x * lax.rsqrt(jnp.mean(x * x, axis=-1, keepdims=True) + EPS)
    return (y * g) * (1.0 + scale) + shift


def _gate(mod, sub):
    return mod[3 * sub + 2:3 * sub + 3]


def _mod_kernel(c_ref, w_ref, b_ref, o_ref):
    c = c_ref[...]
    c_act = c * jax.nn.sigmoid(c)
    o_ref[0] = jnp.dot(c_act, w_ref[0], precision=lax.Precision.HIGHEST,
                       preferred_element_type=F32) + b_ref[0]


def _modulation(c, w_mod, b_mod):
    depth, d, n = w_mod.shape
    b = c.shape[0]
    tn = d
    out = pl.pallas_call(
        _mod_kernel,
        grid=(depth, n // tn),
        in_specs=[pl.BlockSpec((b, d), lambda l, j: (0, 0)),
                  pl.BlockSpec((1, d, tn), lambda l, j: (l, 0, j)),
                  pl.BlockSpec((1, 1, tn), lambda l, j: (l, 0, j))],
        out_specs=pl.BlockSpec((1, b, tn), lambda l, j: (l, 0, j)),
        out_shape=jax.ShapeDtypeStruct((depth, b, n), F32),
        compiler_params=_params("arbitrary", "arbitrary"),
        name="modulation",
    )(c, w_mod, b_mod.reshape(depth, 1, n))
    return out.reshape(depth, b, 3 * N_SUB, d)


def _ffn_kernel(x_ref, mod_ref, g_ref, wg_ref, wu_ref, wd_ref, fg_ref, o_ref, *, sub, final_norm):
    x = x_ref[0]
    mod = mod_ref[0, 0]
    h = _modnorm(x, g_ref[...], mod, sub).astype(BF16)
    gate_act = _dot(h, wg_ref[...])
    up = _dot(h, wu_ref[...])
    a = (gate_act * jax.nn.sigmoid(gate_act) * up).astype(BF16)
    y = _dot(a, wd_ref[...])
    out = x + (0.5 * _gate(mod, sub)) * y
    if final_norm:
        out = out * lax.rsqrt(jnp.mean(out * out, axis=-1, keepdims=True) + EPS) * fg_ref[...]
    o_ref[0] = out


def _ffn(x, mod, layer, sub, g, w_gate, w_up, w_down, final_g, final_norm):
    b, s, d = x.shape
    f = w_gate.shape[1]
    tm = FFN_ROWS
    return pl.pallas_call(
        functools.partial(_ffn_kernel, sub=sub, final_norm=final_norm),
        grid=(b, s // tm),
        in_specs=[pl.BlockSpec((1, tm, d), lambda bi, i: (bi, i, 0)),
                  pl.BlockSpec((1, 1, 3 * N_SUB, d), lambda bi, i: (layer, bi, 0, 0)),
                  pl.BlockSpec((1, d), lambda bi, i: (0, 0)),
                  _resident((d, f), lambda bi, i: (0, 0)),
                  _resident((d, f), lambda bi, i: (0, 0)),
                  _resident((f, d), lambda bi, i: (0, 0)),
                  pl.BlockSpec((1, d), lambda bi, i: (0, 0))],
        out_specs=pl.BlockSpec((1, tm, d), lambda bi, i: (bi, i, 0)),
        out_shape=jax.ShapeDtypeStruct((b, s, d), F32),
        compiler_params=_params("arbitrary", "arbitrary"),
        name="ffn",
    )(x, mod, g.reshape(1, d), w_gate.astype(BF16), w_up.astype(BF16), w_down.astype(BF16),
      final_g.reshape(1, d))


def _even_in_kernel(x_ref, mod_ref, g_ref, w_ref, u_ref, q_ref, k_ref, v_ref, *, sub):
    h = _modnorm(x_ref[0], g_ref[...], mod_ref[0, 0], sub).astype(BF16)
    p = _dot(h, w_ref[...])
    pw = u_ref.shape[-1]
    qw = q_ref.shape[-1]
    u_ref[0] = p[:, :pw]
    q_ref[0] = (p[:, pw:pw + qw] * (DIFF_HEAD_DIM ** -0.5)).astype(BF16)
    k_ref[0] = p[:, pw + qw:pw + 2 * qw].astype(BF16)
    v_ref[0] = p[:, pw + 2 * qw:].astype(BF16)


def _even_in_proj(x, mod, layer, g, w_in, pool_width, qk_width):
    b, s, d = x.shape
    n = w_in.shape[1]
    v_width = n - pool_width - 2 * qk_width
    tm = PROJ_ROWS
    row = lambda bi, i: (bi, i, 0)
    return pl.pallas_call(
        functools.partial(_even_in_kernel, sub=1),
        grid=(b, s // tm),
        in_specs=[pl.BlockSpec((1, tm, d), row),
                  pl.BlockSpec((1, 1, 3 * N_SUB, d), lambda bi, i: (layer, bi, 0, 0)),
                  pl.BlockSpec((1, d), lambda bi, i: (0, 0)),
                  _resident((d, n), lambda bi, i: (0, 0))],
        out_specs=[pl.BlockSpec((1, tm, pool_width), row),
                   pl.BlockSpec((1, tm, qk_width), row),
                   pl.BlockSpec((1, tm, qk_width), row),
                   pl.BlockSpec((1, tm, v_width), row)],
        out_shape=[jax.ShapeDtypeStruct((b, s, pool_width), F32),
                   jax.ShapeDtypeStruct((b, s, qk_width), BF16),
                   jax.ShapeDtypeStruct((b, s, qk_width), BF16),
                   jax.ShapeDtypeStruct((b, s, v_width), BF16)],
        compiler_params=_params("arbitrary", "arbitrary"),
        name="even_in_proj",
    )(x, mod, g.reshape(1, d), w_in.astype(BF16))


def _diff_attn_kernel(q_ref, k_ref, v_ref, lam_ref, sg_ref, o_ref, *, lambda_init, tq, tk):
    i = pl.program_id(2)
    q = q_ref[0]
    lane = lax.broadcasted_iota(jnp.int32, q.shape, 1)
    zero = jnp.zeros_like(q)
    qs = jnp.concatenate([jnp.where(lane < DIFF_HEAD_DIM, q, zero),
                          jnp.where(lane >= DIFF_HEAD_DIM, q, zero)], axis=0)
    def step(j, carry, masked):
        m, l, acc = carry
        start = pl.multiple_of(j * tk, tk)
        kb = k_ref[0, pl.ds(start, tk), :]
        vb = v_ref[0, pl.ds(start, tk), :]
        s = lax.dot_general(qs, kb, (((1,), (1,)), ((), ())), preferred_element_type=F32)
        if masked:
            row = lax.broadcasted_iota(jnp.int32, (2 * tq, tk), 0)
            row = jnp.where(row >= tq, row - tq, row) + i * tq
            col = lax.broadcasted_iota(jnp.int32, (2 * tq, tk), 1) + j * tk
            s = jnp.where(col <= row, s, -jnp.inf)
        m_new = jnp.maximum(m, jnp.max(s, axis=1, keepdims=True))
        alpha = jnp.exp(m - m_new)
        p = jnp.exp(s - m_new)
        l = alpha * l + jnp.sum(p, axis=1, keepdims=True)
        acc = alpha * acc + _dot(p.astype(BF16), vb)
        return m_new, l, acc

    init = (jnp.full((2 * tq, 1), -jnp.inf, F32), jnp.zeros((2 * tq, 1), F32),
            jnp.zeros((2 * tq, v_ref.shape[-1]), F32))
    n_full = i * (tq // tk)
    carry = lax.fori_loop(0, n_full, lambda j, c: step(j, c, False), init)
    for d in range(tq // tk):
        carry = step(n_full + d, carry, True)
    _, l, acc = carry
    o = acc / l
    lp = lam_ref[...]
    lam = (jnp.exp(jnp.sum(lp[0:1] * lp[1:2], axis=1, keepdims=True))
           - jnp.exp(jnp.sum(lp[2:3] * lp[3:4], axis=1, keepdims=True)) + lambda_init)
    o = o[:tq] - lam * o[tq:]
    o = o * lax.rsqrt(jnp.mean(o * o, axis=-1, keepdims=True) + EPS) * sg_ref[...]
    o_ref[0] = (o * (1.0 - lambda_init)).astype(o_ref.dtype)


def _diff_attention(q, k, v, lam_params, subln_g, lambda_init):
    b, s, _ = q.shape
    hd = 2 * DIFF_HEAD_DIM
    tq, tk = ATTN_Q_ROWS, ATTN_K_ROWS
    assert tq % tk == 0 and s % tq == 0
    return pl.pallas_call(
        functools.partial(_diff_attn_kernel, lambda_init=lambda_init, tq=tq, tk=tk),
        grid=(b, DIFF_HEADS, s // tq),
        in_specs=[pl.BlockSpec((1, tq, hd), lambda bi, h, i: (bi, i, h)),
                  pl.BlockSpec((1, s, hd), lambda bi, h, i: (bi, 0, h)),
                  pl.BlockSpec((1, s, hd), lambda bi, h, i: (bi, 0, h)),
                  pl.BlockSpec(lam_params.shape, lambda bi, h, i: (0, 0)),
                  pl.BlockSpec((1, hd), lambda bi, h, i: (0, 0))],
        out_specs=pl.BlockSpec((1, tq, hd), lambda bi, h, i: (bi, i, h)),
        out_shape=jax.ShapeDtypeStruct((b, s, DIFF_HEADS * hd), BF16),
        compiler_params=_params("arbitrary", "arbitrary", "arbitrary"),
        name="diff_attention",
    )(q, k, v, lam_params, subln_g.reshape(1, hd))


def _even_out_kernel(x_ref, mod_ref, u_ref, halo_ref, yd_ref, wp_ref, ps_ref, wo_ref, o_ref, *, sub):
    i = pl.program_id(1)
    tm = u_ref.shape[1]
    u = u_ref[0]
    halo = halo_ref[0] * (i > 0).astype(F32)
    ext = jnp.concatenate([halo, u], axis=0)
    pos = i * tm + lax.broadcasted_iota(jnp.int32, (tm, 1), 0)
    gw = wp_ref.shape[-1]
    sums = ext
    shift = 1
    parts = []
    for g, w in enumerate(POOL_WINDOWS):
        while shift < w:
            sums = sums + pltpu.roll(sums, shift, 0)
            shift *= 2
        total = sums[POOL_HALO:, g * gw:(g + 1) * gw]
        count = jnp.minimum(pos + 1, w).astype(F32)
        pooled = total / count - u[:, g * gw:(g + 1) * gw]
        parts.append(_dot(pooled.astype(BF16), wp_ref[g]))
    y_pool = (jnp.concatenate(parts, axis=1) * ps_ref[...]).astype(BF16)
    pw = y_pool.shape[1]
    y = _dot(y_pool, wo_ref[:pw, :]) + _dot(yd_ref[0], wo_ref[pw:, :])
    o_ref[0] = x_ref[0] + _gate(mod_ref[0, 0], sub) * y


def _even_out_proj(x, mod, layer, u, y_diff, w_pool, pool_scale, w_out):
    b, s, d = x.shape
    pw = u.shape[-1]
    dw = y_diff.shape[-1]
    tm = PROJ_ROWS
    row = lambda bi, i: (bi, i, 0)
    halo_blocks = tm // POOL_HALO
    return pl.pallas_call(
        functools.partial(_even_out_kernel, sub=1),
        grid=(b, s // tm),
        in_specs=[pl.BlockSpec((1, tm, d), row),
                  pl.BlockSpec((1, 1, 3 * N_SUB, d), lambda bi, i: (layer, bi, 0, 0)),
                  pl.BlockSpec((1, tm, pw), row),
                  pl.BlockSpec((1, POOL_HALO, pw),
                               lambda bi, i: (bi, jnp.maximum(i * halo_blocks - 1, 0), 0)),
                  pl.BlockSpec((1, tm, dw), row),
                  _resident(w_pool.shape, lambda bi, i: (0, 0, 0)),
                  pl.BlockSpec((1, pw), lambda bi, i: (0, 0)),
                  _resident(w_out.shape, lambda bi, i: (0, 0))],
        out_specs=pl.BlockSpec((1, tm, d), row),
        out_shape=jax.ShapeDtypeStruct((b, s, d), F32),
        compiler_params=_params("arbitrary", "arbitrary"),
        name="even_out_proj",
    )(x, mod, u, u, y_diff, w_pool.astype(BF16), pool_scale.reshape(1, pw), w_out.astype(BF16))


def _odd_in_kernel(x_ref, mod_ref, g_ref, w_ref, q_ref, k_ref, v_ref, o_ref, gates_ref, *, sub):
    h = _modnorm(x_ref[0], g_ref[...], mod_ref[0, 0], sub).astype(BF16)
    p = _dot(h, w_ref[...])
    nq = q_ref.shape[-1]
    nv = v_ref.shape[-1]
    q_ref[0] = p[:, :nq].astype(BF16)
    k_ref[0] = (p[:, nq:2 * nq] * (MLSTM_QK_DIM ** -0.5)).astype(BF16)
    v_ref[0] = p[:, 2 * nq:2 * nq + nv].astype(BF16)
    o_ref[0] = p[:, 2 * nq + nv:2 * nq + 2 * nv].astype(BF16)
    gates_ref[0] = p[:, 2 * nq + 2 * nv:]


def _odd_in_proj(x, mod, layer, g, w_in, nq, nv):
    b, s, d = x.shape
    n_gates = w_in.shape[1] - 2 * nq - 2 * nv
    w = jnp.pad(w_in, ((0, 0), (0, LANES - n_gates))).astype(BF16)
    n = w.shape[1]
    tm = PROJ_ROWS
    row = lambda bi, i: (bi, i, 0)
    return pl.pallas_call(
        functools.partial(_odd_in_kernel, sub=1),
        grid=(b, s // tm),
        in_specs=[pl.BlockSpec((1, tm, d), row),
                  pl.BlockSpec((1, 1, 3 * N_SUB, d), lambda bi, i: (layer, bi, 0, 0)),
                  pl.BlockSpec((1, d), lambda bi, i: (0, 0)),
                  _resident((d, n), lambda bi, i: (0, 0))],
        out_specs=[pl.BlockSpec((1, tm, nq), row),
                   pl.BlockSpec((1, tm, nq), row),
                   pl.BlockSpec((1, tm, nv), row),
                   pl.BlockSpec((1, tm, nv), row),
                   pl.BlockSpec((1, tm, LANES), row)],
        out_shape=[jax.ShapeDtypeStruct((b, s, nq), BF16),
                   jax.ShapeDtypeStruct((b, s, nq), BF16),
                   jax.ShapeDtypeStruct((b, s, nv), BF16),
                   jax.ShapeDtypeStruct((b, s, nv), BF16),
                   jax.ShapeDtypeStruct((b, s, LANES), F32)],
        compiler_params=_params("arbitrary", "arbitrary"),
        name="odd_in_proj",
    )(x, mod, g.reshape(1, d), w)


def _log_sigmoid(x):
    return jnp.minimum(x, 0.0) - jnp.log1p(jnp.exp(-jnp.abs(x)))


def _lane_cumsum(x):
    lane = lax.broadcasted_iota(jnp.int32, x.shape, 1)
    shift = 1
    while shift < x.shape[-1]:
        x = x + jnp.where(lane >= shift, pltpu.roll(x, shift, 1), 0.0)
        shift *= 2
    return x


def _mlstm_kernel(bg_ref, q_ref, k_ref, v_ref, og_ref, ig_ref, fg_ref, ng_ref, y_ref,
                  c_ref, n_ref, m_ref, *, chunks):
    head = pl.program_id(1)
    L = MLSTM_CHUNK

    @pl.when(pl.program_id(2) == 0)
    def _():
        c_ref[...] = jnp.zeros_like(c_ref)
        n_ref[...] = jnp.zeros_like(n_ref)
        m_ref[...] = jnp.zeros_like(m_ref)

    i_all = ig_ref[0, 0] + bg_ref[head]
    logf_all = _log_sigmoid(fg_ref[0, 0] + bg_ref[MLSTM_HEADS + head])
    b_all = _lane_cumsum(logf_all)
    row_id = lax.broadcasted_iota(jnp.int32, (L, L), 0)
    col_id = lax.broadcasted_iota(jnp.int32, (L, L), 1)
    causal = col_id <= row_id

    def chunk(c, carry):
        rows = pl.ds(pl.multiple_of(c * L, L), L)
        q = q_ref[0, rows, :]
        k = k_ref[0, rows, :]
        v = v_ref[0, rows, :]
        sel = lax.broadcasted_iota(jnp.int32, b_all.shape, 0) == c
        b_row = jnp.sum(jnp.where(sel, b_all, 0.0), axis=0, keepdims=True)
        i_row = jnp.sum(jnp.where(sel, i_all, 0.0), axis=0, keepdims=True)
        b_last = b_row[:, L - 1:L]
        b_rows = jnp.broadcast_to(b_row, (L, L))
        i_rows = jnp.broadcast_to(i_row, (L, L))
        b_col = b_rows.T[:, :1]
        i_col = i_rows.T[:, :1]
        m_prev = m_ref[...]

        d = jnp.where(causal, b_col - b_rows + i_rows, -jnp.inf)
        inter_log = b_col + m_prev
        m_t = jnp.maximum(inter_log, jnp.max(d, axis=1, keepdims=True))
        dw = jnp.exp(d - m_t)
        inter_w = jnp.exp(inter_log - m_t)
        s = lax.dot_general(q, k, (((1,), (1,)), ((), ())), preferred_element_type=F32) * dw
        c_prev = c_ref[...]
        num = _dot(s.astype(BF16), v) + inter_w * _dot(q, c_prev.astype(BF16))
        den = (jnp.sum(s, axis=1, keepdims=True)
               + inter_w * jnp.sum(q.astype(F32) * n_ref[...], axis=1, keepdims=True))
        h = num / jnp.maximum(jnp.abs(den), jnp.exp(-m_t))

        hn = h * lax.rsqrt(jnp.mean(h * h, axis=-1, keepdims=True) + EPS) * ng_ref[...]
        y_ref[0, rows, :] = (jax.nn.sigmoid(og_ref[0, rows, :].astype(F32)) * hn).astype(y_ref.dtype)

        a_col = b_last - b_col + i_col
        m_loc = jnp.max(a_col, axis=0, keepdims=True)
        kw = k.astype(F32) * jnp.exp(a_col - m_loc)
        c_loc = _dot(kw.T.astype(BF16), v)
        n_loc = jnp.sum(kw, axis=0, keepdims=True)
        m_new = jnp.maximum(b_last + m_prev, m_loc)
        s_old = jnp.exp(b_last + m_prev - m_new)
        s_loc = jnp.exp(m_loc - m_new)
        c_ref[...] = s_old * c_prev + s_loc * c_loc
        n_ref[...] = s_old * n_ref[...] + s_loc * n_loc
        m_ref[...] = m_new
        return carry

    lax.fori_loop(0, chunks, chunk, 0)


def _mlstm(q, k, v, o, gates, b_gates, norm_g):
    b, s, _ = q.shape
    dk, dv, L = MLSTM_QK_DIM, MLSTM_V_DIM, MLSTM_CHUNK
    nc = s // L
    g = MLSTM_CHUNKS_PER_STEP
    rows = g * L
    ig = gates[..., :MLSTM_HEADS].transpose(0, 2, 1).reshape(b, MLSTM_HEADS, nc, L)
    fg = gates[..., MLSTM_HEADS:2 * MLSTM_HEADS].transpose(0, 2, 1).reshape(b, MLSTM_HEADS, nc, L)
    gate_spec = pl.BlockSpec((1, 1, g, L), lambda bi, h, c: (bi, h, c, 0))
    return pl.pallas_call(
        functools.partial(_mlstm_kernel, chunks=g),
        grid=(b, MLSTM_HEADS, nc // g),
        in_specs=[pl.BlockSpec(memory_space=pltpu.SMEM),
                  pl.BlockSpec((1, rows, dk), lambda bi, h, c: (bi, c, h)),
                  pl.BlockSpec((1, rows, dk), lambda bi, h, c: (bi, c, h)),
                  pl.BlockSpec((1, rows, dv), lambda bi, h, c: (bi, c, h)),
                  pl.BlockSpec((1, rows, dv), lambda bi, h, c: (bi, c, h)),
                  gate_spec, gate_spec,
                  pl.BlockSpec((1, dv), lambda bi, h, c: (0, h))],
        out_specs=pl.BlockSpec((1, rows, dv), lambda bi, h, c: (bi, c, h)),
        out_shape=jax.ShapeDtypeStruct((b, s, MLSTM_HEADS * dv), BF16),
        scratch_shapes=[pltpu.VMEM((dk, dv), F32), pltpu.VMEM((1, dk), F32),
                        pltpu.VMEM((1, 1), F32)],
        compiler_params=_params("arbitrary", "arbitrary", "arbitrary"),
        name="mlstm",
    )(b_gates.reshape(-1), q, k, v, o, ig, fg, norm_g.reshape(1, -1))


def _out_proj_kernel(x_ref, mod_ref, y_ref, w_ref, o_ref, *, sub):
    o_ref[0] = x_ref[0] + _gate(mod_ref[0, 0], sub) * _dot(y_ref[0], w_ref[...])


def _out_proj(x, mod, layer, y, w_out):
    b, s, d = x.shape
    n_in = y.shape[-1]
    tm = PROJ_ROWS
    row = lambda bi, i: (bi, i, 0)
    return pl.pallas_call(
        functools.partial(_out_proj_kernel, sub=1),
        grid=(b, s // tm),
        in_specs=[pl.BlockSpec((1, tm, d), row),
                  pl.BlockSpec((1, 1, 3 * N_SUB, d), lambda bi, i: (layer, bi, 0, 0)),
                  pl.BlockSpec((1, tm, n_in), row),
                  _resident((n_in, d), lambda bi, i: (0, 0))],
        out_specs=pl.BlockSpec((1, tm, d), row),
        out_shape=jax.ShapeDtypeStruct((b, s, d), F32),
        compiler_params=_params("arbitrary", "arbitrary"),
        name="out_proj",
    )(x, mod, y, w_out.astype(BF16))


def kernel(x, c, w_mod, b_mod, norm_g, w_ffn_gate, w_ffn_up, w_ffn_down, w_in_even, w_pool, pool_scale, diff_lambda, diff_subln_g, w_out_even, w_in_odd, b_gates_odd, mlstm_norm_g, w_out_odd, final_g):
    depth = w_mod.shape[0]
    mod = _modulation(c, w_mod, b_mod)
    pool_width = pool_scale.shape[-1]
    qk_width = DIFF_HEADS * 2 * DIFF_HEAD_DIM
    for l in range(depth):
        x = _ffn(x, mod, l, 0, norm_g[l, 0], w_ffn_gate[l, 0], w_ffn_up[l, 0], w_ffn_down[l, 0],
                 final_g, False)
        if l % 2 == 0:
            e = l // 2
            lambda_init = 0.8 - 0.6 * math.exp(-0.3 * l)
            u, q, k, v = _even_in_proj(x, mod, l, norm_g[l, 1], w_in_even[e], pool_width, qk_width)
            y_diff = _diff_attention(q, k, v, diff_lambda[e], diff_subln_g[e], lambda_init)
            x = _even_out_proj(x, mod, l, u, y_diff, w_pool[e], pool_scale[e], w_out_even[e])
        else:
            o = l // 2
            q, k, v, og, gates = _odd_in_proj(x, mod, l, norm_g[l, 1], w_in_odd[o],
                                              MLSTM_HEADS * MLSTM_QK_DIM, MLSTM_HEADS * MLSTM_V_DIM)
            y = _mlstm(q, k, v, og, gates, b_gates_odd[o], mlstm_norm_g[o])
            x = _out_proj(x, mod, l, y, w_out_odd[o])
        x = _ffn(x, mod, l, 2, norm_g[l, 2], w_ffn_gate[l, 1], w_ffn_up[l, 1], w_ffn_down[l, 1],
                 final_g, l == depth - 1)
    return x
```

```python
import functools
import math

import jax
import jax.numpy as jnp
from jax import lax
from jax.experimental import pallas as pl
from jax.experimental.pallas import tpu as pltpu

F32 = jnp.float32
BF16 = jnp.bfloat16

EPS = 1e-6
N_SUB = 3
POOL_WINDOWS = (2, 4, 8, 16)
POOL_HALO = 16
DIFF_HEADS = 4
DIFF_HEAD_DIM = 64
MLSTM_HEADS = 4
MLSTM_QK_DIM = 128
MLSTM_V_DIM = 256
MLSTM_CHUNK = 128
LANES = 128

VMEM_LIMIT_BYTES = 56 * 1024 * 1024

FFN_ROWS = 512
PROJ_ROWS = 512
ATTN_Q_ROWS = 512
ATTN_K_ROWS = 512
MLSTM_CHUNKS_PER_STEP = 8


def _params(*semantics):
    return pltpu.CompilerParams(dimension_semantics=semantics,
                                vmem_limit_bytes=VMEM_LIMIT_BYTES)


def _resident(block_shape, index_map):
    return pl.BlockSpec(block_shape, index_map, pipeline_mode=pl.Buffered(1))


def _dot(a, b):
    return jnp.dot(a, b, preferred_element_type=F32)


def _modnorm(x, g, mod, sub):
    shift = mod[3 * sub:3 * sub + 1]
    scale = mod[3 * sub + 1:3 * sub + 2]
    y = x * lax.rsqrt(jnp.mean(x * x, axis=-1, keepdims=True) + EPS)
    return (y * g) * (1.0 + scale) + shift


def _gate(mod, sub):
    return mod[3 * sub + 2:3 * sub + 3]


def _mod_kernel(c_ref, w_ref, b_ref, o_ref):
    c = c_ref[...]
    c_act = c * jax.nn.sigmoid(c)
    o_ref[0] = jnp.dot(c_act, w_ref[0], precision=lax.Precision.HIGHEST,
                       preferred_element_type=F32) + b_ref[0]


def _modulation(c, w_mod, b_mod):
    depth, d, n = w_mod.shape
    b = c.shape[0]
    tn = d
    out = pl.pallas_call(
        _mod_kernel,
        grid=(depth, n // tn),
        in_specs=[pl.BlockSpec((b, d), lambda l, j: (0, 0)),
                  pl.BlockSpec((1, d, tn), lambda l, j: (l, 0, j)),
                  pl.BlockSpec((1, 1, tn), lambda l, j: (l, 0, j))],
        out_specs=pl.BlockSpec((1, b, tn), lambda l, j: (l, 0, j)),
        out_shape=jax.ShapeDtypeStruct((depth, b, n), F32),
        compiler_params=_params("arbitrary", "arbitrary"),
        name="modulation",
    )(c, w_mod, b_mod.reshape(depth, 1, n))
    return out.reshape(depth, b, 3 * N_SUB, d)


def _ffn_kernel(x_ref, mod_ref, g_ref, wg_ref, wu_ref, wd_ref, fg_ref, o_ref, *, sub, final_norm):
    x = x_ref[0]
    mod = mod_ref[0, 0]
    h = _modnorm(x, g_ref[...], mod, sub).astype(BF16)
    gate_act = _dot(h, wg_ref[...])
    up = _dot(h, wu_ref[...])
    a = (gate_act * jax.nn.sigmoid(gate_act) * up).astype(BF16)
    y = _dot(a, wd_ref[...])
    out = x + (0.5 * _gate(mod, sub)) * y
    if final_norm:
        out = out * lax.rsqrt(jnp.mean(out * out, axis=-1, keepdims=True) + EPS) * fg_ref[...]
    o_ref[0] = out


def _ffn(x, mod, layer, sub, g, w_gate, w_up, w_down, final_g, final_norm):
    b, s, d = x.shape
    f = w_gate.shape[1]
    tm = FFN_ROWS
    return pl.pallas_call(
        functools.partial(_ffn_kernel, sub=sub, final_norm=final_norm),
        grid=(b, s // tm),
        in_specs=[pl.BlockSpec((1, tm, d), lambda bi, i: (bi, i, 0)),
                  pl.BlockSpec((1, 1, 3 * N_SUB, d), lambda bi, i: (layer, bi, 0, 0)),
                  pl.BlockSpec((1, d), lambda bi, i: (0, 0)),
                  _resident((d, f), lambda bi, i: (0, 0)),
                  _resident((d, f), lambda bi, i: (0, 0)),
                  _resident((f, d), lambda bi, i: (0, 0)),
                  pl.BlockSpec((1, d), lambda bi, i: (0, 0))],
        out_specs=pl.BlockSpec((1, tm, d), lambda bi, i: (bi, i, 0)),
        out_shape=jax.ShapeDtypeStruct((b, s, d), F32),
        compiler_params=_params("arbitrary", "arbitrary"),
        name="ffn",
    )(x, mod, g.reshape(1, d), w_gate.astype(BF16), w_up.astype(BF16), w_down.astype(BF16),
      final_g.reshape(1, d))


def _even_in_kernel(x_ref, mod_ref, g_ref, w_ref, u_ref, q_ref, k_ref, v_ref, *, sub):
    h = _modnorm(x_ref[0], g_ref[...], mod_ref[0, 0], sub).astype(BF16)
    p = _dot(h, w_ref[...])
    pw = u_ref.shape[-1]
    qw = q_ref.shape[-1]
    u_ref[0] = p[:, :pw]
    q_ref[0] = (p[:, pw:pw + qw] * (DIFF_HEAD_DIM ** -0.5)).astype(BF16)
    k_ref[0] = p[:, pw + qw:pw + 2 * qw].astype(BF16)
    v_ref[0] = p[:, pw + 2 * qw:].astype(BF16)


def _even_in_proj(x, mod, layer, g, w_in, pool_width, qk_width):
    b, s, d = x.shape
    n = w_in.shape[1]
    v_width = n - pool_width - 2 * qk_width
    tm = PROJ_ROWS
    row = lambda bi, i: (bi, i, 0)
    return pl.pallas_call(
        functools.partial(_even_in_kernel, sub=1),
        grid=(b, s // tm),
        in_specs=[pl.BlockSpec((1, tm, d), row),
                  pl.BlockSpec((1, 1, 3 * N_SUB, d), lambda bi, i: (layer, bi, 0, 0)),
                  pl.BlockSpec((1, d), lambda bi, i: (0, 0)),
                  _resident((d, n), lambda bi, i: (0, 0))],
        out_specs=[pl.BlockSpec((1, tm, pool_width), row),
                   pl.BlockSpec((1, tm, qk_width), row),
                   pl.BlockSpec((1, tm, qk_width), row),
                   pl.BlockSpec((1, tm, v_width), row)],
        out_shape=[jax.ShapeDtypeStruct((b, s, pool_width), F32),
                   jax.ShapeDtypeStruct((b, s, qk_width), BF16),
                   jax.ShapeDtypeStruct((b, s, qk_width), BF16),
                   jax.ShapeDtypeStruct((b, s, v_width), BF16)],
        compiler_params=_params("arbitrary", "arbitrary"),
        name="even_in_proj",
    )(x, mod, g.reshape(1, d), w_in.astype(BF16))


def _diff_attn_kernel(q_ref, k_ref, v_ref, lam_ref, sg_ref, o_ref, *, lambda_init, tq, tk):
    i = pl.program_id(2)
    q = q_ref[0]
    lane = lax.broadcasted_iota(jnp.int32, q.shape, 1)
    zero = jnp.zeros_like(q)
    qs = jnp.concatenate([jnp.where(lane < DIFF_HEAD_DIM, q, zero),
                          jnp.where(lane >= DIFF_HEAD_DIM, q, zero)], axis=0)
    def step(j, carry, masked):
        m, l, acc = carry
        start = pl.multiple_of(j * tk, tk)
        kb = k_ref[0, pl.ds(start, tk), :]
        vb = v_ref[0, pl.ds(start, tk), :]
        s = lax.dot_general(qs, kb, (((1,), (1,)), ((), ())), preferred_element_type=F32)
        if masked:
            row = lax.broadcasted_iota(jnp.int32, (2 * tq, tk), 0)
            row = jnp.where(row >= tq, row - tq, row) + i * tq
            col = lax.broadcasted_iota(jnp.int32, (2 * tq, tk), 1) + j * tk
            s = jnp.where(col <= row, s, -jnp.inf)
        m_new = jnp.maximum(m, jnp.max(s, axis=1, keepdims=True))
        alpha = jnp.exp(m - m_new)
        p = jnp.exp(s - m_new)
        l = alpha * l + jnp.sum(p, axis=1, keepdims=True)
        acc = alpha * acc + _dot(p.astype(BF16), vb)
        return m_new, l, acc

    init = (jnp.full((2 * tq, 1), -jnp.inf, F32), jnp.zeros((2 * tq, 1), F32),
            jnp.zeros((2 * tq, v_ref.shape[-1]), F32))
    n_full = i * (tq // tk)
    carry = lax.fori_loop(0, n_full, lambda j, c: step(j, c, False), init)
    for d in range(tq // tk):
        carry = step(n_full + d, carry, True)
    _, l, acc = carry
    o = acc / l
    lp = lam_ref[...]
    lam = (jnp.exp(jnp.sum(lp[0:1] * lp[1:2], axis=1, keepdims=True))
           - jnp.exp(jnp.sum(lp[2:3] * lp[3:4], axis=1, keepdims=True)) + lambda_init)
    o = o[:tq] - lam * o[tq:]
    o = o * lax.rsqrt(jnp.mean(o * o, axis=-1, keepdims=True) + EPS) * sg_ref[...]
    o_ref[0] = (o * (1.0 - lambda_init)).astype(o_ref.dtype)


def _diff_attention(q, k, v, lam_params, subln_g, lambda_init):
    b, s, _ = q.shape
    hd = 2 * DIFF_HEAD_DIM
    tq, tk = ATTN_Q_ROWS, ATTN_K_ROWS
    assert tq % tk == 0 and s % tq == 0
    return pl.pallas_call(
        functools.partial(_diff_attn_kernel, lambda_init=lambda_init, tq=tq, tk=tk),
        grid=(b, DIFF_HEADS, s // tq),
        in_specs=[pl.BlockSpec((1, tq, hd), lambda bi, h, i: (bi, i, h)),
                  pl.BlockSpec((1, s, hd), lambda bi, h, i: (bi, 0, h)),
                  pl.BlockSpec((1, s, hd), lambda bi, h, i: (bi, 0, h)),
                  pl.BlockSpec(lam_params.shape, lambda bi, h, i: (0, 0)),
                  pl.BlockSpec((1, hd), lambda bi, h, i: (0, 0))],
        out_specs=pl.BlockSpec((1, tq, hd), lambda bi, h, i: (bi, i, h)),
        out_shape=jax.ShapeDtypeStruct((b, s, DIFF_HEADS * hd), BF16),
        compiler_params=_params("arbitrary", "arbitrary", "arbitrary"),
        name="diff_attention",
    )(q, k, v, lam_params, subln_g.reshape(1, hd))


def _even_out_kernel(x_ref, mod_ref, u_ref, halo_ref, yd_ref, wp_ref, ps_ref, wo_ref, o_ref, *, sub):
    i = pl.program_id(1)
    tm = u_ref.shape[1]
    u = u_ref[0]
    halo = halo_ref[0] * (i > 0).astype(F32)
    ext = jnp.concatenate([halo, u], axis=0)
    pos = i * tm + lax.broadcasted_iota(jnp.int32, (tm, 1), 0)
    gw = wp_ref.shape[-1]
    sums = ext
    shift = 1
    parts = []
    for g, w in enumerate(POOL_WINDOWS):
        while shift < w:
            sums = sums + pltpu.roll(sums, shift, 0)
            shift *= 2
        total = sums[POOL_HALO:, g * gw:(g + 1) * gw]
        count = jnp.minimum(pos + 1, w).astype(F32)
        pooled = total / count - u[:, g * gw:(g + 1) * gw]
        parts.append(_dot(pooled.astype(BF16), wp_ref[g]))
    y_pool = (jnp.concatenate(parts, axis=1) * ps_ref[...]).astype(BF16)
    pw = y_pool.shape[1]
    y = _dot(y_pool, wo_ref[:pw, :]) + _dot(yd_ref[0], wo_ref[pw:, :])
    o_ref[0] = x_ref[0] + _gate(mod_ref[0, 0], sub) * y


def _even_out_proj(x, mod, layer, u, y_diff, w_pool, pool_scale, w_out):
    b, s, d = x.shape
    pw = u.shape[-1]
    dw = y_diff.shape[-1]
    tm = PROJ_ROWS
    row = lambda bi, i: (bi, i, 0)
    halo_blocks = tm // POOL_HALO
    return pl.pallas_call(
        functools.partial(_even_out_kernel, sub=1),
        grid=(b, s // tm),
        in_specs=[pl.BlockSpec((1, tm, d), row),
                  pl.BlockSpec((1, 1, 3 * N_SUB, d), lambda bi, i: (layer, bi, 0, 0)),
                  pl.BlockSpec((1, tm, pw), row),
                  pl.BlockSpec((1, POOL_HALO, pw),
                               lambda bi, i: (bi, jnp.maximum(i * halo_blocks - 1, 0), 0)),
                  pl.BlockSpec((1, tm, dw), row),
                  _resident(w_pool.shape, lambda bi, i: (0, 0, 0)),
                  pl.BlockSpec((1, pw), lambda bi, i: (0, 0)),
                  _resident(w_out.shape, lambda bi, i: (0, 0))],
        out_specs=pl.BlockSpec((1, tm, d), row),
        out_shape=jax.ShapeDtypeStruct((b, s, d), F32),
        compiler_params=_params("arbitrary", "arbitrary"),
        name="even_out_proj",
    )(x, mod, u, u, y_diff, w_pool.astype(BF16), pool_scale.reshape(1, pw), w_out.astype(BF16))


def _odd_in_kernel(x_ref, mod_ref, g_ref, w_ref, q_ref, k_ref, v_ref, o_ref, gates_ref, *, sub):
    h = _modnorm(x_ref[0], g_ref[...], mod_ref[0, 0], sub).astype(BF16)
    p = _dot(h, w_ref[...])
    nq = q_ref.shape[-1]
    nv = v_ref.shape[-1]
    q_ref[0] = p[:, :nq].astype(BF16)
    k_ref[0] = (p[:, nq:2 * nq] * (MLSTM_QK_DIM ** -0.5)).astype(BF16)
    v_ref[0] = p[:, 2 * nq:2 * nq + nv].astype(BF16)
    o_ref[0] = p[:, 2 * nq + nv:2 * nq + 2 * nv].astype(BF16)
    gates_ref[0] = p[:, 2 * nq + 2 * nv:]


def _odd_in_proj(x, mod, layer, g, w_in, nq, nv):
    b, s, d = x.shape
    n_gates = w_in.shape[1] - 2 * nq - 2 * nv
    w = jnp.pad(w_in, ((0, 0), (0, LANES - n_gates))).astype(BF16)
    n = w.shape[1]
    tm = PROJ_ROWS
    row = lambda bi, i: (bi, i, 0)
    return pl.pallas_call(
        functools.partial(_odd_in_kernel, sub=1),
        grid=(b, s // tm),
        in_specs=[pl.BlockSpec((1, tm, d), row),
                  pl.BlockSpec((1, 1, 3 * N_SUB, d), lambda bi, i: (layer, bi, 0, 0)),
                  pl.BlockSpec((1, d), lambda bi, i: (0, 0)),
                  _resident((d, n), lambda bi, i: (0, 0))],
        out_specs=[pl.BlockSpec((1, tm, nq), row),
                   pl.BlockSpec((1, tm, nq), row),
                   pl.BlockSpec((1, tm, nv), row),
                   pl.BlockSpec((1, tm, nv), row),
                   pl.BlockSpec((1, tm, LANES), row)],
        out_shape=[jax.ShapeDtypeStruct((b, s, nq), BF16),
                   jax.ShapeDtypeStruct((b, s, nq), BF16),
                   jax.ShapeDtypeStruct((b, s, nv), BF16),
                   jax.ShapeDtypeStruct((b, s, nv), BF16),
                   jax.ShapeDtypeStruct((b, s, LANES), F32)],
        compiler_params=_params("arbitrary", "arbitrary"),
        name="odd_in_proj",
    )(x, mod, g.reshape(1, d), w)


def _log_sigmoid(x):
    return jnp.minimum(x, 0.0) - jnp.log1p(jnp.exp(-jnp.abs(x)))


def _lane_cumsum(x):
    lane = lax.broadcasted_iota(jnp.int32, x.shape, 1)
    shift = 1
    while shift < x.shape[-1]:
        x = x + jnp.where(lane >= shift, pltpu.roll(x, shift, 1), 0.0)
        shift *= 2
    return x


def _mlstm_kernel(bg_ref, q_ref, k_ref, v_ref, og_ref, ig_ref, fg_ref, ng_ref, y_ref,
                  c_ref, n_ref, m_ref, *, chunks):
    head = pl.program_id(1)
    L = MLSTM_CHUNK

    @pl.when(pl.program_id(2) == 0)
    def _():
        c_ref[...] = jnp.zeros_like(c_ref)
        n_ref[...] = jnp.zeros_like(n_ref)
        m_ref[...] = jnp.zeros_like(m_ref)

    i_all = ig_ref[0, 0] + bg_ref[head]
    logf_all = _log_sigmoid(fg_ref[0, 0] + bg_ref[MLSTM_HEADS + head])
    b_all = _lane_cumsum(logf_all)
    row_id = lax.broadcasted_iota(jnp.int32, (L, L), 0)
    col_id = lax.broadcasted_iota(jnp.int32, (L, L), 1)
    causal = col_id <= row_id

    def chunk(c, carry):
        rows = pl.ds(pl.multiple_of(c * L, L), L)
        q = q_ref[0, rows, :]
        k = k_ref[0, rows, :]
        v = v_ref[0, rows, :]
        sel = lax.broadcasted_iota(jnp.int32, b_all.shape, 0) == c
        b_row = jnp.sum(jnp.where(sel, b_all, 0.0), axis=0, keepdims=True)
        i_row = jnp.sum(jnp.where(sel, i_all, 0.0), axis=0, keepdims=True)
        b_last = b_row[:, L - 1:L]
        b_rows = jnp.broadcast_to(b_row, (L, L))
        i_rows = jnp.broadcast_to(i_row, (L, L))
        b_col = b_rows.T[:, :1]
        i_col = i_rows.T[:, :1]
        m_prev = m_ref[...]

        d = jnp.where(causal, b_col - b_rows + i_rows, -jnp.inf)
        inter_log = b_col + m_prev
        m_t = jnp.maximum(inter_log, jnp.max(d, axis=1, keepdims=True))
        dw = jnp.exp(d - m_t)
        inter_w = jnp.exp(inter_log - m_t)
        s = lax.dot_general(q, k, (((1,), (1,)), ((), ())), preferred_element_type=F32) * dw
        c_prev = c_ref[...]
        num = _dot(s.astype(BF16), v) + inter_w * _dot(q, c_prev.astype(BF16))
        den = (jnp.sum(s, axis=1, keepdims=True)
               + inter_w * jnp.sum(q.astype(F32) * n_ref[...], axis=1, keepdims=True))
        h = num / jnp.maximum(jnp.abs(den), jnp.exp(-m_t))

        hn = h * lax.rsqrt(jnp.mean(h * h, axis=-1, keepdims=True) + EPS) * ng_ref[...]
        y_ref[0, rows, :] = (jax.nn.sigmoid(og_ref[0, rows, :].astype(F32)) * hn).astype(y_ref.dtype)

        a_col = b_last - b_col + i_col
        m_loc = jnp.max(a_col, axis=0, keepdims=True)
        kw = k.astype(F32) * jnp.exp(a_col - m_loc)
        c_loc = _dot(kw.T.astype(BF16), v)
        n_loc = jnp.sum(kw, axis=0, keepdims=True)
        m_new = jnp.maximum(b_last + m_prev, m_loc)
        s_old = jnp.exp(b_last + m_prev - m_new)
        s_loc = jnp.exp(m_loc - m_new)
        c_ref[...] = s_old * c_prev + s_loc * c_loc
        n_ref[...] = s_old * n_ref[...] + s_loc * n_loc
        m_ref[...] = m_new
        return carry

    lax.fori_loop(0, chunks, chunk, 0)


def _mlstm(q, k, v, o, gates, b_gates, norm_g):
    b, s, _ = q.shape
    dk, dv, L = MLSTM_QK_DIM, MLSTM_V_DIM, MLSTM_CHUNK
    nc = s // L
    g = MLSTM_CHUNKS_PER_STEP
    rows = g * L
    ig = gates[..., :MLSTM_HEADS].transpose(0, 2, 1).reshape(b, MLSTM_HEADS, nc, L)
    fg = gates[..., MLSTM_HEADS:2 * MLSTM_HEADS].transpose(0, 2, 1).reshape(b, MLSTM_HEADS, nc, L)
    gate_spec = pl.BlockSpec((1, 1, g, L), lambda bi, h, c: (bi, h, c, 0))
    return pl.pallas_call(
        functools.partial(_mlstm_kernel, chunks=g),
        grid=(b, MLSTM_HEADS, nc // g),
        in_specs=[pl.BlockSpec(memory_space=pltpu.SMEM),
                  pl.BlockSpec((1, rows, dk), lambda bi, h, c: (bi, c, h)),
                  pl.BlockSpec((1, rows, dk), lambda bi, h, c: (bi, c, h)),
                  pl.BlockSpec((1, rows, dv), lambda bi, h, c: (bi, c, h)),
                  pl.BlockSpec((1, rows, dv), lambda bi, h, c: (bi, c, h)),
                  gate_spec, gate_spec,
                  pl.BlockSpec((1, dv), lambda bi, h, c: (0, h))],
        out_specs=pl.BlockSpec((1, rows, dv), lambda bi, h, c: (bi, c, h)),
        out_shape=jax.ShapeDtypeStruct((b, s, MLSTM_HEADS * dv), BF16),
        scratch_shapes=[pltpu.VMEM((dk, dv), F32), pltpu.VMEM((1, dk), F32),
                        pltpu.VMEM((1, 1), F32)],
        compiler_params=_params("arbitrary", "arbitrary", "arbitrary"),
        name="mlstm",
    )(b_gates.reshape(-1), q, k, v, o, ig, fg, norm_g.reshape(1, -1))


def _out_proj_kernel(x_ref, mod_ref, y_ref, w_ref, o_ref, *, sub):
    o_ref[0] = x_ref[0] + _gate(mod_ref[0, 0], sub) * _dot(y_ref[0], w_ref[...])


def _out_proj(x, mod, layer, y, w_out):
    b, s, d = x.shape
    n_in = y.shape[-1]
    tm = PROJ_ROWS
    row = lambda bi, i: (bi, i, 0)
    return pl.pallas_call(
        functools.partial(_out_proj_kernel, sub=1),
        grid=(b, s // tm),
        in_specs=[pl.BlockSpec((1, tm, d), row),
                  pl.BlockSpec((1, 1, 3 * N_SUB, d), lambda bi, i: (layer, bi, 0, 0)),
                  pl.BlockSpec((1, tm, n_in), row),
                  _resident((n_in, d), lambda bi, i: (0, 0))],
        out_specs=pl.BlockSpec((1, tm, d), row),
        out_shape=jax.ShapeDtypeStruct((b, s, d), F32),
        compiler_params=_params("arbitrary", "arbitrary"),
        name="out_proj",
    )(x, mod, y, w_out.astype(BF16))


def kernel(x, c, w_mod, b_mod, norm_g, w_ffn_gate, w_ffn_up, w_ffn_down, w_in_even, w_pool, pool_scale, diff_lambda, diff_subln_g, w_out_even, w_in_odd, b_gates_odd, mlstm_norm_g, w_out_odd, final_g):
    depth = w_mod.shape[0]
    mod = _modulation(c, w_mod, b_mod)
    pool_width = pool_scale.shape[-1]
    qk_width = DIFF_HEADS * 2 * DIFF_HEAD_DIM
    for l in range(depth):
        x = _ffn(x, mod, l, 0, norm_g[l, 0], w_ffn_gate[l, 0], w_ffn_up[l, 0], w_ffn_down[l, 0],
                 final_g, False)
        if l % 2 == 0:
            e = l // 2
            lambda_init = 0.8 - 0.6 * math.exp(-0.3 * l)
            u, q, k, v = _even_in_proj(x, mod, l, norm_g[l, 1], w_in_even[e], pool_width, qk_width)
            y_diff = _diff_attention(q, k, v, diff_lambda[e], diff_subln_g[e], lambda_init)
            x = _even_out_proj(x, mod, l, u, y_diff, w_pool[e], pool_scale[e], w_out_even[e])
        else:
            o = l // 2
            q, k, v, og, gates = _odd_in_proj(x, mod, l, norm_g[l, 1], w_in_odd[o],
                                              MLSTM_HEADS * MLSTM_QK_DIM, MLSTM_HEADS * MLSTM_V_DIM)
            y = _mlstm(q, k, v, og, gates, b_gates_odd[o], mlstm_norm_g[o])
            x = _out_proj(x, mod, l, y, w_out_odd[o])
        x = _ffn(x, mod, l, 2, norm_g[l, 2], w_ffn_gate[l, 1], w_ffn_up[l, 1], w_ffn_down[l, 1],
                 final_g, l == depth - 1)
    return x
```

```python
import functools
import math

import jax
import jax.numpy as jnp
from jax import lax
from jax.experimental import pallas as pl
from jax.experimental.pallas import tpu as pltpu

F32 = jnp.float32
BF16 = jnp.bfloat16

EPS = 1e-6
N_SUB = 3
POOL_WINDOWS = (2, 4, 8, 16)
POOL_HALO = 16
DIFF_HEADS = 4
DIFF_HEAD_DIM = 64
MLSTM_HEADS = 4
MLSTM_QK_DIM = 128
MLSTM_V_DIM = 256
MLSTM_CHUNK = 128
LANES = 128

VMEM_LIMIT_BYTES = 56 * 1024 * 1024

FFN_ROWS = 512
PROJ_ROWS = 512
ATTN_ROWS = 512
ATTN_COL_GROUP = 256
LOG2_E = math.log2(math.e)
MLSTM_CHUNKS_PER_STEP = 8


def _params(*semantics):
    return pltpu.CompilerParams(dimension_semantics=semantics,
                                vmem_limit_bytes=VMEM_LIMIT_BYTES)


def _resident(block_shape, index_map):
    return pl.BlockSpec(block_shape, index_map, pipeline_mode=pl.Buffered(1))


def _dot(a, b):
    return jnp.dot(a, b, preferred_element_type=F32)


def _modnorm(x, g, mod, sub):
    shift = mod[3 * sub:3 * sub + 1]
    scale = mod[3 * sub + 1:3 * sub + 2]
    y = x * lax.rsqrt(jnp.mean(x * x, axis=-1, keepdims=True) + EPS)
    return (y * g) * (1.0 + scale) + shift


def _gate(mod, sub):
    return mod[3 * sub + 2:3 * sub + 3]


def _mod_kernel(c_ref, w_ref, b_ref, o_ref):
    c = c_ref[...]
    c_act = c * jax.nn.sigmoid(c)
    o_ref[0] = jnp.dot(c_act, w_ref[0], precision=lax.Precision.HIGHEST,
                       preferred_element_type=F32) + b_ref[0]


def _modulation(c, w_mod, b_mod):
    depth, d, n = w_mod.shape
    b = c.shape[0]
    tn = d
    out = pl.pallas_call(
        _mod_kernel,
        grid=(depth, n // tn),
        in_specs=[pl.BlockSpec((b, d), lambda l, j: (0, 0)),
                  pl.BlockSpec((1, d, tn), lambda l, j: (l, 0, j)),
                  pl.BlockSpec((1, 1, tn), lambda l, j: (l, 0, j))],
        out_specs=pl.BlockSpec((1, b, tn), lambda l, j: (l, 0, j)),
        out_shape=jax.ShapeDtypeStruct((depth, b, n), F32),
        compiler_params=_params("arbitrary", "arbitrary"),
        name="modulation",
    )(c, w_mod, b_mod.reshape(depth, 1, n))
    return out.reshape(depth, b, 3 * N_SUB, d)


def _ffn_kernel(x_ref, mod_ref, g_ref, wg_ref, wu_ref, wd_ref, fg_ref, o_ref, *, sub, final_norm):
    x = x_ref[0]
    mod = mod_ref[0, 0]
    h = _modnorm(x, g_ref[...], mod, sub).astype(BF16)
    gate_act = _dot(h, wg_ref[...])
    up = _dot(h, wu_ref[...])
    a = (gate_act * jax.nn.sigmoid(gate_act) * up).astype(BF16)
    y = _dot(a, wd_ref[...])
    out = x + (0.5 * _gate(mod, sub)) * y
    if final_norm:
        out = out * lax.rsqrt(jnp.mean(out * out, axis=-1, keepdims=True) + EPS) * fg_ref[...]
    o_ref[0] = out


def _ffn(x, mod, layer, sub, g, w_gate, w_up, w_down, final_g, final_norm):
    b, s, d = x.shape
    f = w_gate.shape[1]
    tm = FFN_ROWS
    return pl.pallas_call(
        functools.partial(_ffn_kernel, sub=sub, final_norm=final_norm),
        grid=(b, s // tm),
        in_specs=[pl.BlockSpec((1, tm, d), lambda bi, i: (bi, i, 0)),
                  pl.BlockSpec((1, 1, 3 * N_SUB, d), lambda bi, i: (layer, bi, 0, 0)),
                  pl.BlockSpec((1, d), lambda bi, i: (0, 0)),
                  _resident((d, f), lambda bi, i: (0, 0)),
                  _resident((d, f), lambda bi, i: (0, 0)),
                  _resident((f, d), lambda bi, i: (0, 0)),
                  pl.BlockSpec((1, d), lambda bi, i: (0, 0))],
        out_specs=pl.BlockSpec((1, tm, d), lambda bi, i: (bi, i, 0)),
        out_shape=jax.ShapeDtypeStruct((b, s, d), F32),
        compiler_params=_params("arbitrary", "arbitrary"),
        name="ffn",
    )(x, mod, g.reshape(1, d), w_gate.astype(BF16), w_up.astype(BF16), w_down.astype(BF16),
      final_g.reshape(1, d))


def _even_in_kernel(x_ref, mod_ref, g_ref, w_ref, wt_ref, u_ref, k_ref, qt_ref, vt_ref, *, sub):
    h = _modnorm(x_ref[0], g_ref[...], mod_ref[0, 0], sub).astype(BF16)
    p = _dot(h, w_ref[...])
    pw = u_ref.shape[-1]
    u_ref[0] = p[:, :pw]
    k_ref[0] = p[:, pw:].astype(BF16)
    pt = lax.dot_general(wt_ref[...], h, (((1,), (1,)), ((), ())), preferred_element_type=F32)
    qw = qt_ref.shape[1]
    qt_ref[0] = (pt[:qw] * (DIFF_HEAD_DIM ** -0.5 * LOG2_E)).astype(BF16)
    vt_ref[0, 0] = pt[qw:].astype(BF16)


def _even_in_proj(x, mod, layer, g, w_in, pool_width, qk_width):
    b, s, d = x.shape
    n = w_in.shape[1]
    v_width = n - pool_width - 2 * qk_width
    tm = ATTN_ROWS
    k0 = pool_width + qk_width
    w_uk = jnp.concatenate([w_in[:, :pool_width], w_in[:, k0:k0 + qk_width]], axis=1).astype(BF16)
    w_qv_t = jnp.concatenate([w_in[:, pool_width:k0], w_in[:, k0 + qk_width:]], axis=1).T.astype(BF16)
    row = lambda bi, i: (bi, i, 0)
    return pl.pallas_call(
        functools.partial(_even_in_kernel, sub=1),
        grid=(b, s // tm),
        in_specs=[pl.BlockSpec((1, tm, d), row),
                  pl.BlockSpec((1, 1, 3 * N_SUB, d), lambda bi, i: (layer, bi, 0, 0)),
                  pl.BlockSpec((1, d), lambda bi, i: (0, 0)),
                  _resident(w_uk.shape, lambda bi, i: (0, 0)),
                  _resident(w_qv_t.shape, lambda bi, i: (0, 0))],
        out_specs=[pl.BlockSpec((1, tm, pool_width), row),
                   pl.BlockSpec((1, tm, qk_width), row),
                   pl.BlockSpec((1, qk_width, tm), lambda bi, i: (bi, 0, i)),
                   pl.BlockSpec((1, 1, v_width, tm), lambda bi, i: (bi, i, 0, 0))],
        out_shape=[jax.ShapeDtypeStruct((b, s, pool_width), F32),
                   jax.ShapeDtypeStruct((b, s, qk_width), BF16),
                   jax.ShapeDtypeStruct((b, qk_width, s), BF16),
                   jax.ShapeDtypeStruct((b, s // tm, v_width, tm), BF16)],
        compiler_params=_params("arbitrary", "arbitrary"),
        name="even_in_proj",
    )(x, mod, g.reshape(1, d), w_uk, w_qv_t)


def _diff_attn_kernel(qt_ref, k_ref, vt_ref, lam_ref, sg_ref, o_ref, m_ref, l_ref, acc_ref,
                      s0_ref, s1_ref, cm0_ref, cm1_ref, *, lambda_init, t):
    i = pl.program_id(2)
    qt = qt_ref[0].astype(F32)
    feat = lax.broadcasted_iota(jnp.int32, qt.shape, 0)
    qs = jnp.concatenate([jnp.where(feat < DIFF_HEAD_DIM, qt, 0.0),
                          jnp.where(feat >= DIFF_HEAD_DIM, qt, 0.0)], axis=1).astype(BF16)
    m_ref[...] = jnp.full_like(m_ref, -jnp.inf)
    l_ref[...] = jnp.zeros_like(l_ref)
    acc_ref[...] = jnp.zeros_like(acc_ref)
    groups = [slice(c, c + ATTN_COL_GROUP) for c in range(0, 2 * t, ATTN_COL_GROUP)]
    s_refs = (s0_ref, s1_ref)
    cm_refs = (cm0_ref, cm1_ref)

    def produce(j, slot, cols):
        kb = k_ref[0, pl.ds(pl.multiple_of(j * t, t), t), :]
        s = _dot(kb, qs[:, cols])
        s_refs[slot][:, cols] = s
        cm_refs[slot][:, cols] = jnp.max(s, axis=0, keepdims=True)

    def consume(j, slot, cols, diagonal):
        s = s_refs[slot][:, cols]
        if diagonal:
            key = lax.broadcasted_iota(jnp.int32, s.shape, 0)
            qry = lax.broadcasted_iota(jnp.int32, s.shape, 1) + cols.start % t
            s = jnp.where(key <= qry, s, -jnp.inf)
            cm = jnp.max(s, axis=0, keepdims=True)
        else:
            cm = cm_refs[slot][:, cols]
        m = m_ref[:, cols]
        m_new = jnp.maximum(m, cm)
        alpha = jnp.exp2(m - m_new)
        p = jnp.exp2(s - m_new)
        l_ref[:, cols] = alpha * l_ref[:, cols] + jnp.sum(p, axis=0, keepdims=True)
        acc_ref[:, cols] = alpha * acc_ref[:, cols] + _dot(vt_ref[0, j], p.astype(BF16))
        m_ref[:, cols] = m_new

    def advance(j, slot):
        for cols in groups:
            produce(j + 1, 1 - slot, cols)
            consume(j, slot, cols, False)

    def diagonal(slot):
        for cols in groups:
            consume(i, slot, cols, True)

    for cols in groups:
        produce(0, 0, cols)

    def body(jj, carry):
        advance(2 * jj, 0)
        advance(2 * jj + 1, 1)
        return carry

    lax.fori_loop(0, i // 2, body, 0)

    @pl.when(i % 2 == 0)
    def _():
        diagonal(0)

    @pl.when(i % 2 == 1)
    def _():
        advance(i - 1, 0)
        diagonal(1)

    o = acc_ref[...] / l_ref[...]
    lp = lam_ref[...]
    lam = (jnp.exp(jnp.sum(lp[0:1] * lp[1:2], axis=1, keepdims=True))
           - jnp.exp(jnp.sum(lp[2:3] * lp[3:4], axis=1, keepdims=True)) + lambda_init)
    o = o[:, :t] - lam * o[:, t:]
    o = o * lax.rsqrt(jnp.mean(o * o, axis=0, keepdims=True) + EPS) * sg_ref[...]
    o_ref[0] = (o * (1.0 - lambda_init)).T.astype(o_ref.dtype)


def _diff_attention(qt, k, vt, lam_params, subln_g, lambda_init):
    b, s, _ = k.shape
    hd = 2 * DIFF_HEAD_DIM
    t = ATTN_ROWS
    assert s % t == 0 and vt.shape[-1] == t and (2 * t) % ATTN_COL_GROUP == 0
    return pl.pallas_call(
        functools.partial(_diff_attn_kernel, lambda_init=lambda_init, t=t),
        grid=(b, DIFF_HEADS, s // t),
        in_specs=[pl.BlockSpec((1, hd, t), lambda bi, h, i: (bi, h, i)),
                  pl.BlockSpec((1, s, hd), lambda bi, h, i: (bi, 0, h)),
                  pl.BlockSpec((1, s // t, hd, t), lambda bi, h, i: (bi, 0, h, 0)),
                  pl.BlockSpec(lam_params.shape, lambda bi, h, i: (0, 0)),
                  pl.BlockSpec((hd, 1), lambda bi, h, i: (0, 0))],
        out_specs=pl.BlockSpec((1, t, hd), lambda bi, h, i: (bi, i, h)),
        out_shape=jax.ShapeDtypeStruct((b, s, DIFF_HEADS * hd), BF16),
        scratch_shapes=[pltpu.VMEM((1, 2 * t), F32), pltpu.VMEM((1, 2 * t), F32),
                        pltpu.VMEM((hd, 2 * t), F32),
                        pltpu.VMEM((t, 2 * t), F32), pltpu.VMEM((t, 2 * t), F32),
                        pltpu.VMEM((1, 2 * t), F32), pltpu.VMEM((1, 2 * t), F32)],
        compiler_params=_params("arbitrary", "arbitrary", "arbitrary"),
        name="diff_attention",
    )(qt, k, vt, lam_params, subln_g.reshape(hd, 1))


def _even_out_kernel(x_ref, mod_ref, u_ref, halo_ref, yd_ref, wp_ref, ps_ref, wo_ref, o_ref, *, sub):
    i = pl.program_id(1)
    tm = u_ref.shape[1]
    u = u_ref[0]
    halo = halo_ref[0] * (i > 0).astype(F32)
    ext = jnp.concatenate([halo, u], axis=0)
    pos = i * tm + lax.broadcasted_iota(jnp.int32, (tm, 1), 0)
    gw = wp_ref.shape[-1]
    sums = ext
    shift = 1
    parts = []
    for g, w in enumerate(POOL_WINDOWS):
        while shift < w:
            sums = sums + pltpu.roll(sums, shift, 0)
            shift *= 2
        total = sums[POOL_HALO:, g * gw:(g + 1) * gw]
        count = jnp.minimum(pos + 1, w).astype(F32)
        pooled = total / count - u[:, g * gw:(g + 1) * gw]
        parts.append(_dot(pooled.astype(BF16), wp_ref[g]))
    y_pool = (jnp.concatenate(parts, axis=1) * ps_ref[...]).astype(BF16)
    pw = y_pool.shape[1]
    y = _dot(y_pool, wo_ref[:pw, :]) + _dot(yd_ref[0], wo_ref[pw:, :])
    o_ref[0] = x_ref[0] + _gate(mod_ref[0, 0], sub) * y


def _even_out_proj(x, mod, layer, u, y_diff, w_pool, pool_scale, w_out):
    b, s, d = x.shape
    pw = u.shape[-1]
    dw = y_diff.shape[-1]
    tm = PROJ_ROWS
    row = lambda bi, i: (bi, i, 0)
    halo_blocks = tm // POOL_HALO
    return pl.pallas_call(
        functools.partial(_even_out_kernel, sub=1),
        grid=(b, s // tm),
        in_specs=[pl.BlockSpec((1, tm, d), row),
                  pl.BlockSpec((1, 1, 3 * N_SUB, d), lambda bi, i: (layer, bi, 0, 0)),
                  pl.BlockSpec((1, tm, pw), row),
                  pl.BlockSpec((1, POOL_HALO, pw),
                               lambda bi, i: (bi, jnp.maximum(i * halo_blocks - 1, 0), 0)),
                  pl.BlockSpec((1, tm, dw), row),
                  _resident(w_pool.shape, lambda bi, i: (0, 0, 0)),
                  pl.BlockSpec((1, pw), lambda bi, i: (0, 0)),
                  _resident(w_out.shape, lambda bi, i: (0, 0))],
        out_specs=pl.BlockSpec((1, tm, d), row),
        out_shape=jax.ShapeDtypeStruct((b, s, d), F32),
        compiler_params=_params("arbitrary", "arbitrary"),
        name="even_out_proj",
    )(x, mod, u, u, y_diff, w_pool.astype(BF16), pool_scale.reshape(1, pw), w_out.astype(BF16))


def _odd_in_kernel(x_ref, mod_ref, g_ref, w_ref, q_ref, k_ref, v_ref, o_ref, gates_ref, *, sub):
    h = _modnorm(x_ref[0], g_ref[...], mod_ref[0, 0], sub).astype(BF16)
    p = _dot(h, w_ref[...])
    nq = q_ref.shape[-1]
    nv = v_ref.shape[-1]
    q_ref[0] = p[:, :nq].astype(BF16)
    k_ref[0] = (p[:, nq:2 * nq] * (MLSTM_QK_DIM ** -0.5)).astype(BF16)
    v_ref[0] = p[:, 2 * nq:2 * nq + nv].astype(BF16)
    o_ref[0] = p[:, 2 * nq + nv:2 * nq + 2 * nv].astype(BF16)
    gates_ref[0] = p[:, 2 * nq + 2 * nv:]


def _odd_in_proj(x, mod, layer, g, w_in, nq, nv):
    b, s, d = x.shape
    n_gates = w_in.shape[1] - 2 * nq - 2 * nv
    w = jnp.pad(w_in, ((0, 0), (0, LANES - n_gates))).astype(BF16)
    n = w.shape[1]
    tm = PROJ_ROWS
    row = lambda bi, i: (bi, i, 0)
    return pl.pallas_call(
        functools.partial(_odd_in_kernel, sub=1),
        grid=(b, s // tm),
        in_specs=[pl.BlockSpec((1, tm, d), row),
                  pl.BlockSpec((1, 1, 3 * N_SUB, d), lambda bi, i: (layer, bi, 0, 0)),
                  pl.BlockSpec((1, d), lambda bi, i: (0, 0)),
                  _resident((d, n), lambda bi, i: (0, 0))],
        out_specs=[pl.BlockSpec((1, tm, nq), row),
                   pl.BlockSpec((1, tm, nq), row),
                   pl.BlockSpec((1, tm, nv), row),
                   pl.BlockSpec((1, tm, nv), row),
                   pl.BlockSpec((1, tm, LANES), row)],
        out_shape=[jax.ShapeDtypeStruct((b, s, nq), BF16),
                   jax.ShapeDtypeStruct((b, s, nq), BF16),
                   jax.ShapeDtypeStruct((b, s, nv), BF16),
                   jax.ShapeDtypeStruct((b, s, nv), BF16),
                   jax.ShapeDtypeStruct((b, s, LANES), F32)],
        compiler_params=_params("arbitrary", "arbitrary"),
        name="odd_in_proj",
    )(x, mod, g.reshape(1, d), w)


def _log_sigmoid(x):
    return jnp.minimum(x, 0.0) - jnp.log1p(jnp.exp(-jnp.abs(x)))


def _lane_cumsum(x):
    lane = lax.broadcasted_iota(jnp.int32, x.shape, 1)
    shift = 1
    while shift < x.shape[-1]:
        x = x + jnp.where(lane >= shift, pltpu.roll(x, shift, 1), 0.0)
        shift *= 2
    return x


def _mlstm_kernel(bg_ref, q_ref, k_ref, v_ref, og_ref, ig_ref, fg_ref, ng_ref, y_ref,
                  c_ref, n_ref, m_ref, *, chunks):
    head = pl.program_id(1)
    L = MLSTM_CHUNK

    @pl.when(pl.program_id(2) == 0)
    def _():
        c_ref[...] = jnp.zeros_like(c_ref)
        n_ref[...] = jnp.zeros_like(n_ref)
        m_ref[...] = jnp.zeros_like(m_ref)

    i_all = ig_ref[0, 0] + bg_ref[head]
    logf_all = _log_sigmoid(fg_ref[0, 0] + bg_ref[MLSTM_HEADS + head])
    b_all = _lane_cumsum(logf_all)
    row_id = lax.broadcasted_iota(jnp.int32, (L, L), 0)
    col_id = lax.broadcasted_iota(jnp.int32, (L, L), 1)
    causal = col_id <= row_id

    def chunk(c, carry):
        rows = pl.ds(pl.multiple_of(c * L, L), L)
        q = q_ref[0, rows, :]
        k = k_ref[0, rows, :]
        v = v_ref[0, rows, :]
        sel = lax.broadcasted_iota(jnp.int32, b_all.shape, 0) == c
        b_row = jnp.sum(jnp.where(sel, b_all, 0.0), axis=0, keepdims=True)
        i_row = jnp.sum(jnp.where(sel, i_all, 0.0), axis=0, keepdims=True)
        b_last = b_row[:, L - 1:L]
        b_rows = jnp.broadcast_to(b_row, (L, L))
        i_rows = jnp.broadcast_to(i_row, (L, L))
        b_col = b_rows.T[:, :1]
        i_col = i_rows.T[:, :1]
        m_prev = m_ref[...]

        d = jnp.where(causal, b_col - b_rows + i_rows, -jnp.inf)
        inter_log = b_col + m_prev
        m_t = jnp.maximum(inter_log, jnp.max(d, axis=1, keepdims=True))
        dw = jnp.exp(d - m_t)
        inter_w = jnp.exp(inter_log - m_t)
        s = lax.dot_general(q, k, (((1,), (1,)), ((), ())), preferred_element_type=F32) * dw
        c_prev = c_ref[...]
        num = _dot(s.astype(BF16), v) + inter_w * _dot(q, c_prev.astype(BF16))
        den = (jnp.sum(s, axis=1, keepdims=True)
               + inter_w * jnp.sum(q.astype(F32) * n_ref[...], axis=1, keepdims=True))
        h = num / jnp.maximum(jnp.abs(den), jnp.exp(-m_t))

        hn = h * lax.rsqrt(jnp.mean(h * h, axis=-1, keepdims=True) + EPS) * ng_ref[...]
        y_ref[0, rows, :] = (jax.nn.sigmoid(og_ref[0, rows, :].astype(F32)) * hn).astype(y_ref.dtype)

        a_col = b_last - b_col + i_col
        m_loc = jnp.max(a_col, axis=0, keepdims=True)
        kw = k.astype(F32) * jnp.exp(a_col - m_loc)
        c_loc = _dot(kw.T.astype(BF16), v)
        n_loc = jnp.sum(kw, axis=0, keepdims=True)
        m_new = jnp.maximum(b_last + m_prev, m_loc)
        s_old = jnp.exp(b_last + m_prev - m_new)
        s_loc = jnp.exp(m_loc - m_new)
        c_ref[...] = s_old * c_prev + s_loc * c_loc
        n_ref[...] = s_old * n_ref[...] + s_loc * n_loc
        m_ref[...] = m_new
        return carry

    lax.fori_loop(0, chunks, chunk, 0)


def _mlstm(q, k, v, o, gates, b_gates, norm_g):
    b, s, _ = q.shape
    dk, dv, L = MLSTM_QK_DIM, MLSTM_V_DIM, MLSTM_CHUNK
    nc = s // L
    g = MLSTM_CHUNKS_PER_STEP
    rows = g * L
    ig = gates[..., :MLSTM_HEADS].transpose(0, 2, 1).reshape(b, MLSTM_HEADS, nc, L)
    fg = gates[..., MLSTM_HEADS:2 * MLSTM_HEADS].transpose(0, 2, 1).reshape(b, MLSTM_HEADS, nc, L)
    gate_spec = pl.BlockSpec((1, 1, g, L), lambda bi, h, c: (bi, h, c, 0))
    return pl.pallas_call(
        functools.partial(_mlstm_kernel, chunks=g),
        grid=(b, MLSTM_HEADS, nc // g),
        in_specs=[pl.BlockSpec(memory_space=pltpu.SMEM),
                  pl.BlockSpec((1, rows, dk), lambda bi, h, c: (bi, c, h)),
                  pl.BlockSpec((1, rows, dk), lambda bi, h, c: (bi, c, h)),
                  pl.BlockSpec((1, rows, dv), lambda bi, h, c: (bi, c, h)),
                  pl.BlockSpec((1, rows, dv), lambda bi, h, c: (bi, c, h)),
                  gate_spec, gate_spec,
                  pl.BlockSpec((1, dv), lambda bi, h, c: (0, h))],
        out_specs=pl.BlockSpec((1, rows, dv), lambda bi, h, c: (bi, c, h)),
        out_shape=jax.ShapeDtypeStruct((b, s, MLSTM_HEADS * dv), BF16),
        scratch_shapes=[pltpu.VMEM((dk, dv), F32), pltpu.VMEM((1, dk), F32),
                        pltpu.VMEM((1, 1), F32)],
        compiler_params=_params("arbitrary", "arbitrary", "arbitrary"),
        name="mlstm",
    )(b_gates.reshape(-1), q, k, v, o, ig, fg, norm_g.reshape(1, -1))


def _out_proj_kernel(x_ref, mod_ref, y_ref, w_ref, o_ref, *, sub):
    o_ref[0] = x_ref[0] + _gate(mod_ref[0, 0], sub) * _dot(y_ref[0], w_ref[...])


def _out_proj(x, mod, layer, y, w_out):
    b, s, d = x.shape
    n_in = y.shape[-1]
    tm = PROJ_ROWS
    row = lambda bi, i: (bi, i, 0)
    return pl.pallas_call(
        functools.partial(_out_proj_kernel, sub=1),
        grid=(b, s // tm),
        in_specs=[pl.BlockSpec((1, tm, d), row),
                  pl.BlockSpec((1, 1, 3 * N_SUB, d), lambda bi, i: (layer, bi, 0, 0)),
                  pl.BlockSpec((1, tm, n_in), row),
                  _resident((n_in, d), lambda bi, i: (0, 0))],
        out_specs=pl.BlockSpec((1, tm, d), row),
        out_shape=jax.ShapeDtypeStruct((b, s, d), F32),
        compiler_params=_params("arbitrary", "arbitrary"),
        name="out_proj",
    )(x, mod, y, w_out.astype(BF16))


def kernel(x, c, w_mod, b_mod, norm_g, w_ffn_gate, w_ffn_up, w_ffn_down, w_in_even, w_pool, pool_scale, diff_lambda, diff_subln_g, w_out_even, w_in_odd, b_gates_odd, mlstm_norm_g, w_out_odd, final_g):
    depth = w_mod.shape[0]
    mod = _modulation(c, w_mod, b_mod)
    pool_width = pool_scale.shape[-1]
    qk_width = DIFF_HEADS * 2 * DIFF_HEAD_DIM
    for l in range(depth):
        x = _ffn(x, mod, l, 0, norm_g[l, 0], w_ffn_gate[l, 0], w_ffn_up[l, 0], w_ffn_down[l, 0],
                 final_g, False)
        if l % 2 == 0:
            e = l // 2
            lambda_init = 0.8 - 0.6 * math.exp(-0.3 * l)
            u, k, qt, vt = _even_in_proj(x, mod, l, norm_g[l, 1], w_in_even[e], pool_width, qk_width)
            y_diff = _diff_attention(qt, k, vt, diff_lambda[e], diff_subln_g[e], lambda_init)
            x = _even_out_proj(x, mod, l, u, y_diff, w_pool[e], pool_scale[e], w_out_even[e])
        else:
            o = l // 2
            q, k, v, og, gates = _odd_in_proj(x, mod, l, norm_g[l, 1], w_in_odd[o],
                                              MLSTM_HEADS * MLSTM_QK_DIM, MLSTM_HEADS * MLSTM_V_DIM)
            y = _mlstm(q, k, v, og, gates, b_gates_odd[o], mlstm_norm_g[o])
            x = _out_proj(x, mod, l, y, w_out_odd[o])
        x = _ffn(x, mod, l, 2, norm_g[l, 2], w_ffn_gate[l, 1], w_ffn_up[l, 1], w_ffn_down[l, 1],
                 final_g, l == depth - 1)
    return x
```

```python
import functools
import math

import jax
import jax.numpy as jnp
from jax import lax
from jax.experimental import pallas as pl
from jax.experimental.pallas import tpu as pltpu

F32 = jnp.float32
BF16 = jnp.bfloat16

EPS = 1e-6
N_SUB = 3
POOL_WINDOWS = (2, 4, 8, 16)
POOL_HALO = 16
DIFF_HEADS = 4
DIFF_HEAD_DIM = 64
MLSTM_HEADS = 4
MLSTM_QK_DIM = 128
MLSTM_V_DIM = 256
MLSTM_CHUNK = 128
LANES = 128

VMEM_LIMIT_BYTES = 56 * 1024 * 1024

FFN_ROWS = 512
PROJ_ROWS = 512
ATTN_ROWS = 512
ATTN_COL_GROUP = 256
LOG2_E = math.log2(math.e)
MLSTM_CHUNKS_PER_STEP = 8


def _params(*semantics):
    return pltpu.CompilerParams(dimension_semantics=semantics,
                                vmem_limit_bytes=VMEM_LIMIT_BYTES)


def _resident(block_shape, index_map):
    return pl.BlockSpec(block_shape, index_map, pipeline_mode=pl.Buffered(1))


def _dot(a, b):
    return jnp.dot(a, b, preferred_element_type=F32)


def _modnorm(x, g, mod, sub):
    shift = mod[3 * sub:3 * sub + 1]
    scale = mod[3 * sub + 1:3 * sub + 2]
    y = x * lax.rsqrt(jnp.mean(x * x, axis=-1, keepdims=True) + EPS)
    return (y * g) * (1.0 + scale) + shift


def _gate(mod, sub):
    return mod[3 * sub + 2:3 * sub + 3]


def _mod_kernel(c_ref, w_ref, b_ref, o_ref):
    c = c_ref[...]
    c_act = c * jax.nn.sigmoid(c)
    o_ref[0] = jnp.dot(c_act, w_ref[0], precision=lax.Precision.HIGHEST,
                       preferred_element_type=F32) + b_ref[0]


def _modulation(c, w_mod, b_mod):
    depth, d, n = w_mod.shape
    b = c.shape[0]
    tn = d
    out = pl.pallas_call(
        _mod_kernel,
        grid=(depth, n // tn),
        in_specs=[pl.BlockSpec((b, d), lambda l, j: (0, 0)),
                  pl.BlockSpec((1, d, tn), lambda l, j: (l, 0, j)),
                  pl.BlockSpec((1, 1, tn), lambda l, j: (l, 0, j))],
        out_specs=pl.BlockSpec((1, b, tn), lambda l, j: (l, 0, j)),
        out_shape=jax.ShapeDtypeStruct((depth, b, n), F32),
        compiler_params=_params("arbitrary", "arbitrary"),
        name="modulation",
    )(c, w_mod, b_mod.reshape(depth, 1, n))
    return out.reshape(depth, b, 3 * N_SUB, d)


def _ffn_kernel(x_ref, mod_ref, g_ref, wg_ref, wu_ref, wd_ref, fg_ref, o_ref, *, sub, final_norm):
    x = x_ref[0]
    mod = mod_ref[0, 0]
    h = _modnorm(x, g_ref[...], mod, sub).astype(BF16)
    gate_act = _dot(h, wg_ref[0, 0])
    up = _dot(h, wu_ref[0, 0])
    a = (gate_act * jax.nn.sigmoid(gate_act) * up).astype(BF16)
    y = _dot(a, wd_ref[0, 0])
    out = x + (0.5 * _gate(mod, sub)) * y
    if final_norm:
        out = out * lax.rsqrt(jnp.mean(out * out, axis=-1, keepdims=True) + EPS) * fg_ref[...]
    o_ref[0] = out


def _ffn(x, mod, layer, sub, which, g, w_gate, w_up, w_down, final_g, final_norm):
    b, s, d = x.shape
    f = w_gate.shape[-1]
    tm = FFN_ROWS
    pick = lambda bi, i: (layer, which, 0, 0)
    return pl.pallas_call(
        functools.partial(_ffn_kernel, sub=sub, final_norm=final_norm),
        grid=(b, s // tm),
        in_specs=[pl.BlockSpec((1, tm, d), lambda bi, i: (bi, i, 0)),
                  pl.BlockSpec((1, 1, 3 * N_SUB, d), lambda bi, i: (layer, bi, 0, 0)),
                  pl.BlockSpec((1, d), lambda bi, i: (0, 0)),
                  _resident((1, 1, d, f), pick),
                  _resident((1, 1, d, f), pick),
                  _resident((1, 1, f, d), pick),
                  pl.BlockSpec((1, d), lambda bi, i: (0, 0))],
        out_specs=pl.BlockSpec((1, tm, d), lambda bi, i: (bi, i, 0)),
        out_shape=jax.ShapeDtypeStruct((b, s, d), F32),
        compiler_params=_params("arbitrary", "arbitrary"),
        name="ffn",
    )(x, mod, g.reshape(1, d), w_gate, w_up, w_down, final_g.reshape(1, d))


def _even_in_kernel(x_ref, mod_ref, g_ref, w_ref, wt_ref, u_ref, k_ref, qt_ref, vt_ref, *, sub):
    h = _modnorm(x_ref[0], g_ref[...], mod_ref[0, 0], sub).astype(BF16)
    p = _dot(h, w_ref[...])
    pw = u_ref.shape[-1]
    u_ref[0] = p[:, :pw]
    k_ref[0] = p[:, pw:].astype(BF16)
    pt = lax.dot_general(wt_ref[...], h, (((1,), (1,)), ((), ())), preferred_element_type=F32)
    qw = qt_ref.shape[1]
    qt_ref[0] = (pt[:qw] * (DIFF_HEAD_DIM ** -0.5 * LOG2_E)).astype(BF16)
    vt_ref[0, 0] = pt[qw:].astype(BF16)


def _even_in_proj(x, mod, layer, g, w_in, pool_width, qk_width):
    b, s, d = x.shape
    n = w_in.shape[1]
    v_width = n - pool_width - 2 * qk_width
    tm = ATTN_ROWS
    k0 = pool_width + qk_width
    w_uk = jnp.concatenate([w_in[:, :pool_width], w_in[:, k0:k0 + qk_width]], axis=1).astype(BF16)
    w_qv_t = jnp.concatenate([w_in[:, pool_width:k0], w_in[:, k0 + qk_width:]], axis=1).T.astype(BF16)
    row = lambda bi, i: (bi, i, 0)
    return pl.pallas_call(
        functools.partial(_even_in_kernel, sub=1),
        grid=(b, s // tm),
        in_specs=[pl.BlockSpec((1, tm, d), row),
                  pl.BlockSpec((1, 1, 3 * N_SUB, d), lambda bi, i: (layer, bi, 0, 0)),
                  pl.BlockSpec((1, d), lambda bi, i: (0, 0)),
                  _resident(w_uk.shape, lambda bi, i: (0, 0)),
                  _resident(w_qv_t.shape, lambda bi, i: (0, 0))],
        out_specs=[pl.BlockSpec((1, tm, pool_width), row),
                   pl.BlockSpec((1, tm, qk_width), row),
                   pl.BlockSpec((1, qk_width, tm), lambda bi, i: (bi, 0, i)),
                   pl.BlockSpec((1, 1, v_width, tm), lambda bi, i: (bi, i, 0, 0))],
        out_shape=[jax.ShapeDtypeStruct((b, s, pool_width), F32),
                   jax.ShapeDtypeStruct((b, s, qk_width), BF16),
                   jax.ShapeDtypeStruct((b, qk_width, s), BF16),
                   jax.ShapeDtypeStruct((b, s // tm, v_width, tm), BF16)],
        compiler_params=_params("arbitrary", "arbitrary"),
        name="even_in_proj",
    )(x, mod, g.reshape(1, d), w_uk, w_qv_t)


def _diff_attn_kernel(qt_ref, k_ref, vt_ref, lam_ref, sg_ref, o_ref, m_ref, l_ref, acc_ref,
                      s0_ref, s1_ref, cm0_ref, cm1_ref, *, lambda_init, t):
    i = pl.program_id(2)
    qt = qt_ref[0].astype(F32)
    feat = lax.broadcasted_iota(jnp.int32, qt.shape, 0)
    qs = jnp.concatenate([jnp.where(feat < DIFF_HEAD_DIM, qt, 0.0),
                          jnp.where(feat >= DIFF_HEAD_DIM, qt, 0.0)], axis=1).astype(BF16)
    m_ref[...] = jnp.full_like(m_ref, -jnp.inf)
    l_ref[...] = jnp.zeros_like(l_ref)
    acc_ref[...] = jnp.zeros_like(acc_ref)
    groups = [slice(c, c + ATTN_COL_GROUP) for c in range(0, 2 * t, ATTN_COL_GROUP)]
    s_refs = (s0_ref, s1_ref)
    cm_refs = (cm0_ref, cm1_ref)

    def produce(j, slot, cols):
        kb = k_ref[0, pl.ds(pl.multiple_of(j * t, t), t), :]
        s = _dot(kb, qs[:, cols])
        s_refs[slot][:, cols] = s
        cm_refs[slot][:, cols] = jnp.max(s, axis=0, keepdims=True)

    def consume(j, slot, cols, diagonal):
        s = s_refs[slot][:, cols]
        if diagonal:
            key = lax.broadcasted_iota(jnp.int32, s.shape, 0)
            qry = lax.broadcasted_iota(jnp.int32, s.shape, 1) + cols.start % t
            s = jnp.where(key <= qry, s, -jnp.inf)
            cm = jnp.max(s, axis=0, keepdims=True)
        else:
            cm = cm_refs[slot][:, cols]
        m = m_ref[:, cols]
        m_new = jnp.maximum(m, cm)
        alpha = jnp.exp2(m - m_new)
        p = jnp.exp2(s - m_new)
        l_ref[:, cols] = alpha * l_ref[:, cols] + jnp.sum(p, axis=0, keepdims=True)
        acc_ref[:, cols] = alpha * acc_ref[:, cols] + _dot(vt_ref[0, j], p.astype(BF16))
        m_ref[:, cols] = m_new

    def advance(j, slot):
        for cols in groups:
            produce(j + 1, 1 - slot, cols)
            consume(j, slot, cols, False)

    def diagonal(slot):
        for cols in groups:
            consume(i, slot, cols, True)

    for cols in groups:
        produce(0, 0, cols)

    def body(jj, carry):
        advance(2 * jj, 0)
        advance(2 * jj + 1, 1)
        return carry

    lax.fori_loop(0, i // 2, body, 0)

    @pl.when(i % 2 == 0)
    def _():
        diagonal(0)

    @pl.when(i % 2 == 1)
    def _():
        advance(i - 1, 0)
        diagonal(1)

    o = acc_ref[...] / l_ref[...]
    lp = lam_ref[...]
    lam = (jnp.exp(jnp.sum(lp[0:1] * lp[1:2], axis=1, keepdims=True))
           - jnp.exp(jnp.sum(lp[2:3] * lp[3:4], axis=1, keepdims=True)) + lambda_init)
    o = o[:, :t] - lam * o[:, t:]
    o = o * lax.rsqrt(jnp.mean(o * o, axis=0, keepdims=True) + EPS) * sg_ref[...]
    o_ref[0] = (o * (1.0 - lambda_init)).T.astype(o_ref.dtype)


def _diff_attention(qt, k, vt, lam_params, subln_g, lambda_init):
    b, s, _ = k.shape
    hd = 2 * DIFF_HEAD_DIM
    t = ATTN_ROWS
    assert s % t == 0 and vt.shape[-1] == t and (2 * t) % ATTN_COL_GROUP == 0
    return pl.pallas_call(
        functools.partial(_diff_attn_kernel, lambda_init=lambda_init, t=t),
        grid=(b, DIFF_HEADS, s // t),
        in_specs=[pl.BlockSpec((1, hd, t), lambda bi, h, i: (bi, h, i)),
                  pl.BlockSpec((1, s, hd), lambda bi, h, i: (bi, 0, h)),
                  pl.BlockSpec((1, s // t, hd, t), lambda bi, h, i: (bi, 0, h, 0)),
                  pl.BlockSpec(lam_params.shape, lambda bi, h, i: (0, 0)),
                  pl.BlockSpec((hd, 1), lambda bi, h, i: (0, 0))],
        out_specs=pl.BlockSpec((1, t, hd), lambda bi, h, i: (bi, i, h)),
        out_shape=jax.ShapeDtypeStruct((b, s, DIFF_HEADS * hd), BF16),
        scratch_shapes=[pltpu.VMEM((1, 2 * t), F32), pltpu.VMEM((1, 2 * t), F32),
                        pltpu.VMEM((hd, 2 * t), F32),
                        pltpu.VMEM((t, 2 * t), F32), pltpu.VMEM((t, 2 * t), F32),
                        pltpu.VMEM((1, 2 * t), F32), pltpu.VMEM((1, 2 * t), F32)],
        compiler_params=_params("arbitrary", "arbitrary", "arbitrary"),
        name="diff_attention",
    )(qt, k, vt, lam_params, subln_g.reshape(hd, 1))


def _even_out_kernel(x_ref, mod_ref, u_ref, halo_ref, yd_ref, wp_ref, ps_ref, wo_ref, o_ref, *, sub):
    i = pl.program_id(1)
    tm = u_ref.shape[1]
    u = u_ref[0]
    halo = halo_ref[0] * (i > 0).astype(F32)
    ext = jnp.concatenate([halo, u], axis=0)
    pos = i * tm + lax.broadcasted_iota(jnp.int32, (tm, 1), 0)
    gw = wp_ref.shape[-1]
    sums = ext
    shift = 1
    parts = []
    for g, w in enumerate(POOL_WINDOWS):
        while shift < w:
            sums = sums + pltpu.roll(sums, shift, 0)
            shift *= 2
        total = sums[POOL_HALO:, g * gw:(g + 1) * gw]
        count = jnp.minimum(pos + 1, w).astype(F32)
        pooled = total / count - u[:, g * gw:(g + 1) * gw]
        parts.append(_dot(pooled.astype(BF16), wp_ref[g]))
    y_pool = (jnp.concatenate(parts, axis=1) * ps_ref[...]).astype(BF16)
    pw = y_pool.shape[1]
    y = _dot(y_pool, wo_ref[:pw, :]) + _dot(yd_ref[0], wo_ref[pw:, :])
    o_ref[0] = x_ref[0] + _gate(mod_ref[0, 0], sub) * y


def _even_out_proj(x, mod, layer, u, y_diff, w_pool, pool_scale, w_out):
    b, s, d = x.shape
    pw = u.shape[-1]
    dw = y_diff.shape[-1]
    tm = PROJ_ROWS
    row = lambda bi, i: (bi, i, 0)
    halo_blocks = tm // POOL_HALO
    return pl.pallas_call(
        functools.partial(_even_out_kernel, sub=1),
        grid=(b, s // tm),
        in_specs=[pl.BlockSpec((1, tm, d), row),
                  pl.BlockSpec((1, 1, 3 * N_SUB, d), lambda bi, i: (layer, bi, 0, 0)),
                  pl.BlockSpec((1, tm, pw), row),
                  pl.BlockSpec((1, POOL_HALO, pw),
                               lambda bi, i: (bi, jnp.maximum(i * halo_blocks - 1, 0), 0)),
                  pl.BlockSpec((1, tm, dw), row),
                  _resident(w_pool.shape, lambda bi, i: (0, 0, 0)),
                  pl.BlockSpec((1, pw), lambda bi, i: (0, 0)),
                  _resident(w_out.shape, lambda bi, i: (0, 0))],
        out_specs=pl.BlockSpec((1, tm, d), row),
        out_shape=jax.ShapeDtypeStruct((b, s, d), F32),
        compiler_params=_params("arbitrary", "arbitrary"),
        name="even_out_proj",
    )(x, mod, u, u, y_diff, w_pool.astype(BF16), pool_scale.reshape(1, pw), w_out.astype(BF16))


def _odd_in_kernel(x_ref, mod_ref, g_ref, w_ref, q_ref, k_ref, v_ref, o_ref, gates_ref, *, sub):
    h = _modnorm(x_ref[0], g_ref[...], mod_ref[0, 0], sub).astype(BF16)
    p = _dot(h, w_ref[...])
    nq = q_ref.shape[-1]
    nv = v_ref.shape[-1]
    q_ref[0] = p[:, :nq].astype(BF16)
    k_ref[0] = (p[:, nq:2 * nq] * (MLSTM_QK_DIM ** -0.5)).astype(BF16)
    v_ref[0] = p[:, 2 * nq:2 * nq + nv].astype(BF16)
    o_ref[0] = p[:, 2 * nq + nv:2 * nq + 2 * nv].astype(BF16)
    g0 = 2 * nq + 2 * nv
    gates_ref[0] = p[:, g0:g0 + gates_ref.shape[-1]]


def _odd_in_proj(x, mod, layer, g, w_in, nq, nv):
    b, s, d = x.shape
    n_gates = w_in.shape[1] - 2 * nq - 2 * nv
    w = jnp.pad(w_in, ((0, 0), (0, LANES - n_gates))).astype(BF16)
    n = w.shape[1]
    tm = PROJ_ROWS
    row = lambda bi, i: (bi, i, 0)
    return pl.pallas_call(
        functools.partial(_odd_in_kernel, sub=1),
        grid=(b, s // tm),
        in_specs=[pl.BlockSpec((1, tm, d), row),
                  pl.BlockSpec((1, 1, 3 * N_SUB, d), lambda bi, i: (layer, bi, 0, 0)),
                  pl.BlockSpec((1, d), lambda bi, i: (0, 0)),
                  _resident((d, n), lambda bi, i: (0, 0))],
        out_specs=[pl.BlockSpec((1, tm, nq), row),
                   pl.BlockSpec((1, tm, nq), row),
                   pl.BlockSpec((1, tm, nv), row),
                   pl.BlockSpec((1, tm, nv), row),
                   pl.BlockSpec((1, tm, n_gates), row)],
        out_shape=[jax.ShapeDtypeStruct((b, s, nq), BF16),
                   jax.ShapeDtypeStruct((b, s, nq), BF16),
                   jax.ShapeDtypeStruct((b, s, nv), BF16),
                   jax.ShapeDtypeStruct((b, s, nv), BF16),
                   jax.ShapeDtypeStruct((b, s, n_gates), F32)],
        compiler_params=_params("arbitrary", "arbitrary"),
        name="odd_in_proj",
    )(x, mod, g.reshape(1, d), w)


def _log_sigmoid(x):
    return jnp.minimum(x, 0.0) - jnp.log1p(jnp.exp(-jnp.abs(x)))


def _lane_scan(x, combine, fill):
    lane = lax.broadcasted_iota(jnp.int32, x.shape, 1)
    shift = 1
    while shift < x.shape[-1]:
        x = combine(x, jnp.where(lane >= shift, pltpu.roll(x, shift, 1), fill))
        shift *= 2
    return x


def _mlstm_kernel(bg_ref, q_ref, k_ref, v_ref, og_ref, ig_ref, fg_ref, ig_next_ref, fg_next_ref,
                  ng_ref, y_ref, c_ref, m_ref, cols_ref, src_ref, w_ref, sold_ref, sloc_ref, *, chunks):
    head = pl.program_id(1)
    step = pl.program_id(2)
    L = MLSTM_CHUNK
    G = chunks

    def store_gate_stats(i_ref, f_ref):
        i_all = i_ref[0, 0] + bg_ref[head]
        logf = _log_sigmoid(f_ref[0, 0] + bg_ref[MLSTM_HEADS + head])
        b_all = _lane_scan(logf, jnp.add, 0.0)
        b_last = b_all[:, L - 1:L]
        a_all = b_last - b_all + i_all
        m_loc = jnp.max(a_all, axis=1, keepdims=True)
        w_ref[...] = jnp.exp(a_all - m_loc)

        m_in = []
        m = m_ref[...]
        for g in range(G):
            m_in.append(m)
            m = jnp.maximum(b_last[g:g + 1] + m, m_loc[g:g + 1])
        m_out = m_in[1:] + [m]
        m_ref[...] = m
        m_prev = jnp.concatenate(m_in, axis=0)
        m_next = jnp.concatenate(m_out, axis=0)
        sold_ref[...] = jnp.exp(b_last + m_prev - m_next)
        sloc_ref[...] = jnp.exp(m_loc - m_next)

        src = i_all - b_all
        src_ref[...] = src
        inter_log = b_all + m_prev
        m_t = jnp.maximum(inter_log, b_all + _lane_scan(src, jnp.maximum, -jnp.inf))
        per_row = jnp.concatenate([b_all - m_t, jnp.exp(inter_log - m_t), jnp.exp(-m_t)], axis=0)
        per_row = jnp.concatenate([per_row, jnp.zeros((L - 3 * G, L), F32)], axis=0)
        cols_ref[...] = per_row.T

    @pl.when(step == 0)
    def _():
        c_ref[...] = jnp.zeros_like(c_ref)
        m_ref[...] = jnp.zeros_like(m_ref)
        store_gate_stats(ig_ref, fg_ref)

    cols = cols_ref[...]
    src = src_ref[...]
    w_all = w_ref[...]
    s_old = sold_ref[...]
    s_loc = sloc_ref[...]
    store_gate_stats(ig_next_ref, fg_next_ref)

    causal = (lax.broadcasted_iota(jnp.int32, (L, L), 1) <= lax.broadcasted_iota(jnp.int32, (L, L), 0))
    dv = v_ref.shape[-1]
    ones_col = (lax.broadcasted_iota(jnp.int32, (L, LANES), 1) == 0).astype(BF16)
    state = c_ref[...]
    for g in range(G):
        rows = slice(g * L, (g + 1) * L)
        q = q_ref[0, rows, :]
        k = k_ref[0, rows, :]
        v_ext = jnp.concatenate([v_ref[0, rows, :], ones_col], axis=1)
        dest = cols[:, g:g + 1]
        inter_w = cols[:, G + g:G + g + 1]
        floor = cols[:, 2 * G + g:2 * G + g + 1]

        dw = jnp.exp(jnp.where(causal, dest + src[g:g + 1], -jnp.inf))
        s = lax.dot_general(q, k, (((1,), (1,)), ((), ())), preferred_element_type=F32) * dw
        nd = _dot(s.astype(BF16), v_ext) + inter_w * _dot(q, state.astype(BF16))
        num = nd[:, :dv]
        den = nd[:, dv:dv + 1]
        r = 1.0 / jnp.maximum(jnp.abs(den), floor)
        scale = r * lax.rsqrt(r * r * jnp.mean(num * num, axis=-1, keepdims=True) + EPS)
        hn = num * scale * ng_ref[...]
        y_ref[0, rows, :] = (jax.nn.sigmoid(og_ref[0, rows, :].astype(F32)) * hn).astype(y_ref.dtype)

        kw_t = (k.astype(F32).T * w_all[g:g + 1]).astype(BF16)
        state = s_old[g:g + 1] * state + s_loc[g:g + 1] * _dot(kw_t, v_ext)
    c_ref[...] = state


def _mlstm(q, k, v, o, gates, b_gates, norm_g):
    b, s, _ = q.shape
    dk, dv, L = MLSTM_QK_DIM, MLSTM_V_DIM, MLSTM_CHUNK
    nc = s // L
    g = MLSTM_CHUNKS_PER_STEP
    rows = g * L
    ig = gates[..., :MLSTM_HEADS].transpose(0, 2, 1).reshape(b, MLSTM_HEADS, nc, L)
    fg = gates[..., MLSTM_HEADS:2 * MLSTM_HEADS].transpose(0, 2, 1).reshape(b, MLSTM_HEADS, nc, L)
    gate_spec = pl.BlockSpec((1, 1, g, L), lambda bi, h, c: (bi, h, c, 0))
    last = nc // g - 1
    next_gate_spec = pl.BlockSpec((1, 1, g, L), lambda bi, h, c: (bi, h, jnp.minimum(c + 1, last), 0))
    return pl.pallas_call(
        functools.partial(_mlstm_kernel, chunks=g),
        grid=(b, MLSTM_HEADS, nc // g),
        in_specs=[pl.BlockSpec(memory_space=pltpu.SMEM),
                  pl.BlockSpec((1, rows, dk), lambda bi, h, c: (bi, c, h)),
                  pl.BlockSpec((1, rows, dk), lambda bi, h, c: (bi, c, h)),
                  pl.BlockSpec((1, rows, dv), lambda bi, h, c: (bi, c, h)),
                  pl.BlockSpec((1, rows, dv), lambda bi, h, c: (bi, c, h)),
                  gate_spec, gate_spec, next_gate_spec, next_gate_spec,
                  pl.BlockSpec((1, dv), lambda bi, h, c: (0, h))],
        out_specs=pl.BlockSpec((1, rows, dv), lambda bi, h, c: (bi, c, h)),
        out_shape=jax.ShapeDtypeStruct((b, s, MLSTM_HEADS * dv), BF16),
        scratch_shapes=[pltpu.VMEM((dk, dv + LANES), F32), pltpu.VMEM((1, 1), F32),
                        pltpu.VMEM((L, L), F32), pltpu.VMEM((g, L), F32), pltpu.VMEM((g, L), F32),
                        pltpu.VMEM((g, 1), F32), pltpu.VMEM((g, 1), F32)],
        compiler_params=_params("arbitrary", "arbitrary", "arbitrary"),
        name="mlstm",
    )(b_gates.reshape(-1), q, k, v, o, ig, fg, ig, fg, norm_g.reshape(1, -1))


def _out_proj_kernel(x_ref, mod_ref, y_ref, w_ref, o_ref, *, sub):
    o_ref[0] = x_ref[0] + _gate(mod_ref[0, 0], sub) * _dot(y_ref[0], w_ref[...])


def _out_proj(x, mod, layer, y, w_out):
    b, s, d = x.shape
    n_in = y.shape[-1]
    tm = PROJ_ROWS
    row = lambda bi, i: (bi, i, 0)
    return pl.pallas_call(
        functools.partial(_out_proj_kernel, sub=1),
        grid=(b, s // tm),
        in_specs=[pl.BlockSpec((1, tm, d), row),
                  pl.BlockSpec((1, 1, 3 * N_SUB, d), lambda bi, i: (layer, bi, 0, 0)),
                  pl.BlockSpec((1, tm, n_in), row),
                  _resident((n_in, d), lambda bi, i: (0, 0))],
        out_specs=pl.BlockSpec((1, tm, d), row),
        out_shape=jax.ShapeDtypeStruct((b, s, d), F32),
        compiler_params=_params("arbitrary", "arbitrary"),
        name="out_proj",
    )(x, mod, y, w_out.astype(BF16))


def kernel(x, c, w_mod, b_mod, norm_g, w_ffn_gate, w_ffn_up, w_ffn_down, w_in_even, w_pool, pool_scale, diff_lambda, diff_subln_g, w_out_even, w_in_odd, b_gates_odd, mlstm_norm_g, w_out_odd, final_g):
    depth = w_mod.shape[0]
    mod = _modulation(c, w_mod, b_mod)
    pool_width = pool_scale.shape[-1]
    qk_width = DIFF_HEADS * 2 * DIFF_HEAD_DIM
    ffn_w = (w_ffn_gate.astype(BF16), w_ffn_up.astype(BF16), w_ffn_down.astype(BF16))
    for l in range(depth):
        x = _ffn(x, mod, l, 0, 0, norm_g[l, 0], *ffn_w, final_g, False)
        if l % 2 == 0:
            e = l // 2
            lambda_init = 0.8 - 0.6 * math.exp(-0.3 * l)
            u, k, qt, vt = _even_in_proj(x, mod, l, norm_g[l, 1], w_in_even[e], pool_width, qk_width)
            y_diff = _diff_attention(qt, k, vt, diff_lambda[e], diff_subln_g[e], lambda_init)
            x = _even_out_proj(x, mod, l, u, y_diff, w_pool[e], pool_scale[e], w_out_even[e])
        else:
            o = l // 2
            q, k, v, og, gates = _odd_in_proj(x, mod, l, norm_g[l, 1], w_in_odd[o],
                                              MLSTM_HEADS * MLSTM_QK_DIM, MLSTM_HEADS * MLSTM_V_DIM)
            y = _mlstm(q, k, v, og, gates, b_gates_odd[o], mlstm_norm_g[o])
            x = _out_proj(x, mod, l, y, w_out_odd[o])
        x = _ffn(x, mod, l, 2, 1, norm_g[l, 2], *ffn_w, final_g, l == depth - 1)
    return x
```

```python
import functools
import math

import jax
import jax.numpy as jnp
from jax import lax
from jax.experimental import pallas as pl
from jax.experimental.pallas import tpu as pltpu

F32 = jnp.float32
BF16 = jnp.bfloat16

EPS = 1e-6
N_SUB = 3
POOL_WINDOWS = (2, 4, 8, 16)
POOL_HALO = 16
DIFF_HEADS = 4
DIFF_HEAD_DIM = 64
MLSTM_HEADS = 4
MLSTM_QK_DIM = 128
MLSTM_V_DIM = 256
MLSTM_CHUNK = 128
LANES = 128

VMEM_LIMIT_BYTES = 56 * 1024 * 1024

FFN_ROWS = 512
PROJ_ROWS = 512
ATTN_ROWS = 512
ATTN_COL_GROUP = 256
LOG2_E = math.log2(math.e)
MLSTM_CHUNKS_PER_STEP = 8


def _params(*semantics):
    return pltpu.CompilerParams(dimension_semantics=semantics,
                                vmem_limit_bytes=VMEM_LIMIT_BYTES)


def _resident(block_shape, index_map):
    return pl.BlockSpec(block_shape, index_map, pipeline_mode=pl.Buffered(1))


def _dot(a, b):
    return jnp.dot(a, b, preferred_element_type=F32)


def _modnorm(x, g, mod, sub):
    shift = mod[3 * sub:3 * sub + 1]
    scale = mod[3 * sub + 1:3 * sub + 2]
    y = x * lax.rsqrt(jnp.mean(x * x, axis=-1, keepdims=True) + EPS)
    return (y * g) * (1.0 + scale) + shift


def _gate(mod, sub):
    return mod[3 * sub + 2:3 * sub + 3]


def _mod_kernel(c_ref, w_ref, b_ref, o_ref):
    c = c_ref[...]
    c_act = c * jax.nn.sigmoid(c)
    o_ref[0] = jnp.dot(c_act, w_ref[0], precision=lax.Precision.HIGHEST,
                       preferred_element_type=F32) + b_ref[0]


def _modulation(c, w_mod, b_mod):
    depth, d, n = w_mod.shape
    b = c.shape[0]
    tn = d
    out = pl.pallas_call(
        _mod_kernel,
        grid=(depth, n // tn),
        in_specs=[pl.BlockSpec((b, d), lambda l, j: (0, 0)),
                  pl.BlockSpec((1, d, tn), lambda l, j: (l, 0, j)),
                  pl.BlockSpec((1, 1, tn), lambda l, j: (l, 0, j))],
        out_specs=pl.BlockSpec((1, b, tn), lambda l, j: (l, 0, j)),
        out_shape=jax.ShapeDtypeStruct((depth, b, n), F32),
        compiler_params=_params("arbitrary", "arbitrary"),
        name="modulation",
    )(c, w_mod, b_mod.reshape(depth, 1, n))
    return out.reshape(depth, b, 3 * N_SUB, d)


def _swiglu_residual(x, mod, sub, g, wg_ref, wu_ref, wd_ref, fg, final_norm):
    h = _modnorm(x, g, mod, sub).astype(BF16)
    gate_act = _dot(h, wg_ref[0, 0])
    up = _dot(h, wu_ref[0, 0])
    a = (gate_act * jax.nn.sigmoid(gate_act) * up).astype(BF16)
    out = x + (0.5 * _gate(mod, sub)) * _dot(a, wd_ref[0, 0])
    if final_norm:
        out = out * lax.rsqrt(jnp.mean(out * out, axis=-1, keepdims=True) + EPS) * fg
    return out


def _ffn_kernel(x_ref, mod_ref, g_ref, wg_ref, wu_ref, wd_ref, fg_ref, o_ref, *, sub, final_norm):
    o_ref[0] = _swiglu_residual(x_ref[0], mod_ref[0, 0], sub, g_ref[...], wg_ref, wu_ref, wd_ref,
                                fg_ref[...], final_norm)


def _odd_mix_ffn_kernel(x_ref, mod_ref, g_ref, wg_ref, wu_ref, wd_ref, fg_ref, y_ref, wo_ref, o_ref, *,
                        sub, final_norm):
    mod = mod_ref[0, 0]
    x = x_ref[0] + _gate(mod, 1) * _dot(y_ref[0], wo_ref[...])
    o_ref[0] = _swiglu_residual(x, mod, sub, g_ref[...], wg_ref, wu_ref, wd_ref, fg_ref[...], final_norm)


def _pool_mixer(u, halo, first_row, wp_ref, scale):
    tm = u.shape[0]
    ext = jnp.concatenate([halo, u], axis=0)
    pos = first_row + lax.broadcasted_iota(jnp.int32, (tm, 1), 0)
    gw = wp_ref.shape[-1]
    sums = ext
    shift = 1
    parts = []
    for g, w in enumerate(POOL_WINDOWS):
        while shift < w:
            sums = sums + pltpu.roll(sums, shift, 0)
            shift *= 2
        total = sums[POOL_HALO:, g * gw:(g + 1) * gw]
        count = jnp.minimum(pos + 1, w).astype(F32)
        pooled = total / count - u[:, g * gw:(g + 1) * gw]
        parts.append(_dot(pooled.astype(BF16), wp_ref[g]))
    return (jnp.concatenate(parts, axis=1) * scale).astype(BF16)


def _even_mix_ffn_kernel(x_ref, mod_ref, g_ref, wg_ref, wu_ref, wd_ref, fg_ref, u_ref, halo_ref, yd_ref,
                         wp_ref, ps_ref, wo_ref, o_ref, *, sub, final_norm):
    i = pl.program_id(1)
    mod = mod_ref[0, 0]
    tm = u_ref.shape[1]
    halo = halo_ref[0] * (i > 0).astype(F32)
    y_pool = _pool_mixer(u_ref[0], halo, i * tm, wp_ref, ps_ref[...])
    pw = y_pool.shape[1]
    y = _dot(y_pool, wo_ref[:pw, :]) + _dot(yd_ref[0], wo_ref[pw:, :])
    x = x_ref[0] + _gate(mod, 1) * y
    o_ref[0] = _swiglu_residual(x, mod, sub, g_ref[...], wg_ref, wu_ref, wd_ref, fg_ref[...], final_norm)


def _ffn(x, mod, layer, sub, which, g, w_gate, w_up, w_down, final_g, final_norm, mix=None):
    b, s, d = x.shape
    f = w_gate.shape[-1]
    tm = FFN_ROWS
    pick = lambda bi, i: (layer, which, 0, 0)
    row = lambda bi, i: (bi, i, 0)
    in_specs = [pl.BlockSpec((1, tm, d), row),
                pl.BlockSpec((1, 1, 3 * N_SUB, d), lambda bi, i: (layer, bi, 0, 0)),
                pl.BlockSpec((1, d), lambda bi, i: (0, 0)),
                _resident((1, 1, d, f), pick),
                _resident((1, 1, d, f), pick),
                _resident((1, 1, f, d), pick),
                pl.BlockSpec((1, d), lambda bi, i: (0, 0))]
    args = [x, mod, g.reshape(1, d), w_gate, w_up, w_down, final_g.reshape(1, d)]
    if mix is None:
        body, name = _ffn_kernel, "ffn"
    elif mix[0] == "odd":
        _, y, w_out = mix
        body, name = _odd_mix_ffn_kernel, "odd_mix_ffn"
        in_specs += [pl.BlockSpec((1, tm, y.shape[-1]), row),
                     _resident(w_out.shape, lambda bi, i: (0, 0))]
        args += [y, w_out.astype(BF16)]
    else:
        _, u, y_diff, w_pool, pool_scale, w_out = mix
        body, name = _even_mix_ffn_kernel, "even_mix_ffn"
        pw = u.shape[-1]
        halo_blocks = tm // POOL_HALO
        in_specs += [pl.BlockSpec((1, tm, pw), row),
                     pl.BlockSpec((1, POOL_HALO, pw),
                                  lambda bi, i: (bi, jnp.maximum(i * halo_blocks - 1, 0), 0)),
                     pl.BlockSpec((1, tm, y_diff.shape[-1]), row),
                     _resident(w_pool.shape, lambda bi, i: (0, 0, 0)),
                     pl.BlockSpec((1, pw), lambda bi, i: (0, 0)),
                     _resident(w_out.shape, lambda bi, i: (0, 0))]
        args += [u, u, y_diff, w_pool.astype(BF16), pool_scale.reshape(1, pw), w_out.astype(BF16)]
    return pl.pallas_call(
        functools.partial(body, sub=sub, final_norm=final_norm),
        grid=(b, s // tm),
        in_specs=in_specs,
        out_specs=pl.BlockSpec((1, tm, d), row),
        out_shape=jax.ShapeDtypeStruct((b, s, d), F32),
        compiler_params=_params("arbitrary", "arbitrary"),
        name=name,
    )(*args)


def _even_in_kernel(x_ref, mod_ref, g_ref, w_ref, wt_ref, u_ref, k_ref, qt_ref, vt_ref, *, sub):
    h = _modnorm(x_ref[0], g_ref[...], mod_ref[0, 0], sub).astype(BF16)
    p = _dot(h, w_ref[...])
    pw = u_ref.shape[-1]
    u_ref[0] = p[:, :pw]
    k_ref[0] = p[:, pw:].astype(BF16)
    pt = lax.dot_general(wt_ref[...], h, (((1,), (1,)), ((), ())), preferred_element_type=F32)
    qw = qt_ref.shape[1]
    qt_ref[0] = (pt[:qw] * (DIFF_HEAD_DIM ** -0.5 * LOG2_E)).astype(BF16)
    vt_ref[0, 0] = pt[qw:].astype(BF16)


def _even_in_proj(x, mod, layer, g, w_in, pool_width, qk_width):
    b, s, d = x.shape
    n = w_in.shape[1]
    v_width = n - pool_width - 2 * qk_width
    tm = ATTN_ROWS
    k0 = pool_width + qk_width
    w_uk = jnp.concatenate([w_in[:, :pool_width], w_in[:, k0:k0 + qk_width]], axis=1).astype(BF16)
    w_qv_t = jnp.concatenate([w_in[:, pool_width:k0], w_in[:, k0 + qk_width:]], axis=1).T.astype(BF16)
    row = lambda bi, i: (bi, i, 0)
    return pl.pallas_call(
        functools.partial(_even_in_kernel, sub=1),
        grid=(b, s // tm),
        in_specs=[pl.BlockSpec((1, tm, d), row),
                  pl.BlockSpec((1, 1, 3 * N_SUB, d), lambda bi, i: (layer, bi, 0, 0)),
                  pl.BlockSpec((1, d), lambda bi, i: (0, 0)),
                  _resident(w_uk.shape, lambda bi, i: (0, 0)),
                  _resident(w_qv_t.shape, lambda bi, i: (0, 0))],
        out_specs=[pl.BlockSpec((1, tm, pool_width), row),
                   pl.BlockSpec((1, tm, qk_width), row),
                   pl.BlockSpec((1, qk_width, tm), lambda bi, i: (bi, 0, i)),
                   pl.BlockSpec((1, 1, v_width, tm), lambda bi, i: (bi, i, 0, 0))],
        out_shape=[jax.ShapeDtypeStruct((b, s, pool_width), F32),
                   jax.ShapeDtypeStruct((b, s, qk_width), BF16),
                   jax.ShapeDtypeStruct((b, qk_width, s), BF16),
                   jax.ShapeDtypeStruct((b, s // tm, v_width, tm), BF16)],
        compiler_params=_params("arbitrary", "arbitrary"),
        name="even_in_proj",
    )(x, mod, g.reshape(1, d), w_uk, w_qv_t)


def _diff_attn_kernel(qt_ref, k_ref, vt_ref, lam_ref, sg_ref, o_ref, m_ref, l_ref, acc_ref,
                      s0_ref, s1_ref, cm0_ref, cm1_ref, *, lambda_init, t):
    i = pl.program_id(2)
    qt = qt_ref[0].astype(F32)
    feat = lax.broadcasted_iota(jnp.int32, qt.shape, 0)
    qs = jnp.concatenate([jnp.where(feat < DIFF_HEAD_DIM, qt, 0.0),
                          jnp.where(feat >= DIFF_HEAD_DIM, qt, 0.0)], axis=1).astype(BF16)
    m_ref[...] = jnp.full_like(m_ref, -jnp.inf)
    l_ref[...] = jnp.zeros_like(l_ref)
    acc_ref[...] = jnp.zeros_like(acc_ref)
    groups = [slice(c, c + ATTN_COL_GROUP) for c in range(0, 2 * t, ATTN_COL_GROUP)]
    s_refs = (s0_ref, s1_ref)
    cm_refs = (cm0_ref, cm1_ref)

    def produce(j, slot, cols):
        kb = k_ref[0, pl.ds(pl.multiple_of(j * t, t), t), :]
        s = _dot(kb, qs[:, cols])
        s_refs[slot][:, cols] = s
        cm_refs[slot][:, cols] = jnp.max(s, axis=0, keepdims=True)

    def consume(j, slot, cols, diagonal):
        s = s_refs[slot][:, cols]
        if diagonal:
            key = lax.broadcasted_iota(jnp.int32, s.shape, 0)
            qry = lax.broadcasted_iota(jnp.int32, s.shape, 1) + cols.start % t
            s = jnp.where(key <= qry, s, -jnp.inf)
            cm = jnp.max(s, axis=0, keepdims=True)
        else:
            cm = cm_refs[slot][:, cols]
        m = m_ref[:, cols]
        m_new = jnp.maximum(m, cm)
        alpha = jnp.exp2(m - m_new)
        p = jnp.exp2(s - m_new)
        l_ref[:, cols] = alpha * l_ref[:, cols] + jnp.sum(p, axis=0, keepdims=True)
        acc_ref[:, cols] = alpha * acc_ref[:, cols] + _dot(vt_ref[0, j], p.astype(BF16))
        m_ref[:, cols] = m_new

    def advance(j, slot):
        for cols in groups:
            produce(j + 1, 1 - slot, cols)
            consume(j, slot, cols, False)

    def diagonal(slot):
        for cols in groups:
            consume(i, slot, cols, True)

    for cols in groups:
        produce(0, 0, cols)

    def body(jj, carry):
        advance(2 * jj, 0)
        advance(2 * jj + 1, 1)
        return carry

    lax.fori_loop(0, i // 2, body, 0)

    @pl.when(i % 2 == 0)
    def _():
        diagonal(0)

    @pl.when(i % 2 == 1)
    def _():
        advance(i - 1, 0)
        diagonal(1)

    o = acc_ref[...] / l_ref[...]
    lp = lam_ref[...]
    lam = (jnp.exp(jnp.sum(lp[0:1] * lp[1:2], axis=1, keepdims=True))
           - jnp.exp(jnp.sum(lp[2:3] * lp[3:4], axis=1, keepdims=True)) + lambda_init)
    o = o[:, :t] - lam * o[:, t:]
    o = o * lax.rsqrt(jnp.mean(o * o, axis=0, keepdims=True) + EPS) * sg_ref[...]
    o_ref[0] = (o * (1.0 - lambda_init)).T.astype(o_ref.dtype)


def _diff_attention(qt, k, vt, lam_params, subln_g, lambda_init):
    b, s, _ = k.shape
    hd = 2 * DIFF_HEAD_DIM
    t = ATTN_ROWS
    assert s % t == 0 and vt.shape[-1] == t and (2 * t) % ATTN_COL_GROUP == 0
    return pl.pallas_call(
        functools.partial(_diff_attn_kernel, lambda_init=lambda_init, t=t),
        grid=(b, DIFF_HEADS, s // t),
        in_specs=[pl.BlockSpec((1, hd, t), lambda bi, h, i: (bi, h, i)),
                  pl.BlockSpec((1, s, hd), lambda bi, h, i: (bi, 0, h)),
                  pl.BlockSpec((1, s // t, hd, t), lambda bi, h, i: (bi, 0, h, 0)),
                  pl.BlockSpec(lam_params.shape, lambda bi, h, i: (0, 0)),
                  pl.BlockSpec((hd, 1), lambda bi, h, i: (0, 0))],
        out_specs=pl.BlockSpec((1, t, hd), lambda bi, h, i: (bi, i, h)),
        out_shape=jax.ShapeDtypeStruct((b, s, DIFF_HEADS * hd), BF16),
        scratch_shapes=[pltpu.VMEM((1, 2 * t), F32), pltpu.VMEM((1, 2 * t), F32),
                        pltpu.VMEM((hd, 2 * t), F32),
                        pltpu.VMEM((t, 2 * t), F32), pltpu.VMEM((t, 2 * t), F32),
                        pltpu.VMEM((1, 2 * t), F32), pltpu.VMEM((1, 2 * t), F32)],
        compiler_params=_params("arbitrary", "arbitrary", "arbitrary"),
        name="diff_attention",
    )(qt, k, vt, lam_params, subln_g.reshape(hd, 1))


def _odd_in_kernel(x_ref, mod_ref, g_ref, w_ref, wgate_ref, q_ref, k_ref, v_ref, o_ref, gates_ref, *, sub):
    h = _modnorm(x_ref[0], g_ref[...], mod_ref[0, 0], sub).astype(BF16)
    p = _dot(h, w_ref[...])
    nq = q_ref.shape[-1]
    nv = v_ref.shape[-1]
    q_ref[0] = p[:, :nq].astype(BF16)
    k_ref[0] = (p[:, nq:2 * nq] * (MLSTM_QK_DIM ** -0.5)).astype(BF16)
    v_ref[0] = p[:, 2 * nq:2 * nq + nv].astype(BF16)
    o_ref[0] = p[:, 2 * nq + nv:].astype(BF16)
    gates_ref[0] = _dot(h, wgate_ref[...]).T[:gates_ref.shape[1]]


def _odd_in_proj(x, mod, layer, g, w_in, nq, nv):
    b, s, d = x.shape
    n_main = 2 * nq + 2 * nv
    n_gates = w_in.shape[1] - n_main
    w_main = w_in[:, :n_main].astype(BF16)
    w_gate = jnp.pad(w_in[:, n_main:], ((0, 0), (0, LANES - n_gates))).astype(BF16)
    tm = PROJ_ROWS
    row = lambda bi, i: (bi, i, 0)
    return pl.pallas_call(
        functools.partial(_odd_in_kernel, sub=1),
        grid=(b, s // tm),
        in_specs=[pl.BlockSpec((1, tm, d), row),
                  pl.BlockSpec((1, 1, 3 * N_SUB, d), lambda bi, i: (layer, bi, 0, 0)),
                  pl.BlockSpec((1, d), lambda bi, i: (0, 0)),
                  _resident((d, n_main), lambda bi, i: (0, 0)),
                  _resident((d, LANES), lambda bi, i: (0, 0))],
        out_specs=[pl.BlockSpec((1, tm, nq), row),
                   pl.BlockSpec((1, tm, nq), row),
                   pl.BlockSpec((1, tm, nv), row),
                   pl.BlockSpec((1, tm, nv), row),
                   pl.BlockSpec((1, n_gates, tm), lambda bi, i: (bi, 0, i))],
        out_shape=[jax.ShapeDtypeStruct((b, s, nq), BF16),
                   jax.ShapeDtypeStruct((b, s, nq), BF16),
                   jax.ShapeDtypeStruct((b, s, nv), BF16),
                   jax.ShapeDtypeStruct((b, s, nv), BF16),
                   jax.ShapeDtypeStruct((b, n_gates, s), F32)],
        compiler_params=_params("arbitrary", "arbitrary"),
        name="odd_in_proj",
    )(x, mod, g.reshape(1, d), w_main, w_gate)


def _log_sigmoid(x):
    return jnp.minimum(x, 0.0) - jnp.log1p(jnp.exp(-jnp.abs(x)))


def _lane_scan(x, combine, fill):
    lane = lax.broadcasted_iota(jnp.int32, x.shape, 1)
    shift = 1
    while shift < x.shape[-1]:
        x = combine(x, jnp.where(lane >= shift, pltpu.roll(x, shift, 1), fill))
        shift *= 2
    return x


def _mlstm_kernel(bg_ref, q_ref, k_ref, v_ref, og_ref, ig_ref, fg_ref, ig_next_ref, fg_next_ref,
                  ng_ref, y_ref, c_ref, m_ref, cols_ref, src_ref, w_ref, sold_ref, sloc_ref, *, chunks):
    head = pl.program_id(1)
    step = pl.program_id(2)
    L = MLSTM_CHUNK
    G = chunks

    def store_gate_stats(i_ref, f_ref):
        i_all = i_ref[0, 0] + bg_ref[head]
        logf = _log_sigmoid(f_ref[0, 0] + bg_ref[MLSTM_HEADS + head])
        b_all = _lane_scan(logf, jnp.add, 0.0)
        b_last = b_all[:, L - 1:L]
        a_all = b_last - b_all + i_all
        m_loc = jnp.max(a_all, axis=1, keepdims=True)
        w_ref[...] = jnp.exp(a_all - m_loc)

        m_in = []
        m = m_ref[...]
        for g in range(G):
            m_in.append(m)
            m = jnp.maximum(b_last[g:g + 1] + m, m_loc[g:g + 1])
        m_out = m_in[1:] + [m]
        m_ref[...] = m
        m_prev = jnp.concatenate(m_in, axis=0)
        m_next = jnp.concatenate(m_out, axis=0)
        sold_ref[...] = jnp.exp(b_last + m_prev - m_next)
        sloc_ref[...] = jnp.exp(m_loc - m_next)

        src = i_all - b_all
        src_ref[...] = src
        inter_log = b_all + m_prev
        m_t = jnp.maximum(inter_log, b_all + _lane_scan(src, jnp.maximum, -jnp.inf))
        per_row = jnp.concatenate([b_all - m_t, jnp.exp(inter_log - m_t), jnp.exp(-m_t)], axis=0)
        per_row = jnp.concatenate([per_row, jnp.zeros((L - 3 * G, L), F32)], axis=0)
        cols_ref[...] = per_row.T

    @pl.when(step == 0)
    def _():
        c_ref[...] = jnp.zeros_like(c_ref)
        m_ref[...] = jnp.zeros_like(m_ref)
        store_gate_stats(ig_ref, fg_ref)

    cols = cols_ref[...]
    src = src_ref[...]
    w_all = w_ref[...]
    s_old = sold_ref[...]
    s_loc = sloc_ref[...]
    store_gate_stats(ig_next_ref, fg_next_ref)

    causal = (lax.broadcasted_iota(jnp.int32, (L, L), 1) <= lax.broadcasted_iota(jnp.int32, (L, L), 0))
    dv = v_ref.shape[-1]
    ones_col = (lax.broadcasted_iota(jnp.int32, (L, LANES), 1) == 0).astype(BF16)
    state = c_ref[...]
    for g in range(G):
        rows = slice(g * L, (g + 1) * L)
        q = q_ref[0, rows, :]
        k = k_ref[0, rows, :]
        v_ext = jnp.concatenate([v_ref[0, rows, :], ones_col], axis=1)
        dest = cols[:, g:g + 1]
        inter_w = cols[:, G + g:G + g + 1]
        floor = cols[:, 2 * G + g:2 * G + g + 1]

        dw = jnp.exp(jnp.where(causal, dest + src[g:g + 1], -jnp.inf))
        s = lax.dot_general(q, k, (((1,), (1,)), ((), ())), preferred_element_type=F32) * dw
        nd = _dot(s.astype(BF16), v_ext) + inter_w * _dot(q, state.astype(BF16))
        num = nd[:, :dv]
        den = nd[:, dv:dv + 1]
        r = 1.0 / jnp.maximum(jnp.abs(den), floor)
        scale = r * lax.rsqrt(r * r * jnp.mean(num * num, axis=-1, keepdims=True) + EPS)
        hn = num * scale * ng_ref[...]
        y_ref[0, rows, :] = (jax.nn.sigmoid(og_ref[0, rows, :].astype(F32)) * hn).astype(y_ref.dtype)

        kw_t = (k.astype(F32).T * w_all[g:g + 1]).astype(BF16)
        state = s_old[g:g + 1] * state + s_loc[g:g + 1] * _dot(kw_t, v_ext)
    c_ref[...] = state


def _mlstm(q, k, v, o, gates, b_gates, norm_g):
    b, s, _ = q.shape
    dk, dv, L = MLSTM_QK_DIM, MLSTM_V_DIM, MLSTM_CHUNK
    nc = s // L
    g = MLSTM_CHUNKS_PER_STEP
    rows = g * L
    gates = gates.reshape(b, 2 * MLSTM_HEADS, nc, L)
    last = nc // g - 1

    def gate_spec(first, ahead):
        return pl.BlockSpec((1, 1, g, L),
                            lambda bi, h, c: (bi, first + h, jnp.minimum(c + ahead, last), 0))

    return pl.pallas_call(
        functools.partial(_mlstm_kernel, chunks=g),
        grid=(b, MLSTM_HEADS, nc // g),
        in_specs=[pl.BlockSpec(memory_space=pltpu.SMEM),
                  pl.BlockSpec((1, rows, dk), lambda bi, h, c: (bi, c, h)),
                  pl.BlockSpec((1, rows, dk), lambda bi, h, c: (bi, c, h)),
                  pl.BlockSpec((1, rows, dv), lambda bi, h, c: (bi, c, h)),
                  pl.BlockSpec((1, rows, dv), lambda bi, h, c: (bi, c, h)),
                  gate_spec(0, 0), gate_spec(MLSTM_HEADS, 0),
                  gate_spec(0, 1), gate_spec(MLSTM_HEADS, 1),
                  pl.BlockSpec((1, dv), lambda bi, h, c: (0, h))],
        out_specs=pl.BlockSpec((1, rows, dv), lambda bi, h, c: (bi, c, h)),
        out_shape=jax.ShapeDtypeStruct((b, s, MLSTM_HEADS * dv), BF16),
        scratch_shapes=[pltpu.VMEM((dk, dv + LANES), F32), pltpu.VMEM((1, 1), F32),
                        pltpu.VMEM((L, L), F32), pltpu.VMEM((g, L), F32), pltpu.VMEM((g, L), F32),
                        pltpu.VMEM((g, 1), F32), pltpu.VMEM((g, 1), F32)],
        compiler_params=_params("arbitrary", "arbitrary", "arbitrary"),
        name="mlstm",
    )(b_gates.reshape(-1), q, k, v, o, gates, gates, gates, gates, norm_g.reshape(1, -1))


def kernel(x, c, w_mod, b_mod, norm_g, w_ffn_gate, w_ffn_up, w_ffn_down, w_in_even, w_pool, pool_scale, diff_lambda, diff_subln_g, w_out_even, w_in_odd, b_gates_odd, mlstm_norm_g, w_out_odd, final_g):
    depth = w_mod.shape[0]
    mod = _modulation(c, w_mod, b_mod)
    pool_width = pool_scale.shape[-1]
    qk_width = DIFF_HEADS * 2 * DIFF_HEAD_DIM
    ffn_w = (w_ffn_gate.astype(BF16), w_ffn_up.astype(BF16), w_ffn_down.astype(BF16))
    for l in range(depth):
        x = _ffn(x, mod, l, 0, 0, norm_g[l, 0], *ffn_w, final_g, False)
        if l % 2 == 0:
            e = l // 2
            lambda_init = 0.8 - 0.6 * math.exp(-0.3 * l)
            u, k, qt, vt = _even_in_proj(x, mod, l, norm_g[l, 1], w_in_even[e], pool_width, qk_width)
            y_diff = _diff_attention(qt, k, vt, diff_lambda[e], diff_subln_g[e], lambda_init)
            mix = ("even", u, y_diff, w_pool[e], pool_scale[e], w_out_even[e])
        else:
            o = l // 2
            q, k, v, og, gates = _odd_in_proj(x, mod, l, norm_g[l, 1], w_in_odd[o],
                                              MLSTM_HEADS * MLSTM_QK_DIM, MLSTM_HEADS * MLSTM_V_DIM)
            y = _mlstm(q, k, v, og, gates, b_gates_odd[o], mlstm_norm_g[o])
            mix = ("odd", y, w_out_odd[o])
        x = _ffn(x, mod, l, 2, 1, norm_g[l, 2], *ffn_w, final_g, l == depth - 1, mix)
    return x
```

```python
import functools
import math

import jax
import jax.numpy as jnp
from jax import lax
from jax.experimental import pallas as pl
from jax.experimental.pallas import tpu as pltpu

F32 = jnp.float32
BF16 = jnp.bfloat16

EPS = 1e-6
N_SUB = 3
POOL_WINDOWS = (2, 4, 8, 16)
POOL_HALO = 16
DIFF_HEADS = 4
DIFF_HEAD_DIM = 64
MLSTM_HEADS = 4
MLSTM_QK_DIM = 128
MLSTM_V_DIM = 256
MLSTM_CHUNK = 128
LANES = 128

VMEM_LIMIT_BYTES = 56 * 1024 * 1024

FFN_ROWS = 512
FFN_SUB_ROWS = 256
PROJ_ROWS = 512
ATTN_ROWS = 512
ATTN_COL_GROUP = 256
ATTN_UNROLL = 4
LOG2_E = math.log2(math.e)
MLSTM_CHUNKS_PER_STEP = 8


def _params(*semantics):
    return pltpu.CompilerParams(dimension_semantics=semantics,
                                vmem_limit_bytes=VMEM_LIMIT_BYTES)


def _resident(block_shape, index_map):
    return pl.BlockSpec(block_shape, index_map, pipeline_mode=pl.Buffered(1))


def _dot(a, b):
    return jnp.dot(a, b, preferred_element_type=F32)


def _modnorm(x, g, mod, sub):
    shift = mod[3 * sub:3 * sub + 1]
    scale = mod[3 * sub + 1:3 * sub + 2]
    y = x * lax.rsqrt(jnp.mean(x * x, axis=-1, keepdims=True) + EPS)
    return (y * g) * (1.0 + scale) + shift


def _gate(mod, sub):
    return mod[3 * sub + 2:3 * sub + 3]


def _mod_kernel(c_ref, w_ref, b_ref, o_ref):
    c = c_ref[...]
    c_act = c * jax.nn.sigmoid(c)
    w = w_ref[0]
    c_hi = c_act.astype(BF16)
    c_lo = (c_act - c_hi.astype(F32)).astype(BF16)
    w_hi = w.astype(BF16)
    w_lo = (w - w_hi.astype(F32)).astype(BF16)
    o_ref[0] = _dot(c_hi, w_hi) + (_dot(c_hi, w_lo) + _dot(c_lo, w_hi)) + b_ref[0]


def _modulation(c, w_mod, b_mod):
    depth, d, n = w_mod.shape
    b = c.shape[0]
    tn = d
    out = pl.pallas_call(
        _mod_kernel,
        grid=(depth, n // tn),
        in_specs=[pl.BlockSpec((b, d), lambda l, j: (0, 0)),
                  pl.BlockSpec((1, d, tn), lambda l, j: (l, 0, j)),
                  pl.BlockSpec((1, 1, tn), lambda l, j: (l, 0, j))],
        out_specs=pl.BlockSpec((1, b, tn), lambda l, j: (l, 0, j)),
        out_shape=jax.ShapeDtypeStruct((depth, b, n), F32),
        compiler_params=_params("arbitrary", "arbitrary"),
        name="modulation",
    )(c, w_mod, b_mod.reshape(depth, 1, n))
    return out.reshape(depth, b, 3 * N_SUB, d)


def _swiglu_residual(x, mod, sub, g, wg_ref, wu_ref, wd_ref, fg, final_norm):
    outs = []
    for r in range(0, x.shape[0], FFN_SUB_ROWS):
        xs = x[r:r + FFN_SUB_ROWS]
        h = _modnorm(xs, g, mod, sub).astype(BF16)
        gate_act = _dot(h, wg_ref[0, 0])
        up = _dot(h, wu_ref[0, 0])
        a = (gate_act * jax.nn.sigmoid(gate_act) * up).astype(BF16)
        out = xs + (0.5 * _gate(mod, sub)) * _dot(a, wd_ref[0, 0])
        if final_norm:
            out = out * lax.rsqrt(jnp.mean(out * out, axis=-1, keepdims=True) + EPS) * fg
        outs.append(out)
    return jnp.concatenate(outs, axis=0)


def _ffn_kernel(x_ref, mod_ref, g_ref, wg_ref, wu_ref, wd_ref, fg_ref, o_ref, *, sub, final_norm):
    o_ref[0] = _swiglu_residual(x_ref[0], mod_ref[0, 0], sub, g_ref[...], wg_ref, wu_ref, wd_ref,
                                fg_ref[...], final_norm)


def _odd_mix_ffn_kernel(x_ref, mod_ref, g_ref, wg_ref, wu_ref, wd_ref, fg_ref, y_ref, wo_ref, o_ref, *,
                        sub, final_norm):
    mod = mod_ref[0, 0]
    x = x_ref[0] + _gate(mod, 1) * _dot(y_ref[0], wo_ref[...])
    o_ref[0] = _swiglu_residual(x, mod, sub, g_ref[...], wg_ref, wu_ref, wd_ref, fg_ref[...], final_norm)


def _pool_mixer(u, halo, first_row, wp_ref, scale):
    tm = u.shape[0]
    ext = jnp.concatenate([halo, u], axis=0)
    pos = first_row + lax.broadcasted_iota(jnp.int32, (tm, 1), 0)
    gw = wp_ref.shape[-1]
    sums = ext
    shift = 1
    parts = []
    for g, w in enumerate(POOL_WINDOWS):
        while shift < w:
            sums = sums + pltpu.roll(sums, shift, 0)
            shift *= 2
        total = sums[POOL_HALO:, g * gw:(g + 1) * gw]
        count = jnp.minimum(pos + 1, w).astype(F32)
        pooled = total / count - u[:, g * gw:(g + 1) * gw]
        parts.append(_dot(pooled.astype(BF16), wp_ref[g]))
    return (jnp.concatenate(parts, axis=1) * scale).astype(BF16)


def _even_mix_ffn_kernel(x_ref, mod_ref, g_ref, wg_ref, wu_ref, wd_ref, fg_ref, u_ref, halo_ref, yd_ref,
                         wp_ref, ps_ref, wo_ref, o_ref, *, sub, final_norm):
    i = pl.program_id(1)
    mod = mod_ref[0, 0]
    tm = u_ref.shape[1]
    halo = halo_ref[0] * (i > 0).astype(F32)
    y_pool = _pool_mixer(u_ref[0], halo, i * tm, wp_ref, ps_ref[...])
    pw = y_pool.shape[1]
    y = _dot(y_pool, wo_ref[:pw, :]) + _dot(yd_ref[0], wo_ref[pw:, :])
    x = x_ref[0] + _gate(mod, 1) * y
    o_ref[0] = _swiglu_residual(x, mod, sub, g_ref[...], wg_ref, wu_ref, wd_ref, fg_ref[...], final_norm)


def _ffn(x, mod, layer, sub, which, g, w_gate, w_up, w_down, final_g, final_norm, mix=None):
    b, s, d = x.shape
    f = w_gate.shape[-1]
    tm = FFN_ROWS
    pick = lambda bi, i: (layer, which, 0, 0)
    row = lambda bi, i: (bi, i, 0)
    in_specs = [pl.BlockSpec((1, tm, d), row),
                pl.BlockSpec((1, 1, 3 * N_SUB, d), lambda bi, i: (layer, bi, 0, 0)),
                pl.BlockSpec((1, d), lambda bi, i: (0, 0)),
                _resident((1, 1, d, f), pick),
                _resident((1, 1, d, f), pick),
                _resident((1, 1, f, d), pick),
                pl.BlockSpec((1, d), lambda bi, i: (0, 0))]
    args = [x, mod, g.reshape(1, d), w_gate, w_up, w_down, final_g.reshape(1, d)]
    if mix is None:
        body, name = _ffn_kernel, "ffn"
    elif mix[0] == "odd":
        _, y, w_out = mix
        body, name = _odd_mix_ffn_kernel, "odd_mix_ffn"
        in_specs += [pl.BlockSpec((1, tm, y.shape[-1]), row),
                     _resident(w_out.shape, lambda bi, i: (0, 0))]
        args += [y, w_out.astype(BF16)]
    else:
        _, u, y_diff, w_pool, pool_scale, w_out = mix
        body, name = _even_mix_ffn_kernel, "even_mix_ffn"
        pw = u.shape[-1]
        halo_blocks = tm // POOL_HALO
        in_specs += [pl.BlockSpec((1, tm, pw), row),
                     pl.BlockSpec((1, POOL_HALO, pw),
                                  lambda bi, i: (bi, jnp.maximum(i * halo_blocks - 1, 0), 0)),
                     pl.BlockSpec((1, tm, y_diff.shape[-1]), row),
                     _resident(w_pool.shape, lambda bi, i: (0, 0, 0)),
                     pl.BlockSpec((1, pw), lambda bi, i: (0, 0)),
                     _resident(w_out.shape, lambda bi, i: (0, 0))]
        args += [u, u, y_diff, w_pool.astype(BF16), pool_scale.reshape(1, pw), w_out.astype(BF16)]
    return pl.pallas_call(
        functools.partial(body, sub=sub, final_norm=final_norm),
        grid=(b, s // tm),
        in_specs=in_specs,
        out_specs=pl.BlockSpec((1, tm, d), row),
        out_shape=jax.ShapeDtypeStruct((b, s, d), F32),
        compiler_params=_params("arbitrary", "arbitrary"),
        name=name,
    )(*args)


def _even_in_kernel(x_ref, mod_ref, g_ref, w_ref, wt_ref, u_ref, k_ref, qt_ref, vt_ref, *, sub):
    h = _modnorm(x_ref[0], g_ref[...], mod_ref[0, 0], sub).astype(BF16)
    p = _dot(h, w_ref[...])
    pw = u_ref.shape[-1]
    u_ref[0] = p[:, :pw]
    k_ref[0] = p[:, pw:].astype(BF16)
    pt = lax.dot_general(wt_ref[...], h, (((1,), (1,)), ((), ())), preferred_element_type=F32)
    qw = qt_ref.shape[1]
    qt_ref[0] = (pt[:qw] * (DIFF_HEAD_DIM ** -0.5 * LOG2_E)).astype(BF16)
    vt_ref[0, 0] = pt[qw:].astype(BF16)


def _even_in_proj(x, mod, layer, g, w_in, pool_width, qk_width):
    b, s, d = x.shape
    n = w_in.shape[1]
    v_width = n - pool_width - 2 * qk_width
    tm = ATTN_ROWS
    k0 = pool_width + qk_width
    w_uk = jnp.concatenate([w_in[:, :pool_width], w_in[:, k0:k0 + qk_width]], axis=1).astype(BF16)
    w_qv_t = jnp.concatenate([w_in[:, pool_width:k0], w_in[:, k0 + qk_width:]], axis=1).T.astype(BF16)
    row = lambda bi, i: (bi, i, 0)
    return pl.pallas_call(
        functools.partial(_even_in_kernel, sub=1),
        grid=(b, s // tm),
        in_specs=[pl.BlockSpec((1, tm, d), row),
                  pl.BlockSpec((1, 1, 3 * N_SUB, d), lambda bi, i: (layer, bi, 0, 0)),
                  pl.BlockSpec((1, d), lambda bi, i: (0, 0)),
                  _resident(w_uk.shape, lambda bi, i: (0, 0)),
                  _resident(w_qv_t.shape, lambda bi, i: (0, 0))],
        out_specs=[pl.BlockSpec((1, tm, pool_width), row),
                   pl.BlockSpec((1, tm, qk_width), row),
                   pl.BlockSpec((1, qk_width, tm), lambda bi, i: (bi, 0, i)),
                   pl.BlockSpec((1, 1, v_width, tm), lambda bi, i: (bi, i, 0, 0))],
        out_shape=[jax.ShapeDtypeStruct((b, s, pool_width), F32),
                   jax.ShapeDtypeStruct((b, s, qk_width), BF16),
                   jax.ShapeDtypeStruct((b, qk_width, s), BF16),
                   jax.ShapeDtypeStruct((b, s // tm, v_width, tm), BF16)],
        compiler_params=_params("arbitrary", "arbitrary"),
        name="even_in_proj",
    )(x, mod, g.reshape(1, d), w_uk, w_qv_t)


def _diff_attn_kernel(qt_ref, k_ref, vt_ref, lam_ref, sg_ref, o_ref, m_ref, l_ref, acc_ref,
                      s0_ref, s1_ref, cm0_ref, cm1_ref, *, lambda_init, t):
    i = pl.program_id(2)
    qt = qt_ref[0].astype(F32)
    feat = lax.broadcasted_iota(jnp.int32, qt.shape, 0)
    qs = jnp.concatenate([jnp.where(feat < DIFF_HEAD_DIM, qt, 0.0),
                          jnp.where(feat >= DIFF_HEAD_DIM, qt, 0.0)], axis=1).astype(BF16)
    m_ref[...] = jnp.full_like(m_ref, -jnp.inf)
    l_ref[...] = jnp.zeros_like(l_ref)
    acc_ref[...] = jnp.zeros_like(acc_ref)
    groups = [slice(c, c + ATTN_COL_GROUP) for c in range(0, 2 * t, ATTN_COL_GROUP)]
    s_refs = (s0_ref, s1_ref)
    cm_refs = (cm0_ref, cm1_ref)

    def produce(j, slot, cols):
        kb = k_ref[0, pl.ds(pl.multiple_of(j * t, t), t), :]
        s = _dot(kb, qs[:, cols])
        s_refs[slot][:, cols] = s
        cm_refs[slot][:, cols] = jnp.max(s, axis=0, keepdims=True)

    def consume(j, slot, cols, diagonal):
        if diagonal:
            first_q = cols.start % t
            n_keys = first_q + ATTN_COL_GROUP
            s = s_refs[slot][:n_keys, cols]
            key = lax.broadcasted_iota(jnp.int32, s.shape, 0)
            qry = lax.broadcasted_iota(jnp.int32, s.shape, 1) + first_q
            s = jnp.where(key <= qry, s, -jnp.inf)
            cm = jnp.max(s, axis=0, keepdims=True)
            vt = vt_ref[0, j][:, :n_keys]
        else:
            s = s_refs[slot][:, cols]
            cm = cm_refs[slot][:, cols]
            vt = vt_ref[0, j]
        m = m_ref[:, cols]
        m_new = jnp.maximum(m, cm)
        alpha = jnp.exp2(m - m_new)
        p = jnp.exp2(s - m_new)
        l_ref[:, cols] = alpha * l_ref[:, cols] + jnp.sum(p, axis=0, keepdims=True)
        acc_ref[:, cols] = alpha * acc_ref[:, cols] + _dot(vt, p.astype(BF16))
        m_ref[:, cols] = m_new

    def advance(j, slot):
        for cols in groups:
            produce(j + 1, 1 - slot, cols)
            consume(j, slot, cols, False)

    def diagonal(slot):
        for cols in groups:
            consume(i, slot, cols, True)

    for cols in groups:
        produce(0, 0, cols)

    def body(jj, carry):
        for d in range(ATTN_UNROLL):
            advance(ATTN_UNROLL * jj + d, d % 2)
        return carry

    lax.fori_loop(0, i // ATTN_UNROLL, body, 0)
    rest = i % ATTN_UNROLL

    @pl.when(rest >= 2)
    def _():
        advance(i - rest, 0)
        advance(i - rest + 1, 1)

    @pl.when(rest % 2 == 0)
    def _():
        diagonal(0)

    @pl.when(rest % 2 == 1)
    def _():
        advance(i - 1, 0)
        diagonal(1)

    o = acc_ref[...] / l_ref[...]
    lp = lam_ref[...]
    lam = (jnp.exp(jnp.sum(lp[0:1] * lp[1:2], axis=1, keepdims=True))
           - jnp.exp(jnp.sum(lp[2:3] * lp[3:4], axis=1, keepdims=True)) + lambda_init)
    o = o[:, :t] - lam * o[:, t:]
    o = o * lax.rsqrt(jnp.mean(o * o, axis=0, keepdims=True) + EPS) * sg_ref[...]
    o_ref[0] = (o * (1.0 - lambda_init)).T.astype(o_ref.dtype)


def _diff_attention(qt, k, vt, lam_params, subln_g, lambda_init):
    b, s, _ = k.shape
    hd = 2 * DIFF_HEAD_DIM
    t = ATTN_ROWS
    assert s % t == 0 and vt.shape[-1] == t and (2 * t) % ATTN_COL_GROUP == 0
    return pl.pallas_call(
        functools.partial(_diff_attn_kernel, lambda_init=lambda_init, t=t),
        grid=(b, DIFF_HEADS, s // t),
        in_specs=[pl.BlockSpec((1, hd, t), lambda bi, h, i: (bi, h, i)),
                  pl.BlockSpec((1, s, hd), lambda bi, h, i: (bi, 0, h)),
                  pl.BlockSpec((1, s // t, hd, t), lambda bi, h, i: (bi, 0, h, 0)),
                  pl.BlockSpec(lam_params.shape, lambda bi, h, i: (0, 0)),
                  pl.BlockSpec((hd, 1), lambda bi, h, i: (0, 0))],
        out_specs=pl.BlockSpec((1, t, hd), lambda bi, h, i: (bi, i, h)),
        out_shape=jax.ShapeDtypeStruct((b, s, DIFF_HEADS * hd), BF16),
        scratch_shapes=[pltpu.VMEM((1, 2 * t), F32), pltpu.VMEM((1, 2 * t), F32),
                        pltpu.VMEM((hd, 2 * t), F32),
                        pltpu.VMEM((t, 2 * t), F32), pltpu.VMEM((t, 2 * t), F32),
                        pltpu.VMEM((1, 2 * t), F32), pltpu.VMEM((1, 2 * t), F32)],
        compiler_params=_params("arbitrary", "arbitrary", "arbitrary"),
        name="diff_attention",
    )(qt, k, vt, lam_params, subln_g.reshape(hd, 1))


def _odd_in_kernel(x_ref, mod_ref, g_ref, w_ref, wgate_ref, q_ref, k_ref, v_ref, o_ref, gates_ref, *, sub):
    h = _modnorm(x_ref[0], g_ref[...], mod_ref[0, 0], sub).astype(BF16)
    p = _dot(h, w_ref[...])
    nq = q_ref.shape[-1]
    nv = v_ref.shape[-1]
    q_ref[0] = p[:, :nq].astype(BF16)
    k_ref[0] = (p[:, nq:2 * nq] * (MLSTM_QK_DIM ** -0.5)).astype(BF16)
    v_ref[0] = p[:, 2 * nq:2 * nq + nv].astype(BF16)
    o_ref[0] = p[:, 2 * nq + nv:].astype(BF16)
    gates_ref[0] = _dot(h, wgate_ref[...]).T[:gates_ref.shape[1]]


def _odd_in_proj(x, mod, layer, g, w_in, nq, nv):
    b, s, d = x.shape
    n_main = 2 * nq + 2 * nv
    n_gates = w_in.shape[1] - n_main
    w_main = w_in[:, :n_main].astype(BF16)
    w_gate = jnp.pad(w_in[:, n_main:], ((0, 0), (0, LANES - n_gates))).astype(BF16)
    tm = PROJ_ROWS
    row = lambda bi, i: (bi, i, 0)
    return pl.pallas_call(
        functools.partial(_odd_in_kernel, sub=1),
        grid=(b, s // tm),
        in_specs=[pl.BlockSpec((1, tm, d), row),
                  pl.BlockSpec((1, 1, 3 * N_SUB, d), lambda bi, i: (layer, bi, 0, 0)),
                  pl.BlockSpec((1, d), lambda bi, i: (0, 0)),
                  _resident((d, n_main), lambda bi, i: (0, 0)),
                  _resident((d, LANES), lambda bi, i: (0, 0))],
        out_specs=[pl.BlockSpec((1, tm, nq), row),
                   pl.BlockSpec((1, tm, nq), row),
                   pl.BlockSpec((1, tm, nv), row),
                   pl.BlockSpec((1, tm, nv), row),
                   pl.BlockSpec((1, n_gates, tm), lambda bi, i: (bi, 0, i))],
        out_shape=[jax.ShapeDtypeStruct((b, s, nq), BF16),
                   jax.ShapeDtypeStruct((b, s, nq), BF16),
                   jax.ShapeDtypeStruct((b, s, nv), BF16),
                   jax.ShapeDtypeStruct((b, s, nv), BF16),
                   jax.ShapeDtypeStruct((b, n_gates, s), F32)],
        compiler_params=_params("arbitrary", "arbitrary"),
        name="odd_in_proj",
    )(x, mod, g.reshape(1, d), w_main, w_gate)


def _log_sigmoid(x):
    return jnp.minimum(x, 0.0) - jnp.log1p(jnp.exp(-jnp.abs(x)))


def _lane_scan(x, combine, fill):
    lane = lax.broadcasted_iota(jnp.int32, x.shape, 1)
    shift = 1
    while shift < x.shape[-1]:
        x = combine(x, jnp.where(lane >= shift, pltpu.roll(x, shift, 1), fill))
        shift *= 2
    return x


def _mlstm_kernel(bg_ref, q_ref, k_ref, v_ref, og_ref, ig_ref, fg_ref, ig_next_ref, fg_next_ref,
                  ng_ref, y_ref, c_ref, m_ref, cols_ref, src_ref, w_ref, sold_ref, sloc_ref, *, chunks):
    head = pl.program_id(1)
    step = pl.program_id(2)
    L = MLSTM_CHUNK
    G = chunks

    def store_gate_stats(i_ref, f_ref):
        i_all = i_ref[0, 0] + bg_ref[head]
        logf = _log_sigmoid(f_ref[0, 0] + bg_ref[MLSTM_HEADS + head])
        b_all = _lane_scan(logf, jnp.add, 0.0)
        b_last = b_all[:, L - 1:L]
        a_all = b_last - b_all + i_all
        m_loc = jnp.max(a_all, axis=1, keepdims=True)
        w_ref[...] = jnp.exp(a_all - m_loc)

        m_in = []
        m = m_ref[...]
        for g in range(G):
            m_in.append(m)
            m = jnp.maximum(b_last[g:g + 1] + m, m_loc[g:g + 1])
        m_out = m_in[1:] + [m]
        m_ref[...] = m
        m_prev = jnp.concatenate(m_in, axis=0)
        m_next = jnp.concatenate(m_out, axis=0)
        sold_ref[...] = jnp.exp(b_last + m_prev - m_next)
        sloc_ref[...] = jnp.exp(m_loc - m_next)

        src = i_all - b_all
        src_ref[...] = src
        inter_log = b_all + m_prev
        m_t = jnp.maximum(inter_log, b_all + _lane_scan(src, jnp.maximum, -jnp.inf))
        per_row = jnp.concatenate([b_all - m_t, jnp.exp(inter_log - m_t), jnp.exp(-m_t)], axis=0)
        per_row = jnp.concatenate([per_row, jnp.zeros((L - 3 * G, L), F32)], axis=0)
        cols_ref[...] = per_row.T

    @pl.when(step == 0)
    def _():
        c_ref[...] = jnp.zeros_like(c_ref)
        m_ref[...] = jnp.zeros_like(m_ref)
        store_gate_stats(ig_ref, fg_ref)

    cols = cols_ref[...]
    src = src_ref[...]
    w_all = w_ref[...]
    s_old = sold_ref[...]
    s_loc = sloc_ref[...]
    store_gate_stats(ig_next_ref, fg_next_ref)

    causal = (lax.broadcasted_iota(jnp.int32, (L, L), 1) <= lax.broadcasted_iota(jnp.int32, (L, L), 0))
    dv = v_ref.shape[-1]
    ones_col = (lax.broadcasted_iota(jnp.int32, (L, LANES), 1) == 0).astype(BF16)
    state = c_ref[...]
    for g in range(G):
        rows = slice(g * L, (g + 1) * L)
        q = q_ref[0, rows, :]
        k = k_ref[0, rows, :]
        v_ext = jnp.concatenate([v_ref[0, rows, :], ones_col], axis=1)
        dest = cols[:, g:g + 1]
        inter_w = cols[:, G + g:G + g + 1]
        floor = cols[:, 2 * G + g:2 * G + g + 1]

        dw = jnp.exp(jnp.where(causal, dest + src[g:g + 1], -jnp.inf))
        s = lax.dot_general(q, k, (((1,), (1,)), ((), ())), preferred_element_type=F32) * dw
        nd = _dot(s.astype(BF16), v_ext) + inter_w * _dot(q, state.astype(BF16))
        num = nd[:, :dv]
        den = nd[:, dv:dv + 1]
        r = 1.0 / jnp.maximum(jnp.abs(den), floor)
        scale = r * lax.rsqrt(r * r * jnp.mean(num * num, axis=-1, keepdims=True) + EPS)
        hn = num * scale * ng_ref[...]
        y_ref[0, rows, :] = (jax.nn.sigmoid(og_ref[0, rows, :].astype(F32)) * hn).astype(y_ref.dtype)

        kw_t = (k.astype(F32).T * w_all[g:g + 1]).astype(BF16)
        state = s_old[g:g + 1] * state + s_loc[g:g + 1] * _dot(kw_t, v_ext)
    c_ref[...] = state


def _mlstm(q, k, v, o, gates, b_gates, norm_g):
    b, s, _ = q.shape
    dk, dv, L = MLSTM_QK_DIM, MLSTM_V_DIM, MLSTM_CHUNK
    nc = s // L
    g = MLSTM_CHUNKS_PER_STEP
    rows = g * L
    gates = gates.reshape(b, 2 * MLSTM_HEADS, nc, L)
    last = nc // g - 1

    def gate_spec(first, ahead):
        return pl.BlockSpec((1, 1, g, L),
                            lambda bi, h, c: (bi, first + h, jnp.minimum(c + ahead, last), 0))

    return pl.pallas_call(
        functools.partial(_mlstm_kernel, chunks=g),
        grid=(b, MLSTM_HEADS, nc // g),
        in_specs=[pl.BlockSpec(memory_space=pltpu.SMEM),
                  pl.BlockSpec((1, rows, dk), lambda bi, h, c: (bi, c, h)),
                  pl.BlockSpec((1, rows, dk), lambda bi, h, c: (bi, c, h)),
                  pl.BlockSpec((1, rows, dv), lambda bi, h, c: (bi, c, h)),
                  pl.BlockSpec((1, rows, dv), lambda bi, h, c: (bi, c, h)),
                  gate_spec(0, 0), gate_spec(MLSTM_HEADS, 0),
                  gate_spec(0, 1), gate_spec(MLSTM_HEADS, 1),
                  pl.BlockSpec((1, dv), lambda bi, h, c: (0, h))],
        out_specs=pl.BlockSpec((1, rows, dv), lambda bi, h, c: (bi, c, h)),
        out_shape=jax.ShapeDtypeStruct((b, s, MLSTM_HEADS * dv), BF16),
        scratch_shapes=[pltpu.VMEM((dk, dv + LANES), F32), pltpu.VMEM((1, 1), F32),
                        pltpu.VMEM((L, L), F32), pltpu.VMEM((g, L), F32), pltpu.VMEM((g, L), F32),
                        pltpu.VMEM((g, 1), F32), pltpu.VMEM((g, 1), F32)],
        compiler_params=_params("arbitrary", "arbitrary", "arbitrary"),
        name="mlstm",
    )(b_gates.reshape(-1), q, k, v, o, gates, gates, gates, gates, norm_g.reshape(1, -1))


def kernel(x, c, w_mod, b_mod, norm_g, w_ffn_gate, w_ffn_up, w_ffn_down, w_in_even, w_pool, pool_scale, diff_lambda, diff_subln_g, w_out_even, w_in_odd, b_gates_odd, mlstm_norm_g, w_out_odd, final_g):
    depth = w_mod.shape[0]
    mod = _modulation(c, w_mod, b_mod)
    pool_width = pool_scale.shape[-1]
    qk_width = DIFF_HEADS * 2 * DIFF_HEAD_DIM
    ffn_w = (w_ffn_gate.astype(BF16), w_ffn_up.astype(BF16), w_ffn_down.astype(BF16))
    for l in range(depth):
        x = _ffn(x, mod, l, 0, 0, norm_g[l, 0], *ffn_w, final_g, False)
        if l % 2 == 0:
            e = l // 2
            lambda_init = 0.8 - 0.6 * math.exp(-0.3 * l)
            u, k, qt, vt = _even_in_proj(x, mod, l, norm_g[l, 1], w_in_even[e], pool_width, qk_width)
            y_diff = _diff_attention(qt, k, vt, diff_lambda[e], diff_subln_g[e], lambda_init)
            mix = ("even", u, y_diff, w_pool[e], pool_scale[e], w_out_even[e])
        else:
            o = l // 2
            q, k, v, og, gates = _odd_in_proj(x, mod, l, norm_g[l, 1], w_in_odd[o],
                                              MLSTM_HEADS * MLSTM_QK_DIM, MLSTM_HEADS * MLSTM_V_DIM)
            y = _mlstm(q, k, v, og, gates, b_gates_odd[o], mlstm_norm_g[o])
            mix = ("odd", y, w_out_odd[o])
        x = _ffn(x, mod, l, 2, 1, norm_g[l, 2], *ffn_w, final_g, l == depth - 1, mix)
    return x
```

```python
import functools
import math

import jax
import jax.numpy as jnp
from jax import lax
from jax.experimental import pallas as pl
from jax.experimental.pallas import tpu as pltpu

F32 = jnp.float32
BF16 = jnp.bfloat16

EPS = 1e-6
N_SUB = 3
POOL_WINDOWS = (2, 4, 8, 16)
POOL_HALO = 16
DIFF_HEADS = 4
DIFF_HEAD_DIM = 64
MLSTM_HEADS = 4
MLSTM_QK_DIM = 128
MLSTM_V_DIM = 256
MLSTM_CHUNK = 128
LANES = 128

VMEM_LIMIT_BYTES = 56 * 1024 * 1024

MOD_ROWS = 256
FFN_ROWS = 512
FFN_SUB_ROWS = 256
PROJ_ROWS = 512
ATTN_ROWS = 512
ATTN_COL_GROUP = 256
ATTN_UNROLL = 4
LOG2_E = math.log2(math.e)
MLSTM_CHUNKS_PER_STEP = 32


def _params(*semantics):
    return pltpu.CompilerParams(dimension_semantics=semantics,
                                vmem_limit_bytes=VMEM_LIMIT_BYTES)


def _resident(block_shape, index_map):
    return pl.BlockSpec(block_shape, index_map, pipeline_mode=pl.Buffered(1))


def _dot(a, b):
    return jnp.dot(a, b, preferred_element_type=F32)


def _modnorm(x, g, mod, sub):
    shift = mod[3 * sub:3 * sub + 1]
    scale = mod[3 * sub + 1:3 * sub + 2]
    y = x * lax.rsqrt(jnp.mean(x * x, axis=-1, keepdims=True) + EPS)
    return (y * g) * (1.0 + scale) + shift


def _gate(mod, sub):
    return mod[3 * sub + 2:3 * sub + 3]


def _mod_kernel(c_ref, w_ref, b_ref, o_ref):
    @pl.when(pl.program_id(1) == 0)
    def _():
        o_ref[0] = jnp.broadcast_to(b_ref[0], o_ref.shape[1:])

    c = c_ref[...]
    c_act = c * jax.nn.sigmoid(c)
    w = w_ref[0]
    c_hi = c_act.astype(BF16)
    c_lo = (c_act - c_hi.astype(F32)).astype(BF16)
    w_hi = w.astype(BF16)
    w_lo = (w - w_hi.astype(F32)).astype(BF16)
    o_ref[0] += _dot(c_hi, w_hi) + (_dot(c_hi, w_lo) + _dot(c_lo, w_hi))


def _modulation(c, w_mod, b_mod):
    depth, d, n = w_mod.shape
    b = c.shape[0]
    tk = MOD_ROWS
    out = pl.pallas_call(
        _mod_kernel,
        grid=(depth, d // tk),
        in_specs=[pl.BlockSpec((b, tk), lambda l, j: (0, j)),
                  pl.BlockSpec((1, tk, n), lambda l, j: (l, j, 0)),
                  pl.BlockSpec((1, 1, n), lambda l, j: (l, 0, 0))],
        out_specs=pl.BlockSpec((1, b, n), lambda l, j: (l, 0, 0)),
        out_shape=jax.ShapeDtypeStruct((depth, b, n), F32),
        compiler_params=_params("arbitrary", "arbitrary"),
        name="modulation",
    )(c, w_mod, b_mod.reshape(depth, 1, n))
    return out.reshape(depth, b, 3 * N_SUB, d)


def _swiglu_residual(x, mod, sub, g, wg_ref, wu_ref, wd_ref, fg, final_norm):
    outs = []
    for r in range(0, x.shape[0], FFN_SUB_ROWS):
        xs = x[r:r + FFN_SUB_ROWS]
        h = _modnorm(xs, g, mod, sub).astype(BF16)
        gate_act = _dot(h, wg_ref[0, 0])
        up = _dot(h, wu_ref[0, 0])
        a = (gate_act * jax.nn.sigmoid(gate_act) * up).astype(BF16)
        out = xs + (0.5 * _gate(mod, sub)) * _dot(a, wd_ref[0, 0])
        if final_norm:
            out = out * lax.rsqrt(jnp.mean(out * out, axis=-1, keepdims=True) + EPS) * fg
        outs.append(out)
    return jnp.concatenate(outs, axis=0)


def _ffn_kernel(x_ref, mod_ref, g_ref, wg_ref, wu_ref, wd_ref, fg_ref, o_ref, *, sub, final_norm):
    o_ref[0] = _swiglu_residual(x_ref[0], mod_ref[0, 0], sub, g_ref[...], wg_ref, wu_ref, wd_ref,
                                fg_ref[...], final_norm)


def _odd_mix_ffn_kernel(x_ref, mod_ref, g_ref, wg_ref, wu_ref, wd_ref, fg_ref, y_ref, wo_ref, o_ref, *,
                        sub, final_norm):
    mod = mod_ref[0, 0]
    x = x_ref[0] + _gate(mod, 1) * _dot(y_ref[0], wo_ref[...])
    o_ref[0] = _swiglu_residual(x, mod, sub, g_ref[...], wg_ref, wu_ref, wd_ref, fg_ref[...], final_norm)


def _pool_mixer(u, halo, first_row, wp_ref, scale):
    tm = u.shape[0]
    ext = jnp.concatenate([halo, u], axis=0)
    pos = first_row + lax.broadcasted_iota(jnp.int32, (tm, 1), 0)
    gw = wp_ref.shape[-1]
    sums = ext
    shift = 1
    parts = []
    for g, w in enumerate(POOL_WINDOWS):
        while shift < w:
            sums = sums + pltpu.roll(sums, shift, 0)
            shift *= 2
        total = sums[POOL_HALO:, g * gw:(g + 1) * gw]
        count = jnp.minimum(pos + 1, w).astype(F32)
        pooled = total / count - u[:, g * gw:(g + 1) * gw]
        parts.append(_dot(pooled.astype(BF16), wp_ref[g]))
    return (jnp.concatenate(parts, axis=1) * scale).astype(BF16)


def _even_mix_ffn_kernel(x_ref, mod_ref, g_ref, wg_ref, wu_ref, wd_ref, fg_ref, u_ref, halo_ref, yd_ref,
                         wp_ref, ps_ref, wo_ref, o_ref, *, sub, final_norm):
    i = pl.program_id(1)
    mod = mod_ref[0, 0]
    tm = u_ref.shape[1]
    halo = halo_ref[0] * (i > 0).astype(F32)
    y_pool = _pool_mixer(u_ref[0], halo, i * tm, wp_ref, ps_ref[...])
    pw = y_pool.shape[1]
    y = _dot(y_pool, wo_ref[:pw, :]) + _dot(yd_ref[0], wo_ref[pw:, :])
    x = x_ref[0] + _gate(mod, 1) * y
    o_ref[0] = _swiglu_residual(x, mod, sub, g_ref[...], wg_ref, wu_ref, wd_ref, fg_ref[...], final_norm)


def _ffn(x, mod, layer, sub, which, g, w_gate, w_up, w_down, final_g, final_norm, mix=None):
    b, s, d = x.shape
    f = w_gate.shape[-1]
    tm = FFN_ROWS
    pick = lambda bi, i: (layer, which, 0, 0)
    row = lambda bi, i: (bi, i, 0)
    in_specs = [pl.BlockSpec((1, tm, d), row),
                pl.BlockSpec((1, 1, 3 * N_SUB, d), lambda bi, i: (layer, bi, 0, 0)),
                pl.BlockSpec((1, d), lambda bi, i: (0, 0)),
                _resident((1, 1, d, f), pick),
                _resident((1, 1, d, f), pick),
                _resident((1, 1, f, d), pick),
                pl.BlockSpec((1, d), lambda bi, i: (0, 0))]
    args = [x, mod, g.reshape(1, d), w_gate, w_up, w_down, final_g.reshape(1, d)]
    if mix is None:
        body, name = _ffn_kernel, "ffn"
    elif mix[0] == "odd":
        _, y, w_out = mix
        body, name = _odd_mix_ffn_kernel, "odd_mix_ffn"
        in_specs += [pl.BlockSpec((1, tm, y.shape[-1]), row),
                     _resident(w_out.shape, lambda bi, i: (0, 0))]
        args += [y, w_out.astype(BF16)]
    else:
        _, u, y_diff, w_pool, pool_scale, w_out = mix
        body, name = _even_mix_ffn_kernel, "even_mix_ffn"
        pw = u.shape[-1]
        halo_blocks = tm // POOL_HALO
        in_specs += [pl.BlockSpec((1, tm, pw), row),
                     pl.BlockSpec((1, POOL_HALO, pw),
                                  lambda bi, i: (bi, jnp.maximum(i * halo_blocks - 1, 0), 0)),
                     pl.BlockSpec((1, tm, y_diff.shape[-1]), row),
                     _resident(w_pool.shape, lambda bi, i: (0, 0, 0)),
                     pl.BlockSpec((1, pw), lambda bi, i: (0, 0)),
                     _resident(w_out.shape, lambda bi, i: (0, 0))]
        args += [u, u, y_diff, w_pool.astype(BF16), pool_scale.reshape(1, pw), w_out.astype(BF16)]
    return pl.pallas_call(
        functools.partial(body, sub=sub, final_norm=final_norm),
        grid=(b, s // tm),
        in_specs=in_specs,
        out_specs=pl.BlockSpec((1, tm, d), row),
        out_shape=jax.ShapeDtypeStruct((b, s, d), F32),
        compiler_params=_params("arbitrary", "arbitrary"),
        name=name,
    )(*args)


def _even_in_kernel(x_ref, mod_ref, g_ref, w_ref, wt_ref, u_ref, k_ref, qt_ref, vt_ref, *, sub):
    h = _modnorm(x_ref[0], g_ref[...], mod_ref[0, 0], sub).astype(BF16)
    p = _dot(h, w_ref[...])
    pw = u_ref.shape[-1]
    u_ref[0] = p[:, :pw]
    k_ref[0] = p[:, pw:].astype(BF16)
    pt = lax.dot_general(wt_ref[...], h, (((1,), (1,)), ((), ())), preferred_element_type=F32)
    qw = qt_ref.shape[1]
    qt_ref[0] = (pt[:qw] * (DIFF_HEAD_DIM ** -0.5 * LOG2_E)).astype(BF16)
    vt_ref[0, 0] = pt[qw:].astype(BF16)


def _even_in_proj(x, mod, layer, g, w_in, pool_width, qk_width):
    b, s, d = x.shape
    n = w_in.shape[1]
    v_width = n - pool_width - 2 * qk_width
    tm = ATTN_ROWS
    k0 = pool_width + qk_width
    w_uk = jnp.concatenate([w_in[:, :pool_width], w_in[:, k0:k0 + qk_width]], axis=1).astype(BF16)
    w_qv_t = jnp.concatenate([w_in[:, pool_width:k0], w_in[:, k0 + qk_width:]], axis=1).T.astype(BF16)
    row = lambda bi, i: (bi, i, 0)
    return pl.pallas_call(
        functools.partial(_even_in_kernel, sub=1),
        grid=(b, s // tm),
        in_specs=[pl.BlockSpec((1, tm, d), row),
                  pl.BlockSpec((1, 1, 3 * N_SUB, d), lambda bi, i: (layer, bi, 0, 0)),
                  pl.BlockSpec((1, d), lambda bi, i: (0, 0)),
                  _resident(w_uk.shape, lambda bi, i: (0, 0)),
                  _resident(w_qv_t.shape, lambda bi, i: (0, 0))],
        out_specs=[pl.BlockSpec((1, tm, pool_width), row),
                   pl.BlockSpec((1, tm, qk_width), row),
                   pl.BlockSpec((1, qk_width, tm), lambda bi, i: (bi, 0, i)),
                   pl.BlockSpec((1, 1, v_width, tm), lambda bi, i: (bi, i, 0, 0))],
        out_shape=[jax.ShapeDtypeStruct((b, s, pool_width), F32),
                   jax.ShapeDtypeStruct((b, s, qk_width), BF16),
                   jax.ShapeDtypeStruct((b, qk_width, s), BF16),
                   jax.ShapeDtypeStruct((b, s // tm, v_width, tm), BF16)],
        compiler_params=_params("arbitrary", "arbitrary"),
        name="even_in_proj",
    )(x, mod, g.reshape(1, d), w_uk, w_qv_t)


def _diff_attn_kernel(qt_ref, k_ref, vt_ref, lam_ref, sg_ref, o_ref, m_ref, l_ref, acc_ref,
                      s0_ref, s1_ref, cm0_ref, cm1_ref, *, lambda_init, t):
    i = pl.program_id(2)
    qt = qt_ref[0].astype(F32)
    feat = lax.broadcasted_iota(jnp.int32, qt.shape, 0)
    qs = jnp.concatenate([jnp.where(feat < DIFF_HEAD_DIM, qt, 0.0),
                          jnp.where(feat >= DIFF_HEAD_DIM, qt, 0.0)], axis=1).astype(BF16)
    m_ref[...] = jnp.full_like(m_ref, -jnp.inf)
    l_ref[...] = jnp.zeros_like(l_ref)
    acc_ref[...] = jnp.zeros_like(acc_ref)
    groups = [slice(c, c + ATTN_COL_GROUP) for c in range(0, 2 * t, ATTN_COL_GROUP)]
    s_refs = (s0_ref, s1_ref)
    cm_refs = (cm0_ref, cm1_ref)

    def produce(j, slot, cols):
        kb = k_ref[0, pl.ds(pl.multiple_of(j * t, t), t), :]
        s = _dot(kb, qs[:, cols])
        s_refs[slot][:, cols] = s
        cm_refs[slot][:, cols] = jnp.max(s, axis=0, keepdims=True)

    def consume(j, slot, cols, diagonal):
        if diagonal:
            first_q = cols.start % t
            n_keys = first_q + ATTN_COL_GROUP
            s = s_refs[slot][:n_keys, cols]
            key = lax.broadcasted_iota(jnp.int32, s.shape, 0)
            qry = lax.broadcasted_iota(jnp.int32, s.shape, 1) + first_q
            s = jnp.where(key <= qry, s, -jnp.inf)
            cm = jnp.max(s, axis=0, keepdims=True)
            vt = vt_ref[0, j][:, :n_keys]
        else:
            s = s_refs[slot][:, cols]
            cm = cm_refs[slot][:, cols]
            vt = vt_ref[0, j]
        m = m_ref[:, cols]
        m_new = jnp.maximum(m, cm)
        alpha = jnp.exp2(m - m_new)
        p = jnp.exp2(s - m_new)
        l_ref[:, cols] = alpha * l_ref[:, cols] + jnp.sum(p, axis=0, keepdims=True)
        acc_ref[:, cols] = alpha * acc_ref[:, cols] + _dot(vt, p.astype(BF16))
        m_ref[:, cols] = m_new

    def advance(j, slot):
        for cols in groups:
            produce(j + 1, 1 - slot, cols)
            consume(j, slot, cols, False)

    def diagonal(slot):
        for cols in groups:
            consume(i, slot, cols, True)

    for cols in groups:
        produce(0, 0, cols)

    def body(jj, carry):
        for d in range(ATTN_UNROLL):
            advance(ATTN_UNROLL * jj + d, d % 2)
        return carry

    lax.fori_loop(0, i // ATTN_UNROLL, body, 0)
    rest = i % ATTN_UNROLL

    @pl.when(rest >= 2)
    def _():
        advance(i - rest, 0)
        advance(i - rest + 1, 1)

    @pl.when(rest % 2 == 0)
    def _():
        diagonal(0)

    @pl.when(rest % 2 == 1)
    def _():
        advance(i - 1, 0)
        diagonal(1)

    o = acc_ref[...] / l_ref[...]
    lp = lam_ref[...]
    lam = (jnp.exp(jnp.sum(lp[0:1] * lp[1:2], axis=1, keepdims=True))
           - jnp.exp(jnp.sum(lp[2:3] * lp[3:4], axis=1, keepdims=True)) + lambda_init)
    o = o[:, :t] - lam * o[:, t:]
    o = o * lax.rsqrt(jnp.mean(o * o, axis=0, keepdims=True) + EPS) * sg_ref[...]
    o_ref[0] = (o * (1.0 - lambda_init)).T.astype(o_ref.dtype)


def _diff_attention(qt, k, vt, lam_params, subln_g, lambda_init):
    b, s, _ = k.shape
    hd = 2 * DIFF_HEAD_DIM
    t = ATTN_ROWS
    assert s % t == 0 and vt.shape[-1] == t and (2 * t) % ATTN_COL_GROUP == 0
    return pl.pallas_call(
        functools.partial(_diff_attn_kernel, lambda_init=lambda_init, t=t),
        grid=(b, DIFF_HEADS, s // t),
        in_specs=[pl.BlockSpec((1, hd, t), lambda bi, h, i: (bi, h, i)),
                  pl.BlockSpec((1, s, hd), lambda bi, h, i: (bi, 0, h)),
                  pl.BlockSpec((1, s // t, hd, t), lambda bi, h, i: (bi, 0, h, 0)),
                  pl.BlockSpec(lam_params.shape, lambda bi, h, i: (0, 0)),
                  pl.BlockSpec((hd, 1), lambda bi, h, i: (0, 0))],
        out_specs=pl.BlockSpec((1, t, hd), lambda bi, h, i: (bi, i, h)),
        out_shape=jax.ShapeDtypeStruct((b, s, DIFF_HEADS * hd), BF16),
        scratch_shapes=[pltpu.VMEM((1, 2 * t), F32), pltpu.VMEM((1, 2 * t), F32),
                        pltpu.VMEM((hd, 2 * t), F32),
                        pltpu.VMEM((t, 2 * t), F32), pltpu.VMEM((t, 2 * t), F32),
                        pltpu.VMEM((1, 2 * t), F32), pltpu.VMEM((1, 2 * t), F32)],
        compiler_params=_params("arbitrary", "arbitrary", "arbitrary"),
        name="diff_attention",
    )(qt, k, vt, lam_params, subln_g.reshape(hd, 1))


def _odd_in_kernel(x_ref, mod_ref, g_ref, w_ref, wgate_ref, q_ref, k_ref, v_ref, o_ref, gates_ref, *, sub):
    h = _modnorm(x_ref[0], g_ref[...], mod_ref[0, 0], sub).astype(BF16)
    p = _dot(h, w_ref[...])
    nq = q_ref.shape[-1]
    nv = v_ref.shape[-1]
    q_ref[0] = p[:, :nq].astype(BF16)
    k_ref[0] = (p[:, nq:2 * nq] * (MLSTM_QK_DIM ** -0.5)).astype(BF16)
    v_ref[0] = p[:, 2 * nq:2 * nq + nv].astype(BF16)
    o_ref[0] = jax.nn.sigmoid(p[:, 2 * nq + nv:]).astype(BF16)
    gates_ref[0] = _dot(h, wgate_ref[...]).T[:gates_ref.shape[1]]


def _odd_in_proj(x, mod, layer, g, w_in, nq, nv):
    b, s, d = x.shape
    n_main = 2 * nq + 2 * nv
    n_gates = w_in.shape[1] - n_main
    w_main = w_in[:, :n_main].astype(BF16)
    w_gate = jnp.pad(w_in[:, n_main:], ((0, 0), (0, LANES - n_gates))).astype(BF16)
    tm = PROJ_ROWS
    row = lambda bi, i: (bi, i, 0)
    return pl.pallas_call(
        functools.partial(_odd_in_kernel, sub=1),
        grid=(b, s // tm),
        in_specs=[pl.BlockSpec((1, tm, d), row),
                  pl.BlockSpec((1, 1, 3 * N_SUB, d), lambda bi, i: (layer, bi, 0, 0)),
                  pl.BlockSpec((1, d), lambda bi, i: (0, 0)),
                  _resident((d, n_main), lambda bi, i: (0, 0)),
                  _resident((d, LANES), lambda bi, i: (0, 0))],
        out_specs=[pl.BlockSpec((1, tm, nq), row),
                   pl.BlockSpec((1, tm, nq), row),
                   pl.BlockSpec((1, tm, nv), row),
                   pl.BlockSpec((1, tm, nv), row),
                   pl.BlockSpec((1, n_gates, tm), lambda bi, i: (bi, 0, i))],
        out_shape=[jax.ShapeDtypeStruct((b, s, nq), BF16),
                   jax.ShapeDtypeStruct((b, s, nq), BF16),
                   jax.ShapeDtypeStruct((b, s, nv), BF16),
                   jax.ShapeDtypeStruct((b, s, nv), BF16),
                   jax.ShapeDtypeStruct((b, n_gates, s), F32)],
        compiler_params=_params("arbitrary", "arbitrary"),
        name="odd_in_proj",
    )(x, mod, g.reshape(1, d), w_main, w_gate)


def _log_sigmoid(x):
    return jnp.minimum(x, 0.0) - jnp.log1p(jnp.exp(-jnp.abs(x)))


def _lane_scan(x, combine, fill):
    lane = lax.broadcasted_iota(jnp.int32, x.shape, 1)
    shift = 1
    while shift < x.shape[-1]:
        x = combine(x, jnp.where(lane >= shift, pltpu.roll(x, shift, 1), fill))
        shift *= 2
    return x


def _mlstm_kernel(bg_ref, q_ref, k_ref, v_ref, og_ref, ig_ref, fg_ref, ig_next_ref, fg_next_ref,
                  ng_ref, y_ref, c_ref, m_ref, cols_ref, src_ref, w_ref, sold_ref, *, chunks):
    head = pl.program_id(1)
    step = pl.program_id(2)
    L = MLSTM_CHUNK
    G = chunks

    def store_gate_stats(i_ref, f_ref):
        i_all = i_ref[0, 0] + bg_ref[head]
        logf = _log_sigmoid(f_ref[0, 0] + bg_ref[MLSTM_HEADS + head])
        b_all = _lane_scan(logf, jnp.add, 0.0)
        b_last = b_all[:, L - 1:L]
        a_all = b_last - b_all + i_all
        m_loc = jnp.max(a_all, axis=1, keepdims=True)

        m_in = []
        m = m_ref[...]
        for g in range(G):
            m_in.append(m)
            m = jnp.maximum(b_last[g:g + 1] + m, m_loc[g:g + 1])
        m_out = m_in[1:] + [m]
        m_ref[...] = m
        m_prev = jnp.concatenate(m_in, axis=0)
        m_next = jnp.concatenate(m_out, axis=0)
        sold_ref[...] = jnp.exp(b_last + m_prev - m_next)
        w_ref[...] = jnp.exp(a_all - m_next)

        src = i_all - b_all
        src_ref[...] = src
        inter_log = b_all + m_prev
        m_t = jnp.maximum(inter_log, b_all + _lane_scan(src, jnp.maximum, -jnp.inf))
        per_row = jnp.concatenate([b_all - m_t, jnp.exp(inter_log - m_t), jnp.exp(-m_t)], axis=0)
        per_row = jnp.concatenate([per_row, jnp.zeros((L - 3 * G, L), F32)], axis=0)
        cols_ref[...] = per_row.T

    @pl.when(step == 0)
    def _():
        c_ref[...] = jnp.zeros_like(c_ref)
        m_ref[...] = jnp.zeros_like(m_ref)
        store_gate_stats(ig_ref, fg_ref)

    cols = cols_ref[...]
    src = src_ref[...]
    w_all = w_ref[...]
    s_old = sold_ref[...]
    store_gate_stats(ig_next_ref, fg_next_ref)

    causal = (lax.broadcasted_iota(jnp.int32, (L, L), 1) <= lax.broadcasted_iota(jnp.int32, (L, L), 0))
    dv = v_ref.shape[-1]
    ones_col = (lax.broadcasted_iota(jnp.int32, (L, LANES), 1) == 0).astype(BF16)
    state = c_ref[...]
    for g in range(G):
        rows = slice(g * L, (g + 1) * L)
        q = q_ref[0, rows, :]
        k = k_ref[0, rows, :]
        v_ext = jnp.concatenate([v_ref[0, rows, :], ones_col], axis=1)
        dest = cols[:, g:g + 1]
        inter_w = cols[:, G + g:G + g + 1]
        floor = cols[:, 2 * G + g:2 * G + g + 1]

        dw = jnp.exp(jnp.where(causal, dest + src[g:g + 1], -jnp.inf))
        s = lax.dot_general(q, k, (((1,), (1,)), ((), ())), preferred_element_type=F32) * dw
        q_w = (inter_w * q.astype(F32)).astype(BF16)
        nd = _dot(s.astype(BF16), v_ext) + _dot(q_w, state.astype(BF16))
        num = nd[:, :dv]
        den = nd[:, dv:dv + 1]
        r = 1.0 / jnp.maximum(jnp.abs(den), floor)
        scale = r * lax.rsqrt(r * r * jnp.mean(num * num, axis=-1, keepdims=True) + EPS)
        hn = num * scale * ng_ref[...]
        y_ref[0, rows, :] = (og_ref[0, rows, :].astype(F32) * hn).astype(y_ref.dtype)

        kw_t = (k.astype(F32).T * w_all[g:g + 1]).astype(BF16)
        state = s_old[g:g + 1] * state + _dot(kw_t, v_ext)
    c_ref[...] = state


def _mlstm(q, k, v, o, gates, b_gates, norm_g):
    b, s, _ = q.shape
    dk, dv, L = MLSTM_QK_DIM, MLSTM_V_DIM, MLSTM_CHUNK
    nc = s // L
    g = MLSTM_CHUNKS_PER_STEP
    rows = g * L
    gates = gates.reshape(b, 2 * MLSTM_HEADS, nc, L)
    last = nc // g - 1

    def gate_spec(first, ahead):
        return pl.BlockSpec((1, 1, g, L),
                            lambda bi, h, c: (bi, first + h, jnp.minimum(c + ahead, last), 0))

    return pl.pallas_call(
        functools.partial(_mlstm_kernel, chunks=g),
        grid=(b, MLSTM_HEADS, nc // g),
        in_specs=[pl.BlockSpec(memory_space=pltpu.SMEM),
                  pl.BlockSpec((1, rows, dk), lambda bi, h, c: (bi, c, h)),
                  pl.BlockSpec((1, rows, dk), lambda bi, h, c: (bi, c, h)),
                  pl.BlockSpec((1, rows, dv), lambda bi, h, c: (bi, c, h)),
                  pl.BlockSpec((1, rows, dv), lambda bi, h, c: (bi, c, h)),
                  gate_spec(0, 0), gate_spec(MLSTM_HEADS, 0),
                  gate_spec(0, 1), gate_spec(MLSTM_HEADS, 1),
                  pl.BlockSpec((1, dv), lambda bi, h, c: (0, h))],
        out_specs=pl.BlockSpec((1, rows, dv), lambda bi, h, c: (bi, c, h)),
        out_shape=jax.ShapeDtypeStruct((b, s, MLSTM_HEADS * dv), BF16),
        scratch_shapes=[pltpu.VMEM((dk, dv + LANES), F32), pltpu.VMEM((1, 1), F32),
                        pltpu.VMEM((L, L), F32), pltpu.VMEM((g, L), F32), pltpu.VMEM((g, L), F32),
                        pltpu.VMEM((g, 1), F32)],
        compiler_params=_params("arbitrary", "arbitrary", "arbitrary"),
        name="mlstm",
    )(b_gates.reshape(-1), q, k, v, o, gates, gates, gates, gates, norm_g.reshape(1, -1))


def kernel(x, c, w_mod, b_mod, norm_g, w_ffn_gate, w_ffn_up, w_ffn_down, w_in_even, w_pool, pool_scale, diff_lambda, diff_subln_g, w_out_even, w_in_odd, b_gates_odd, mlstm_norm_g, w_out_odd, final_g):
    depth = w_mod.shape[0]
    mod = _modulation(c, w_mod, b_mod)
    pool_width = pool_scale.shape[-1]
    qk_width = DIFF_HEADS * 2 * DIFF_HEAD_DIM
    ffn_w = (w_ffn_gate.astype(BF16), w_ffn_up.astype(BF16), w_ffn_down.astype(BF16))
    for l in range(depth):
        x = _ffn(x, mod, l, 0, 0, norm_g[l, 0], *ffn_w, final_g, False)
        if l % 2 == 0:
            e = l // 2
            lambda_init = 0.8 - 0.6 * math.exp(-0.3 * l)
            u, k, qt, vt = _even_in_proj(x, mod, l, norm_g[l, 1], w_in_even[e], pool_width, qk_width)
            y_diff = _diff_attention(qt, k, vt, diff_lambda[e], diff_subln_g[e], lambda_init)
            mix = ("even", u, y_diff, w_pool[e], pool_scale[e], w_out_even[e])
        else:
            o = l // 2
            q, k, v, og, gates = _odd_in_proj(x, mod, l, norm_g[l, 1], w_in_odd[o],
                                              MLSTM_HEADS * MLSTM_QK_DIM, MLSTM_HEADS * MLSTM_V_DIM)
            y = _mlstm(q, k, v, og, gates, b_gates_odd[o], mlstm_norm_g[o])
            mix = ("odd", y, w_out_odd[o])
        x = _ffn(x, mod, l, 2, 1, norm_g[l, 2], *ffn_w, final_g, l == depth - 1, mix)
    return x
```

```python
import functools
import math

import jax
import jax.numpy as jnp
from jax import lax
from jax.experimental import pallas as pl
from jax.experimental.pallas import tpu as pltpu

F32 = jnp.float32
BF16 = jnp.bfloat16

EPS = 1e-6
N_SUB = 3
POOL_WINDOWS = (2, 4, 8, 16)
POOL_HALO = 16
DIFF_HEADS = 4
DIFF_HEAD_DIM = 64
MLSTM_HEADS = 4
MLSTM_QK_DIM = 128
MLSTM_V_DIM = 256
MLSTM_CHUNK = 128
LANES = 128

VMEM_LIMIT_BYTES = 56 * 1024 * 1024

MOD_ROWS = 256
FFN_ROWS = 1024
FFN_MIX_ROWS = 512
FFN_CAST_CHUNKS = 16
FFN_SUB_ROWS = 256
PROJ_ROWS = 1024
ATTN_ROWS = 512
ATTN_COL_GROUP = 256
ATTN_UNROLL = 4
LOG2_E = math.log2(math.e)
MLSTM_CHUNKS_PER_STEP = 32


def _params(*semantics):
    return pltpu.CompilerParams(dimension_semantics=semantics,
                                vmem_limit_bytes=VMEM_LIMIT_BYTES)


def _resident(block_shape, index_map):
    return pl.BlockSpec(block_shape, index_map, pipeline_mode=pl.Buffered(1))


def _dot(a, b):
    return jnp.dot(a, b, preferred_element_type=F32)


def _modnorm(x, g, mod, sub):
    shift = mod[3 * sub:3 * sub + 1]
    scale = mod[3 * sub + 1:3 * sub + 2]
    y = x * lax.rsqrt(jnp.mean(x * x, axis=-1, keepdims=True) + EPS)
    return (y * g) * (1.0 + scale) + shift


def _gate(mod, sub):
    return mod[3 * sub + 2:3 * sub + 3]


def _mod_kernel(c_ref, w_ref, b_ref, o_ref):
    @pl.when(pl.program_id(1) == 0)
    def _():
        o_ref[0] = jnp.broadcast_to(b_ref[0], o_ref.shape[1:])

    c = c_ref[...]
    c_act = c * jax.nn.sigmoid(c)
    w = w_ref[0]
    c_hi = c_act.astype(BF16)
    c_lo = (c_act - c_hi.astype(F32)).astype(BF16)
    w_hi = w.astype(BF16)
    w_lo = (w - w_hi.astype(F32)).astype(BF16)
    o_ref[0] += _dot(c_hi, w_hi) + (_dot(c_hi, w_lo) + _dot(c_lo, w_hi))


def _modulation(c, w_mod, b_mod):
    depth, d, n = w_mod.shape
    b = c.shape[0]
    tk = MOD_ROWS
    out = pl.pallas_call(
        _mod_kernel,
        grid=(depth, d // tk),
        in_specs=[pl.BlockSpec((b, tk), lambda l, j: (0, j)),
                  pl.BlockSpec((1, tk, n), lambda l, j: (l, j, 0)),
                  pl.BlockSpec((1, 1, n), lambda l, j: (l, 0, 0))],
        out_specs=pl.BlockSpec((1, b, n), lambda l, j: (l, 0, 0)),
        out_shape=jax.ShapeDtypeStruct((depth, b, n), F32),
        compiler_params=_params("arbitrary", "arbitrary"),
        name="modulation",
    )(c, w_mod, b_mod.reshape(depth, 1, n))
    return out.reshape(depth, b, 3 * N_SUB, d)


def _swiglu_residual(x, mod, sub, g, wg_ref, wu_ref, wd_ref, fg, final_norm):
    outs = []
    for r in range(0, x.shape[0], FFN_SUB_ROWS):
        xs = x[r:r + FFN_SUB_ROWS]
        h = _modnorm(xs, g, mod, sub).astype(BF16)
        gate_act = _dot(h, wg_ref[...])
        up = _dot(h, wu_ref[...])
        a = (gate_act * jax.nn.sigmoid(gate_act) * up).astype(BF16)
        out = xs + (0.5 * _gate(mod, sub)) * _dot(a, wd_ref[...])
        if final_norm:
            out = out * lax.rsqrt(jnp.mean(out * out, axis=-1, keepdims=True) + EPS) * fg
        outs.append(out)
    return jnp.concatenate(outs, axis=0)


def _pool_mixer(u, halo, first_row, wp_ref, scale):
    tm = u.shape[0]
    ext = jnp.concatenate([halo, u], axis=0)
    pos = first_row + lax.broadcasted_iota(jnp.int32, (tm, 1), 0)
    gw = wp_ref.shape[-1]
    sums = ext
    shift = 1
    parts = []
    for g, w in enumerate(POOL_WINDOWS):
        while shift < w:
            sums = sums + pltpu.roll(sums, shift, 0)
            shift *= 2
        total = sums[POOL_HALO:, g * gw:(g + 1) * gw]
        count = jnp.minimum(pos + 1, w).astype(F32)
        pooled = total / count - u[:, g * gw:(g + 1) * gw]
        parts.append(_dot(pooled.astype(BF16), wp_ref[g]))
    return (jnp.concatenate(parts, axis=1) * scale).astype(BF16)


def _mixed_input(mix, x_ref, mod, mix_refs):
    if mix is None:
        return x_ref[0]
    if mix == "odd":
        y_ref, wo_ref = mix_refs
        return x_ref[0] + _gate(mod, 1) * _dot(y_ref[0], wo_ref[...])
    u_ref, halo_ref, yd_ref, wp_ref, ps_ref, wo_ref = mix_refs
    i = pl.program_id(1)
    tm = u_ref.shape[1]
    halo = halo_ref[0] * (i > 0).astype(F32)
    y_pool = _pool_mixer(u_ref[0], halo, i * tm, wp_ref, ps_ref[...])
    pw = y_pool.shape[1]
    y = _dot(y_pool, wo_ref[:pw, :]) + _dot(yd_ref[0], wo_ref[pw:, :])
    return x_ref[0] + _gate(mod, 1) * y


_N_MIX_REFS = {None: 0, "odd": 2, "even": 6}


def _ffn_kernel(*refs, sub, final_norm, mix, cast_next):
    x_ref, mod_ref, g_ref, wg_ref, wu_ref, wd_ref, fg_ref = refs[:7]
    n_mix = _N_MIX_REFS[mix]
    mix_refs = refs[7:7 + n_mix]
    n_cast = 3 if cast_next else 0
    cast_in = refs[7 + n_mix:7 + n_mix + n_cast]
    o_ref = refs[7 + n_mix + n_cast]
    cast_out = refs[8 + n_mix + n_cast:]
    mod = mod_ref[0, 0]
    x = _mixed_input(mix, x_ref, mod, mix_refs)
    o_ref[0] = _swiglu_residual(x, mod, sub, g_ref[...], wg_ref, wu_ref, wd_ref, fg_ref[...], final_norm)
    for src_ref, dst_ref in zip(cast_in, cast_out):
        dst_ref[...] = src_ref[0, 0].astype(BF16)


def _ffn(x, mod, layer, sub, g, weights, final_g, final_norm, mix=None, cast_next=None):
    b, s, d = x.shape
    f = weights[0].shape[-1]
    tm = FFN_ROWS if mix is None else FFN_MIX_ROWS
    n_i = s // tm
    row = lambda bi, i: (bi, i, 0)
    const2 = lambda bi, i: (0, 0)
    in_specs = [pl.BlockSpec((1, tm, d), row),
                pl.BlockSpec((1, 1, 3 * N_SUB, d), lambda bi, i: (layer, bi, 0, 0)),
                pl.BlockSpec((1, d), const2),
                _resident((d, f), const2),
                _resident((d, f), const2),
                _resident((f, d), const2),
                pl.BlockSpec((1, d), const2)]
    args = [x, mod, g.reshape(1, d), *weights, final_g.reshape(1, d)]
    kind = None if mix is None else mix[0]
    if kind == "odd":
        _, y, w_out = mix
        in_specs += [pl.BlockSpec((1, tm, y.shape[-1]), row), _resident(w_out.shape, const2)]
        args += [y, w_out.astype(BF16)]
    elif kind == "even":
        _, u, y_diff, w_pool, pool_scale, w_out = mix
        pw = u.shape[-1]
        halo_blocks = tm // POOL_HALO
        in_specs += [pl.BlockSpec((1, tm, pw), row),
                     pl.BlockSpec((1, POOL_HALO, pw),
                                  lambda bi, i: (bi, jnp.maximum(i * halo_blocks - 1, 0), 0)),
                     pl.BlockSpec((1, tm, y_diff.shape[-1]), row),
                     _resident(w_pool.shape, lambda bi, i: (0, 0, 0)),
                     pl.BlockSpec((1, pw), const2),
                     _resident(w_out.shape, const2)]
        args += [u, u, y_diff, w_pool.astype(BF16), pool_scale.reshape(1, pw), w_out.astype(BF16)]
    out_specs = [pl.BlockSpec((1, tm, d), row)]
    out_shape = [jax.ShapeDtypeStruct((b, s, d), F32)]
    if cast_next is not None:
        stacked, next_layer, next_which = cast_next
        steps = b * n_i
        assert steps % FFN_CAST_CHUNKS == 0
        chunk_of = lambda bi, i: (bi * n_i + i) // (steps // FFN_CAST_CHUNKS)
        for w in stacked:
            rows, cols = w.shape[-2:]
            assert rows % (FFN_CAST_CHUNKS * 16) == 0
            chunk = rows // FFN_CAST_CHUNKS
            in_specs.append(pl.BlockSpec((1, 1, chunk, cols),
                                         lambda bi, i: (next_layer, next_which, chunk_of(bi, i), 0)))
            args.append(w)
            out_specs.append(pl.BlockSpec((chunk, cols), lambda bi, i: (chunk_of(bi, i), 0)))
            out_shape.append(jax.ShapeDtypeStruct((rows, cols), BF16))
    outs = pl.pallas_call(
        functools.partial(_ffn_kernel, sub=sub, final_norm=final_norm, mix=kind,
                          cast_next=cast_next is not None),
        grid=(b, n_i),
        in_specs=in_specs,
        out_specs=out_specs,
        out_shape=out_shape,
        compiler_params=_params("arbitrary", "arbitrary"),
        name="ffn" if kind is None else kind + "_mix_ffn",
    )(*args)
    return outs[0], (tuple(outs[1:]) if cast_next is not None else None)


def _even_in_kernel(x_ref, mod_ref, g_ref, w_ref, wt_ref, u_ref, k_ref, qt_ref, vt_ref, *, sub):
    h = _modnorm(x_ref[0], g_ref[...], mod_ref[0, 0], sub).astype(BF16)
    p = _dot(h, w_ref[...])
    pw = u_ref.shape[-1]
    u_ref[0] = p[:, :pw]
    k_ref[0] = p[:, pw:].astype(BF16)
    pt = lax.dot_general(wt_ref[...], h, (((1,), (1,)), ((), ())), preferred_element_type=F32)
    qw = qt_ref.shape[1]
    qt_ref[0] = (pt[:qw] * (DIFF_HEAD_DIM ** -0.5 * LOG2_E)).astype(BF16)
    vt_ref[0, 0] = pt[qw:].astype(BF16)


def _even_in_proj(x, mod, layer, g, w_in, pool_width, qk_width):
    b, s, d = x.shape
    n = w_in.shape[1]
    v_width = n - pool_width - 2 * qk_width
    tm = ATTN_ROWS
    k0 = pool_width + qk_width
    w_uk = jnp.concatenate([w_in[:, :pool_width], w_in[:, k0:k0 + qk_width]], axis=1).astype(BF16)
    w_qv_t = jnp.concatenate([w_in[:, pool_width:k0], w_in[:, k0 + qk_width:]], axis=1).T.astype(BF16)
    row = lambda bi, i: (bi, i, 0)
    return pl.pallas_call(
        functools.partial(_even_in_kernel, sub=1),
        grid=(b, s // tm),
        in_specs=[pl.BlockSpec((1, tm, d), row),
                  pl.BlockSpec((1, 1, 3 * N_SUB, d), lambda bi, i: (layer, bi, 0, 0)),
                  pl.BlockSpec((1, d), lambda bi, i: (0, 0)),
                  _resident(w_uk.shape, lambda bi, i: (0, 0)),
                  _resident(w_qv_t.shape, lambda bi, i: (0, 0))],
        out_specs=[pl.BlockSpec((1, tm, pool_width), row),
                   pl.BlockSpec((1, tm, qk_width), row),
                   pl.BlockSpec((1, qk_width, tm), lambda bi, i: (bi, 0, i)),
                   pl.BlockSpec((1, 1, v_width, tm), lambda bi, i: (bi, i, 0, 0))],
        out_shape=[jax.ShapeDtypeStruct((b, s, pool_width), F32),
                   jax.ShapeDtypeStruct((b, s, qk_width), BF16),
                   jax.ShapeDtypeStruct((b, qk_width, s), BF16),
                   jax.ShapeDtypeStruct((b, s // tm, v_width, tm), BF16)],
        compiler_params=_params("arbitrary", "arbitrary"),
        name="even_in_proj",
    )(x, mod, g.reshape(1, d), w_uk, w_qv_t)


def _diff_attn_kernel(qt_ref, k_ref, vt_ref, lam_ref, sg_ref, o_ref, m_ref, l_ref, acc_ref,
                      s0_ref, s1_ref, cm0_ref, cm1_ref, *, lambda_init, t):
    i = pl.program_id(2)
    qt = qt_ref[0].astype(F32)
    feat = lax.broadcasted_iota(jnp.int32, qt.shape, 0)
    qs = jnp.concatenate([jnp.where(feat < DIFF_HEAD_DIM, qt, 0.0),
                          jnp.where(feat >= DIFF_HEAD_DIM, qt, 0.0)], axis=1).astype(BF16)
    m_ref[...] = jnp.full_like(m_ref, -jnp.inf)
    l_ref[...] = jnp.zeros_like(l_ref)
    acc_ref[...] = jnp.zeros_like(acc_ref)
    groups = [slice(c, c + ATTN_COL_GROUP) for c in range(0, 2 * t, ATTN_COL_GROUP)]
    s_refs = (s0_ref, s1_ref)
    cm_refs = (cm0_ref, cm1_ref)

    def produce(j, slot, cols):
        kb = k_ref[0, pl.ds(pl.multiple_of(j * t, t), t), :]
        s = _dot(kb, qs[:, cols])
        s_refs[slot][:, cols] = s
        cm_refs[slot][:, cols] = jnp.max(s, axis=0, keepdims=True)

    def consume(j, slot, cols, diagonal):
        if diagonal:
            first_q = cols.start % t
            n_keys = first_q + ATTN_COL_GROUP
            s = s_refs[slot][:n_keys, cols]
            key = lax.broadcasted_iota(jnp.int32, s.shape, 0)
            qry = lax.broadcasted_iota(jnp.int32, s.shape, 1) + first_q
            s = jnp.where(key <= qry, s, -jnp.inf)
            cm = jnp.max(s, axis=0, keepdims=True)
            vt = vt_ref[0, j][:, :n_keys]
        else:
            s = s_refs[slot][:, cols]
            cm = cm_refs[slot][:, cols]
            vt = vt_ref[0, j]
        m = m_ref[:, cols]
        m_new = jnp.maximum(m, cm)
        alpha = jnp.exp2(m - m_new)
        p = jnp.exp2(s - m_new)
        l_ref[:, cols] = alpha * l_ref[:, cols] + jnp.sum(p, axis=0, keepdims=True)
        acc_ref[:, cols] = alpha * acc_ref[:, cols] + _dot(vt, p.astype(BF16))
        m_ref[:, cols] = m_new

    def advance(j, slot):
        for cols in groups:
            produce(j + 1, 1 - slot, cols)
            consume(j, slot, cols, False)

    def diagonal(slot):
        for cols in groups:
            consume(i, slot, cols, True)

    for cols in groups:
        produce(0, 0, cols)

    def body(jj, carry):
        for d in range(ATTN_UNROLL):
            advance(ATTN_UNROLL * jj + d, d % 2)
        return carry

    lax.fori_loop(0, i // ATTN_UNROLL, body, 0)
    rest = i % ATTN_UNROLL

    @pl.when(rest >= 2)
    def _():
        advance(i - rest, 0)
        advance(i - rest + 1, 1)

    @pl.when(rest % 2 == 0)
    def _():
        diagonal(0)

    @pl.when(rest % 2 == 1)
    def _():
        advance(i - 1, 0)
        diagonal(1)

    o = acc_ref[...] / l_ref[...]
    lp = lam_ref[...]
    lam = (jnp.exp(jnp.sum(lp[0:1] * lp[1:2], axis=1, keepdims=True))
           - jnp.exp(jnp.sum(lp[2:3] * lp[3:4], axis=1, keepdims=True)) + lambda_init)
    o = o[:, :t] - lam * o[:, t:]
    o = o * lax.rsqrt(jnp.mean(o * o, axis=0, keepdims=True) + EPS) * sg_ref[...]
    o_ref[0] = (o * (1.0 - lambda_init)).T.astype(o_ref.dtype)


def _diff_attention(qt, k, vt, lam_params, subln_g, lambda_init):
    b, s, _ = k.shape
    hd = 2 * DIFF_HEAD_DIM
    t = ATTN_ROWS
    assert s % t == 0 and vt.shape[-1] == t and (2 * t) % ATTN_COL_GROUP == 0
    return pl.pallas_call(
        functools.partial(_diff_attn_kernel, lambda_init=lambda_init, t=t),
        grid=(b, DIFF_HEADS, s // t),
        in_specs=[pl.BlockSpec((1, hd, t), lambda bi, h, i: (bi, h, i)),
                  pl.BlockSpec((1, s, hd), lambda bi, h, i: (bi, 0, h)),
                  pl.BlockSpec((1, s // t, hd, t), lambda bi, h, i: (bi, 0, h, 0)),
                  pl.BlockSpec(lam_params.shape, lambda bi, h, i: (0, 0)),
                  pl.BlockSpec((hd, 1), lambda bi, h, i: (0, 0))],
        out_specs=pl.BlockSpec((1, t, hd), lambda bi, h, i: (bi, i, h)),
        out_shape=jax.ShapeDtypeStruct((b, s, DIFF_HEADS * hd), BF16),
        scratch_shapes=[pltpu.VMEM((1, 2 * t), F32), pltpu.VMEM((1, 2 * t), F32),
                        pltpu.VMEM((hd, 2 * t), F32),
                        pltpu.VMEM((t, 2 * t), F32), pltpu.VMEM((t, 2 * t), F32),
                        pltpu.VMEM((1, 2 * t), F32), pltpu.VMEM((1, 2 * t), F32)],
        compiler_params=_params("arbitrary", "arbitrary", "arbitrary"),
        name="diff_attention",
    )(qt, k, vt, lam_params, subln_g.reshape(hd, 1))


def _odd_in_kernel(x_ref, mod_ref, g_ref, w_ref, wgate_ref, q_ref, k_ref, v_ref, o_ref, gates_ref, *, sub):
    h = _modnorm(x_ref[0], g_ref[...], mod_ref[0, 0], sub).astype(BF16)
    p = _dot(h, w_ref[...])
    nq = q_ref.shape[-1]
    nv = v_ref.shape[-1]
    q_ref[0] = p[:, :nq].astype(BF16)
    k_ref[0] = (p[:, nq:2 * nq] * (MLSTM_QK_DIM ** -0.5)).astype(BF16)
    v_ref[0] = p[:, 2 * nq:2 * nq + nv].astype(BF16)
    o_ref[0] = jax.nn.sigmoid(p[:, 2 * nq + nv:]).astype(BF16)
    gates_ref[0] = _dot(h, wgate_ref[...]).T[:gates_ref.shape[1]]


def _odd_in_proj(x, mod, layer, g, w_in, nq, nv):
    b, s, d = x.shape
    n_main = 2 * nq + 2 * nv
    n_gates = w_in.shape[1] - n_main
    w_main = w_in[:, :n_main].astype(BF16)
    w_gate = jnp.pad(w_in[:, n_main:], ((0, 0), (0, LANES - n_gates))).astype(BF16)
    tm = PROJ_ROWS
    row = lambda bi, i: (bi, i, 0)
    return pl.pallas_call(
        functools.partial(_odd_in_kernel, sub=1),
        grid=(b, s // tm),
        in_specs=[pl.BlockSpec((1, tm, d), row),
                  pl.BlockSpec((1, 1, 3 * N_SUB, d), lambda bi, i: (layer, bi, 0, 0)),
                  pl.BlockSpec((1, d), lambda bi, i: (0, 0)),
                  _resident((d, n_main), lambda bi, i: (0, 0)),
                  _resident((d, LANES), lambda bi, i: (0, 0))],
        out_specs=[pl.BlockSpec((1, tm, nq), row),
                   pl.BlockSpec((1, tm, nq), row),
                   pl.BlockSpec((1, tm, nv), row),
                   pl.BlockSpec((1, tm, nv), row),
                   pl.BlockSpec((1, n_gates, tm), lambda bi, i: (bi, 0, i))],
        out_shape=[jax.ShapeDtypeStruct((b, s, nq), BF16),
                   jax.ShapeDtypeStruct((b, s, nq), BF16),
                   jax.ShapeDtypeStruct((b, s, nv), BF16),
                   jax.ShapeDtypeStruct((b, s, nv), BF16),
                   jax.ShapeDtypeStruct((b, n_gates, s), F32)],
        compiler_params=_params("arbitrary", "arbitrary"),
        name="odd_in_proj",
    )(x, mod, g.reshape(1, d), w_main, w_gate)


def _log_sigmoid(x):
    return jnp.minimum(x, 0.0) - jnp.log1p(jnp.exp(-jnp.abs(x)))


def _lane_scan(x, combine, fill):
    lane = lax.broadcasted_iota(jnp.int32, x.shape, 1)
    shift = 1
    while shift < x.shape[-1]:
        x = combine(x, jnp.where(lane >= shift, pltpu.roll(x, shift, 1), fill))
        shift *= 2
    return x


def _mlstm_kernel(bg_ref, q_ref, k_ref, v_ref, og_ref, ig_ref, fg_ref, ig_next_ref, fg_next_ref,
                  ng_ref, y_ref, c_ref, m_ref, cols_ref, src_ref, w_ref, sold_ref, *, chunks):
    head = pl.program_id(1)
    step = pl.program_id(2)
    L = MLSTM_CHUNK
    G = chunks

    def store_gate_stats(i_ref, f_ref):
        i_all = i_ref[0, 0] + bg_ref[head]
        logf = _log_sigmoid(f_ref[0, 0] + bg_ref[MLSTM_HEADS + head])
        b_all = _lane_scan(logf, jnp.add, 0.0)
        b_last = b_all[:, L - 1:L]
        a_all = b_last - b_all + i_all
        m_loc = jnp.max(a_all, axis=1, keepdims=True)

        m_in = []
        m = m_ref[...]
        for g in range(G):
            m_in.append(m)
            m = jnp.maximum(b_last[g:g + 1] + m, m_loc[g:g + 1])
        m_out = m_in[1:] + [m]
        m_ref[...] = m
        m_prev = jnp.concatenate(m_in, axis=0)
        m_next = jnp.concatenate(m_out, axis=0)
        sold_ref[...] = jnp.exp(b_last + m_prev - m_next)
        w_ref[...] = jnp.exp(a_all - m_next)

        src = i_all - b_all
        src_ref[...] = src
        inter_log = b_all + m_prev
        m_t = jnp.maximum(inter_log, b_all + _lane_scan(src, jnp.maximum, -jnp.inf))
        per_row = jnp.concatenate([b_all - m_t, jnp.exp(inter_log - m_t), jnp.exp(-m_t)], axis=0)
        per_row = jnp.concatenate([per_row, jnp.zeros((L - 3 * G, L), F32)], axis=0)
        cols_ref[...] = per_row.T

    @pl.when(step == 0)
    def _():
        c_ref[...] = jnp.zeros_like(c_ref)
        m_ref[...] = jnp.zeros_like(m_ref)
        store_gate_stats(ig_ref, fg_ref)

    cols = cols_ref[...]
    src = src_ref[...]
    w_all = w_ref[...]
    s_old = sold_ref[...]
    store_gate_stats(ig_next_ref, fg_next_ref)

    causal = (lax.broadcasted_iota(jnp.int32, (L, L), 1) <= lax.broadcasted_iota(jnp.int32, (L, L), 0))
    dv = v_ref.shape[-1]
    ones_col = (lax.broadcasted_iota(jnp.int32, (L, LANES), 1) == 0).astype(BF16)
    state = c_ref[...]
    for g in range(G):
        rows = slice(g * L, (g + 1) * L)
        q = q_ref[0, rows, :]
        k = k_ref[0, rows, :]
        v_ext = jnp.concatenate([v_ref[0, rows, :], ones_col], axis=1)
        dest = cols[:, g:g + 1]
        inter_w = cols[:, G + g:G + g + 1]
        floor = cols[:, 2 * G + g:2 * G + g + 1]

        dw = jnp.exp(jnp.where(causal, dest + src[g:g + 1], -jnp.inf))
        s = lax.dot_general(q, k, (((1,), (1,)), ((), ())), preferred_element_type=F32) * dw
        q_w = (inter_w * q.astype(F32)).astype(BF16)
        nd = _dot(s.astype(BF16), v_ext) + _dot(q_w, state.astype(BF16))
        num = nd[:, :dv]
        den = nd[:, dv:dv + 1]
        r = 1.0 / jnp.maximum(jnp.abs(den), floor)
        scale = r * lax.rsqrt(r * r * jnp.mean(num * num, axis=-1, keepdims=True) + EPS)
        hn = num * scale * ng_ref[...]
        y_ref[0, rows, :] = (og_ref[0, rows, :].astype(F32) * hn).astype(y_ref.dtype)

        kw_t = (k.astype(F32).T * w_all[g:g + 1]).astype(BF16)
        state = s_old[g:g + 1] * state + _dot(kw_t, v_ext)
    c_ref[...] = state


def _mlstm(q, k, v, o, gates, b_gates, norm_g):
    b, s, _ = q.shape
    dk, dv, L = MLSTM_QK_DIM, MLSTM_V_DIM, MLSTM_CHUNK
    nc = s // L
    g = MLSTM_CHUNKS_PER_STEP
    rows = g * L
    gates = gates.reshape(b, 2 * MLSTM_HEADS, nc, L)
    last = nc // g - 1

    def gate_spec(first, ahead):
        return pl.BlockSpec((1, 1, g, L),
                            lambda bi, h, c: (bi, first + h, jnp.minimum(c + ahead, last), 0))

    return pl.pallas_call(
        functools.partial(_mlstm_kernel, chunks=g),
        grid=(b, MLSTM_HEADS, nc // g),
        in_specs=[pl.BlockSpec(memory_space=pltpu.SMEM),
                  pl.BlockSpec((1, rows, dk), lambda bi, h, c: (bi, c, h)),
                  pl.BlockSpec((1, rows, dk), lambda bi, h, c: (bi, c, h)),
                  pl.BlockSpec((1, rows, dv), lambda bi, h, c: (bi, c, h)),
                  pl.BlockSpec((1, rows, dv), lambda bi, h, c: (bi, c, h)),
                  gate_spec(0, 0), gate_spec(MLSTM_HEADS, 0),
                  gate_spec(0, 1), gate_spec(MLSTM_HEADS, 1),
                  pl.BlockSpec((1, dv), lambda bi, h, c: (0, h))],
        out_specs=pl.BlockSpec((1, rows, dv), lambda bi, h, c: (bi, c, h)),
        out_shape=jax.ShapeDtypeStruct((b, s, MLSTM_HEADS * dv), BF16),
        scratch_shapes=[pltpu.VMEM((dk, dv + LANES), F32), pltpu.VMEM((1, 1), F32),
                        pltpu.VMEM((L, L), F32), pltpu.VMEM((g, L), F32), pltpu.VMEM((g, L), F32),
                        pltpu.VMEM((g, 1), F32)],
        compiler_params=_params("arbitrary", "arbitrary", "arbitrary"),
        name="mlstm",
    )(b_gates.reshape(-1), q, k, v, o, gates, gates, gates, gates, norm_g.reshape(1, -1))


def kernel(x, c, w_mod, b_mod, norm_g, w_ffn_gate, w_ffn_up, w_ffn_down, w_in_even, w_pool, pool_scale, diff_lambda, diff_subln_g, w_out_even, w_in_odd, b_gates_odd, mlstm_norm_g, w_out_odd, final_g):
    depth = w_mod.shape[0]
    mod = _modulation(c, w_mod, b_mod)
    pool_width = pool_scale.shape[-1]
    qk_width = DIFF_HEADS * 2 * DIFF_HEAD_DIM
    ffn_stacked = (w_ffn_gate, w_ffn_up, w_ffn_down)
    ffn_w = tuple(w[0, 0].astype(BF16) for w in ffn_stacked)
    for l in range(depth):
        x, ffn_w = _ffn(x, mod, l, 0, norm_g[l, 0], ffn_w, final_g, False,
                        cast_next=(ffn_stacked, l, 1))
        if l % 2 == 0:
            e = l // 2
            lambda_init = 0.8 - 0.6 * math.exp(-0.3 * l)
            u, k, qt, vt = _even_in_proj(x, mod, l, norm_g[l, 1], w_in_even[e], pool_width, qk_width)
            y_diff = _diff_attention(qt, k, vt, diff_lambda[e], diff_subln_g[e], lambda_init)
            mix = ("even", u, y_diff, w_pool[e], pool_scale[e], w_out_even[e])
        else:
            o = l // 2
            q, k, v, og, gates = _odd_in_proj(x, mod, l, norm_g[l, 1], w_in_odd[o],
                                              MLSTM_HEADS * MLSTM_QK_DIM, MLSTM_HEADS * MLSTM_V_DIM)
            y = _mlstm(q, k, v, og, gates, b_gates_odd[o], mlstm_norm_g[o])
            mix = ("odd", y, w_out_odd[o])
        last = l == depth - 1
        x, ffn_w = _ffn(x, mod, l, 2, norm_g[l, 2], ffn_w, final_g, last, mix,
                        cast_next=None if last else (ffn_stacked, l + 1, 0))
    return x
```

```python
import functools
import math

import jax
import jax.numpy as jnp
from jax import lax
from jax.experimental import pallas as pl
from jax.experimental.pallas import tpu as pltpu

F32 = jnp.float32
BF16 = jnp.bfloat16

EPS = 1e-6
N_SUB = 3
POOL_WINDOWS = (2, 4, 8, 16)
POOL_HALO = 16
DIFF_HEADS = 4
DIFF_HEAD_DIM = 64
MLSTM_HEADS = 4
MLSTM_QK_DIM = 128
MLSTM_V_DIM = 256
MLSTM_CHUNK = 128
LANES = 128

VMEM_LIMIT_BYTES = 56 * 1024 * 1024

MOD_ROWS = 256
FFN_ROWS = 1024
FFN_MIX_ROWS = 512
FFN_CAST_CHUNKS = 16
FFN_SUB_ROWS = 256
PROJ_ROWS = 1024
ATTN_ROWS = 512
ATTN_COL_GROUP = 256
ATTN_TILES_PER_STEP = 2
ATTN_UNROLL = 4
LOG2_E = math.log2(math.e)
MLSTM_CHUNKS_PER_STEP = 32


def _params(*semantics):
    return pltpu.CompilerParams(dimension_semantics=semantics,
                                vmem_limit_bytes=VMEM_LIMIT_BYTES)


def _resident(block_shape, index_map):
    return pl.BlockSpec(block_shape, index_map, pipeline_mode=pl.Buffered(1))


def _dot(a, b):
    return jnp.dot(a, b, preferred_element_type=F32)


def _modnorm(x, g, mod, sub):
    shift = mod[3 * sub:3 * sub + 1]
    scale = mod[3 * sub + 1:3 * sub + 2]
    y = x * lax.rsqrt(jnp.mean(x * x, axis=-1, keepdims=True) + EPS)
    return (y * g) * (1.0 + scale) + shift


def _gate(mod, sub):
    return mod[3 * sub + 2:3 * sub + 3]


def _mod_kernel(c_ref, w_ref, b_ref, o_ref):
    @pl.when(pl.program_id(1) == 0)
    def _():
        o_ref[0] = jnp.broadcast_to(b_ref[0], o_ref.shape[1:])

    c = c_ref[...]
    c_act = c * jax.nn.sigmoid(c)
    w = w_ref[0]
    c_hi = c_act.astype(BF16)
    c_lo = (c_act - c_hi.astype(F32)).astype(BF16)
    w_hi = w.astype(BF16)
    w_lo = (w - w_hi.astype(F32)).astype(BF16)
    o_ref[0] += _dot(c_hi, w_hi) + (_dot(c_hi, w_lo) + _dot(c_lo, w_hi))


def _modulation(c, w_mod, b_mod):
    depth, d, n = w_mod.shape
    b = c.shape[0]
    tk = MOD_ROWS
    out = pl.pallas_call(
        _mod_kernel,
        grid=(depth, d // tk),
        in_specs=[pl.BlockSpec((b, tk), lambda l, j: (0, j)),
                  pl.BlockSpec((1, tk, n), lambda l, j: (l, j, 0)),
                  pl.BlockSpec((1, 1, n), lambda l, j: (l, 0, 0))],
        out_specs=pl.BlockSpec((1, b, n), lambda l, j: (l, 0, 0)),
        out_shape=jax.ShapeDtypeStruct((depth, b, n), F32),
        compiler_params=_params("arbitrary", "arbitrary"),
        name="modulation",
    )(c, w_mod, b_mod.reshape(depth, 1, n))
    return out.reshape(depth, b, 3 * N_SUB, d)


def _swiglu_residual(x, mod, sub, g, wg_ref, wu_ref, wd_ref, fg, final_norm):
    outs = []
    for r in range(0, x.shape[0], FFN_SUB_ROWS):
        xs = x[r:r + FFN_SUB_ROWS]
        h = _modnorm(xs, g, mod, sub).astype(BF16)
        gate_act = _dot(h, wg_ref[...])
        up = _dot(h, wu_ref[...])
        a = (gate_act * jax.nn.sigmoid(gate_act) * up).astype(BF16)
        out = xs + (0.5 * _gate(mod, sub)) * _dot(a, wd_ref[...])
        if final_norm:
            out = out * lax.rsqrt(jnp.mean(out * out, axis=-1, keepdims=True) + EPS) * fg
        outs.append(out)
    return jnp.concatenate(outs, axis=0)


def _pool_mixer(u, halo, first_row, wp_ref, scale):
    tm = u.shape[0]
    ext = jnp.concatenate([halo, u], axis=0)
    pos = first_row + lax.broadcasted_iota(jnp.int32, (tm, 1), 0)
    gw = wp_ref.shape[-1]
    sums = ext
    shift = 1
    parts = []
    for g, w in enumerate(POOL_WINDOWS):
        while shift < w:
            sums = sums + pltpu.roll(sums, shift, 0)
            shift *= 2
        total = sums[POOL_HALO:, g * gw:(g + 1) * gw]
        count = jnp.minimum(pos + 1, w).astype(F32)
        pooled = total / count - u[:, g * gw:(g + 1) * gw]
        parts.append(_dot(pooled.astype(BF16), wp_ref[g]))
    return (jnp.concatenate(parts, axis=1) * scale).astype(BF16)


def _mixed_input(mix, x_ref, mod, mix_refs):
    if mix is None:
        return x_ref[0]
    if mix == "odd":
        y_ref, wo_ref = mix_refs
        return x_ref[0] + _gate(mod, 1) * _dot(y_ref[0], wo_ref[...])
    u_ref, halo_ref, yd_ref, wp_ref, ps_ref, wo_ref = mix_refs
    i = pl.program_id(1)
    tm = u_ref.shape[1]
    halo = halo_ref[0] * (i > 0).astype(F32)
    y_pool = _pool_mixer(u_ref[0], halo, i * tm, wp_ref, ps_ref[...])
    pw = y_pool.shape[1]
    y = _dot(y_pool, wo_ref[:pw, :]) + _dot(yd_ref[0], wo_ref[pw:, :])
    return x_ref[0] + _gate(mod, 1) * y


_N_MIX_REFS = {None: 0, "odd": 2, "even": 6}


def _ffn_kernel(*refs, sub, final_norm, mix, n_cast):
    x_ref, mod_ref, g_ref, wg_ref, wu_ref, wd_ref, fg_ref = refs[:7]
    n_mix = _N_MIX_REFS[mix]
    mix_refs = refs[7:7 + n_mix]
    cast_in = refs[7 + n_mix:7 + n_mix + n_cast]
    o_ref = refs[7 + n_mix + n_cast]
    cast_out = refs[8 + n_mix + n_cast:]
    mod = mod_ref[0, 0]
    x = _mixed_input(mix, x_ref, mod, mix_refs)
    o_ref[0] = _swiglu_residual(x, mod, sub, g_ref[...], wg_ref, wu_ref, wd_ref, fg_ref[...], final_norm)
    for src_ref, dst_ref in zip(cast_in, cast_out):
        dst_ref[...] = src_ref[(0,) * (len(src_ref.shape) - 2)].astype(BF16)


def _ffn(x, mod, layer, sub, g, weights, final_g, final_norm, mix=None, casts=()):
    b, s, d = x.shape
    f = weights[0].shape[-1]
    tm = FFN_ROWS if mix is None else FFN_MIX_ROWS
    n_i = s // tm
    row = lambda bi, i: (bi, i, 0)
    const2 = lambda bi, i: (0, 0)
    in_specs = [pl.BlockSpec((1, tm, d), row),
                pl.BlockSpec((1, 1, 3 * N_SUB, d), lambda bi, i: (layer, bi, 0, 0)),
                pl.BlockSpec((1, d), const2),
                _resident((d, f), const2),
                _resident((d, f), const2),
                _resident((f, d), const2),
                pl.BlockSpec((1, d), const2)]
    args = [x, mod, g.reshape(1, d), *weights, final_g.reshape(1, d)]
    kind = None if mix is None else mix[0]
    if kind == "odd":
        _, y, w_out = mix
        in_specs += [pl.BlockSpec((1, tm, y.shape[-1]), row), _resident(w_out.shape, const2)]
        args += [y, w_out.astype(BF16)]
    elif kind == "even":
        _, u, y_diff, w_pool, pool_scale, w_out = mix
        pw = u.shape[-1]
        halo_blocks = tm // POOL_HALO
        in_specs += [pl.BlockSpec((1, tm, pw), row),
                     pl.BlockSpec((1, POOL_HALO, pw),
                                  lambda bi, i: (bi, jnp.maximum(i * halo_blocks - 1, 0), 0)),
                     pl.BlockSpec((1, tm, y_diff.shape[-1]), row),
                     _resident(w_pool.shape, lambda bi, i: (0, 0, 0)),
                     pl.BlockSpec((1, pw), const2),
                     _resident(w_out.shape, const2)]
        args += [u, u, y_diff, w_pool.astype(BF16), pool_scale.reshape(1, pw), w_out.astype(BF16)]
    out_specs = [pl.BlockSpec((1, tm, d), row)]
    out_shape = [jax.ShapeDtypeStruct((b, s, d), F32)]
    steps = b * n_i
    assert steps % FFN_CAST_CHUNKS == 0
    chunk_of = lambda bi, i: (bi * n_i + i) // (steps // FFN_CAST_CHUNKS)
    for w, lead in casts:
        rows, cols = w.shape[-2:]
        assert rows % (FFN_CAST_CHUNKS * 16) == 0
        chunk = rows // FFN_CAST_CHUNKS
        in_specs.append(pl.BlockSpec((1,) * len(lead) + (chunk, cols),
                                     lambda bi, i, lead=lead: (*lead, chunk_of(bi, i), 0)))
        args.append(w)
        out_specs.append(pl.BlockSpec((chunk, cols), lambda bi, i: (chunk_of(bi, i), 0)))
        out_shape.append(jax.ShapeDtypeStruct((rows, cols), BF16))
    outs = pl.pallas_call(
        functools.partial(_ffn_kernel, sub=sub, final_norm=final_norm, mix=kind,
                          n_cast=len(casts)),
        grid=(b, n_i),
        in_specs=in_specs,
        out_specs=out_specs,
        out_shape=out_shape,
        compiler_params=_params("arbitrary", "arbitrary"),
        name="ffn" if kind is None else kind + "_mix_ffn",
    )(*args)
    return outs[0], tuple(outs[1:])


def _even_in_kernel(x_ref, mod_ref, g_ref, w_ref, wt_ref, u_ref, k_ref, qt_ref, vt_ref, *, sub):
    h = _modnorm(x_ref[0], g_ref[...], mod_ref[0, 0], sub).astype(BF16)
    p = _dot(h, w_ref[...])
    pw = u_ref.shape[-1]
    u_ref[0] = p[:, :pw]
    k_ref[0] = p[:, pw:].astype(BF16)
    pt = lax.dot_general(wt_ref[...], h, (((1,), (1,)), ((), ())), preferred_element_type=F32)
    qw = qt_ref.shape[1]
    qt_ref[0] = (pt[:qw] * (DIFF_HEAD_DIM ** -0.5 * LOG2_E)).astype(BF16)
    vt_ref[0, 0] = pt[qw:].astype(BF16)


def _even_in_proj(x, mod, layer, g, w_in, pool_width, qk_width):
    b, s, d = x.shape
    n = w_in.shape[1]
    v_width = n - pool_width - 2 * qk_width
    tm = ATTN_ROWS
    k0 = pool_width + qk_width
    w_uk = jnp.concatenate([w_in[:, :pool_width], w_in[:, k0:k0 + qk_width]], axis=1).astype(BF16)
    w_qv_t = jnp.concatenate([w_in[:, pool_width:k0], w_in[:, k0 + qk_width:]], axis=1).T.astype(BF16)
    row = lambda bi, i: (bi, i, 0)
    return pl.pallas_call(
        functools.partial(_even_in_kernel, sub=1),
        grid=(b, s // tm),
        in_specs=[pl.BlockSpec((1, tm, d), row),
                  pl.BlockSpec((1, 1, 3 * N_SUB, d), lambda bi, i: (layer, bi, 0, 0)),
                  pl.BlockSpec((1, d), lambda bi, i: (0, 0)),
                  _resident(w_uk.shape, lambda bi, i: (0, 0)),
                  _resident(w_qv_t.shape, lambda bi, i: (0, 0))],
        out_specs=[pl.BlockSpec((1, tm, pool_width), row),
                   pl.BlockSpec((1, tm, qk_width), row),
                   pl.BlockSpec((1, qk_width, tm), lambda bi, i: (bi, 0, i)),
                   pl.BlockSpec((1, 1, v_width, tm), lambda bi, i: (bi, i, 0, 0))],
        out_shape=[jax.ShapeDtypeStruct((b, s, pool_width), F32),
                   jax.ShapeDtypeStruct((b, s, qk_width), BF16),
                   jax.ShapeDtypeStruct((b, qk_width, s), BF16),
                   jax.ShapeDtypeStruct((b, s // tm, v_width, tm), BF16)],
        compiler_params=_params("arbitrary", "arbitrary"),
        name="even_in_proj",
    )(x, mod, g.reshape(1, d), w_uk, w_qv_t)


def _diff_attn_kernel(qt_ref, k_ref, vt_ref, lam_ref, sg_ref, o_ref, m_ref, l_ref, acc_ref,
                      s0_ref, s1_ref, cm0_ref, cm1_ref, *, lambda_init, t):
    for part in range(ATTN_TILES_PER_STEP):
        rows = slice(part * t, (part + 1) * t)
        _diff_attn_query_tile(pl.program_id(2) * ATTN_TILES_PER_STEP + part, qt_ref[0, :, rows], k_ref,
                              vt_ref, lam_ref, sg_ref, o_ref.at[0, rows, :], m_ref, l_ref, acc_ref,
                              (s0_ref, s1_ref), (cm0_ref, cm1_ref), lambda_init, t)


def _diff_attn_query_tile(i, qt, k_ref, vt_ref, lam_ref, sg_ref, o_ref, m_ref, l_ref, acc_ref, s_refs,
                          cm_refs, lambda_init, t):
    qt = qt.astype(F32)
    feat = lax.broadcasted_iota(jnp.int32, qt.shape, 0)
    qs = jnp.concatenate([jnp.where(feat < DIFF_HEAD_DIM, qt, 0.0),
                          jnp.where(feat >= DIFF_HEAD_DIM, qt, 0.0)], axis=1).astype(BF16)
    m_ref[...] = jnp.full_like(m_ref, -jnp.inf)
    l_ref[...] = jnp.zeros_like(l_ref)
    acc_ref[...] = jnp.zeros_like(acc_ref)
    groups = [slice(c, c + ATTN_COL_GROUP) for c in range(0, 2 * t, ATTN_COL_GROUP)]

    def produce(j, slot, cols):
        kb = k_ref[0, pl.ds(pl.multiple_of(j * t, t), t), :]
        s = _dot(kb, qs[:, cols])
        s_refs[slot][:, cols] = s
        cm_refs[slot][:, cols] = jnp.max(s, axis=0, keepdims=True)

    def consume(j, slot, cols, diagonal):
        if diagonal:
            first_q = cols.start % t
            n_keys = first_q + ATTN_COL_GROUP
            s = s_refs[slot][:n_keys, cols]
            key = lax.broadcasted_iota(jnp.int32, s.shape, 0)
            qry = lax.broadcasted_iota(jnp.int32, s.shape, 1) + first_q
            s = jnp.where(key <= qry, s, -jnp.inf)
            cm = jnp.max(s, axis=0, keepdims=True)
            vt = vt_ref[0, j][:, :n_keys]
        else:
            s = s_refs[slot][:, cols]
            cm = cm_refs[slot][:, cols]
            vt = vt_ref[0, j]
        m = m_ref[:, cols]
        m_new = jnp.maximum(m, cm)
        alpha = jnp.exp2(m - m_new)
        p = jnp.exp2(s - m_new)
        l_ref[:, cols] = alpha * l_ref[:, cols] + jnp.sum(p, axis=0, keepdims=True)
        acc_ref[:, cols] = alpha * acc_ref[:, cols] + _dot(vt, p.astype(BF16))
        m_ref[:, cols] = m_new

    def advance(j, slot):
        for cols in groups:
            produce(j + 1, 1 - slot, cols)
            consume(j, slot, cols, False)

    def diagonal(slot):
        for cols in groups:
            consume(i, slot, cols, True)

    for cols in groups:
        produce(0, 0, cols)

    def body(jj, carry):
        for d in range(ATTN_UNROLL):
            advance(ATTN_UNROLL * jj + d, d % 2)
        return carry

    lax.fori_loop(0, i // ATTN_UNROLL, body, 0)
    rest = i % ATTN_UNROLL

    @pl.when(rest >= 2)
    def _():
        advance(i - rest, 0)
        advance(i - rest + 1, 1)

    @pl.when(rest % 2 == 0)
    def _():
        diagonal(0)

    @pl.when(rest % 2 == 1)
    def _():
        advance(i - 1, 0)
        diagonal(1)

    o = acc_ref[...] / l_ref[...]
    lp = lam_ref[...]
    lam = (jnp.exp(jnp.sum(lp[0:1] * lp[1:2], axis=1, keepdims=True))
           - jnp.exp(jnp.sum(lp[2:3] * lp[3:4], axis=1, keepdims=True)) + lambda_init)
    o = o[:, :t] - lam * o[:, t:]
    o = o * lax.rsqrt(jnp.mean(o * o, axis=0, keepdims=True) + EPS) * sg_ref[...]
    o_ref[...] = (o * (1.0 - lambda_init)).T.astype(o_ref.dtype)


def _diff_attention(qt, k, vt, lam_params, subln_g, lambda_init):
    b, s, _ = k.shape
    hd = 2 * DIFF_HEAD_DIM
    t = ATTN_ROWS
    tq = ATTN_TILES_PER_STEP * t
    assert s % tq == 0 and vt.shape[-1] == t and (2 * t) % ATTN_COL_GROUP == 0
    return pl.pallas_call(
        functools.partial(_diff_attn_kernel, lambda_init=lambda_init, t=t),
        grid=(b, DIFF_HEADS, s // tq),
        in_specs=[pl.BlockSpec((1, hd, tq), lambda bi, h, i: (bi, h, i)),
                  pl.BlockSpec((1, s, hd), lambda bi, h, i: (bi, 0, h)),
                  pl.BlockSpec((1, s // t, hd, t), lambda bi, h, i: (bi, 0, h, 0)),
                  pl.BlockSpec(lam_params.shape, lambda bi, h, i: (0, 0)),
                  pl.BlockSpec((hd, 1), lambda bi, h, i: (0, 0))],
        out_specs=pl.BlockSpec((1, tq, hd), lambda bi, h, i: (bi, i, h)),
        out_shape=jax.ShapeDtypeStruct((b, s, DIFF_HEADS * hd), BF16),
        scratch_shapes=[pltpu.VMEM((1, 2 * t), F32), pltpu.VMEM((1, 2 * t), F32),
                        pltpu.VMEM((hd, 2 * t), F32),
                        pltpu.VMEM((t, 2 * t), F32), pltpu.VMEM((t, 2 * t), F32),
                        pltpu.VMEM((1, 2 * t), F32), pltpu.VMEM((1, 2 * t), F32)],
        compiler_params=_params("arbitrary", "arbitrary", "arbitrary"),
        name="diff_attention",
    )(qt, k, vt, lam_params, subln_g.reshape(hd, 1))


def _odd_in_kernel(x_ref, mod_ref, g_ref, w_ref, wgate_ref, q_ref, k_ref, v_ref, o_ref, gates_ref, *, sub):
    h = _modnorm(x_ref[0], g_ref[...], mod_ref[0, 0], sub).astype(BF16)
    p = _dot(h, w_ref[...])
    nq = q_ref.shape[-1]
    nv = v_ref.shape[-1]
    q_ref[0] = p[:, :nq].astype(BF16)
    k_ref[0] = (p[:, nq:2 * nq] * (MLSTM_QK_DIM ** -0.5)).astype(BF16)
    v_ref[0] = p[:, 2 * nq:2 * nq + nv].astype(BF16)
    o_ref[0] = jax.nn.sigmoid(p[:, 2 * nq + nv:]).astype(BF16)
    gates_ref[0] = _dot(h, wgate_ref[...]).T[:gates_ref.shape[1]]


def _odd_in_proj(x, mod, layer, g, w_in, nq, nv):
    b, s, d = x.shape
    n_main = 2 * nq + 2 * nv
    n_gates = w_in.shape[1] - n_main
    w_gate = jnp.pad(w_in[:, n_main:], ((0, 0), (0, LANES - n_gates)))
    tm = PROJ_ROWS
    row = lambda bi, i: (bi, i, 0)
    return pl.pallas_call(
        functools.partial(_odd_in_kernel, sub=1),
        grid=(b, s // tm),
        in_specs=[pl.BlockSpec((1, tm, d), row),
                  pl.BlockSpec((1, 1, 3 * N_SUB, d), lambda bi, i: (layer, bi, 0, 0)),
                  pl.BlockSpec((1, d), lambda bi, i: (0, 0)),
                  _resident((d, n_main), lambda bi, i: (0, 0)),
                  _resident((d, LANES), lambda bi, i: (0, 0))],
        out_specs=[pl.BlockSpec((1, tm, nq), row),
                   pl.BlockSpec((1, tm, nq), row),
                   pl.BlockSpec((1, tm, nv), row),
                   pl.BlockSpec((1, tm, nv), row),
                   pl.BlockSpec((1, n_gates, tm), lambda bi, i: (bi, 0, i))],
        out_shape=[jax.ShapeDtypeStruct((b, s, nq), BF16),
                   jax.ShapeDtypeStruct((b, s, nq), BF16),
                   jax.ShapeDtypeStruct((b, s, nv), BF16),
                   jax.ShapeDtypeStruct((b, s, nv), BF16),
                   jax.ShapeDtypeStruct((b, n_gates, s), F32)],
        compiler_params=_params("arbitrary", "arbitrary"),
        name="odd_in_proj",
    )(x, mod, g.reshape(1, d), w_in, w_gate)


def _log_sigmoid(x):
    return jnp.minimum(x, 0.0) - jnp.log1p(jnp.exp(-jnp.abs(x)))


def _lane_scan(x, combine, fill):
    lane = lax.broadcasted_iota(jnp.int32, x.shape, 1)
    shift = 1
    while shift < x.shape[-1]:
        x = combine(x, jnp.where(lane >= shift, pltpu.roll(x, shift, 1), fill))
        shift *= 2
    return x


def _mlstm_kernel(bg_ref, q_ref, k_ref, v_ref, og_ref, ig_ref, fg_ref, ig_next_ref, fg_next_ref,
                  ng_ref, y_ref, c_ref, m_ref, cols_ref, src_ref, w_ref, sold_ref, *, chunks):
    head = pl.program_id(1)
    step = pl.program_id(2)
    L = MLSTM_CHUNK
    G = chunks

    def store_gate_stats(i_ref, f_ref):
        i_all = i_ref[0, 0] + bg_ref[head]
        logf = _log_sigmoid(f_ref[0, 0] + bg_ref[MLSTM_HEADS + head])
        b_all = _lane_scan(logf, jnp.add, 0.0)
        b_last = b_all[:, L - 1:L]
        a_all = b_last - b_all + i_all
        m_loc = jnp.max(a_all, axis=1, keepdims=True)

        m_in = []
        m = m_ref[...]
        for g in range(G):
            m_in.append(m)
            m = jnp.maximum(b_last[g:g + 1] + m, m_loc[g:g + 1])
        m_out = m_in[1:] + [m]
        m_ref[...] = m
        m_prev = jnp.concatenate(m_in, axis=0)
        m_next = jnp.concatenate(m_out, axis=0)
        sold_ref[...] = jnp.exp(b_last + m_prev - m_next)
        w_ref[...] = jnp.exp(a_all - m_next)

        src = i_all - b_all
        src_ref[...] = src
        inter_log = b_all + m_prev
        m_t = jnp.maximum(inter_log, b_all + _lane_scan(src, jnp.maximum, -jnp.inf))
        per_row = jnp.concatenate([b_all - m_t, jnp.exp(inter_log - m_t), jnp.exp(-m_t)], axis=0)
        per_row = jnp.concatenate([per_row, jnp.zeros((LANES - 3 * G, L), F32)], axis=0)
        cols_ref[...] = per_row.T

    @pl.when(step == 0)
    def _():
        c_ref[...] = jnp.zeros_like(c_ref)
        m_ref[...] = jnp.zeros_like(m_ref)
        store_gate_stats(ig_ref, fg_ref)

    cols = cols_ref[...]
    src = src_ref[...]
    w_all = w_ref[...]
    s_old = sold_ref[...]
    store_gate_stats(ig_next_ref, fg_next_ref)

    causal = (lax.broadcasted_iota(jnp.int32, (L, L), 1) <= lax.broadcasted_iota(jnp.int32, (L, L), 0))
    dv = v_ref.shape[-1]
    ones_col = (lax.broadcasted_iota(jnp.int32, (L, LANES), 1) == 0).astype(BF16)
    state = c_ref[...]
    for g in range(G):
        rows = slice(g * L, (g + 1) * L)
        q = q_ref[0, rows, :]
        k = k_ref[0, rows, :]
        v_ext = jnp.concatenate([v_ref[0, rows, :], ones_col], axis=1)
        dest = cols[:, g:g + 1]
        inter_w = cols[:, G + g:G + g + 1]
        floor = cols[:, 2 * G + g:2 * G + g + 1]

        dw = jnp.exp(jnp.where(causal, dest + src[g:g + 1], -jnp.inf))
        s = lax.dot_general(q, k, (((1,), (1,)), ((), ())), preferred_element_type=F32) * dw
        q_w = (inter_w * q.astype(F32)).astype(BF16)
        nd = _dot(s.astype(BF16), v_ext) + _dot(q_w, state.astype(BF16))
        num = nd[:, :dv]
        den = nd[:, dv:dv + 1]
        r = 1.0 / jnp.maximum(jnp.abs(den), floor)
        scale = r * lax.rsqrt(r * r * jnp.mean(num * num, axis=-1, keepdims=True) + EPS)
        hn = num * scale * ng_ref[...]
        y_ref[0, rows, :] = (og_ref[0, rows, :].astype(F32) * hn).astype(y_ref.dtype)

        kw_t = (k.astype(F32).T * w_all[g:g + 1]).astype(BF16)
        state = s_old[g:g + 1] * state + _dot(kw_t, v_ext)
    c_ref[...] = state


def _mlstm(q, k, v, o, gates, b_gates, norm_g):
    b, s, _ = q.shape
    dk, dv, L = MLSTM_QK_DIM, MLSTM_V_DIM, MLSTM_CHUNK
    nc = s // L
    g = MLSTM_CHUNKS_PER_STEP
    rows = g * L
    gates = gates.reshape(b, 2 * MLSTM_HEADS, nc, L)
    last = nc // g - 1

    def gate_spec(first, ahead):
        return pl.BlockSpec((1, 1, g, L),
                            lambda bi, h, c: (bi, first + h, jnp.minimum(c + ahead, last), 0))

    return pl.pallas_call(
        functools.partial(_mlstm_kernel, chunks=g),
        grid=(b, MLSTM_HEADS, nc // g),
        in_specs=[pl.BlockSpec(memory_space=pltpu.SMEM),
                  pl.BlockSpec((1, rows, dk), lambda bi, h, c: (bi, c, h)),
                  pl.BlockSpec((1, rows, dk), lambda bi, h, c: (bi, c, h)),
                  pl.BlockSpec((1, rows, dv), lambda bi, h, c: (bi, c, h)),
                  pl.BlockSpec((1, rows, dv), lambda bi, h, c: (bi, c, h)),
                  gate_spec(0, 0), gate_spec(MLSTM_HEADS, 0),
                  gate_spec(0, 1), gate_spec(MLSTM_HEADS, 1),
                  pl.BlockSpec((1, dv), lambda bi, h, c: (0, h))],
        out_specs=pl.BlockSpec((1, rows, dv), lambda bi, h, c: (bi, c, h)),
        out_shape=jax.ShapeDtypeStruct((b, s, MLSTM_HEADS * dv), BF16),
        scratch_shapes=[pltpu.VMEM((dk, dv + LANES), F32), pltpu.VMEM((1, 1), F32),
                        pltpu.VMEM((L, LANES), F32), pltpu.VMEM((g, L), F32), pltpu.VMEM((g, L), F32),
                        pltpu.VMEM((g, 1), F32)],
        compiler_params=_params("arbitrary", "arbitrary", "arbitrary"),
        name="mlstm",
    )(b_gates.reshape(-1), q, k, v, o, gates, gates, gates, gates, norm_g.reshape(1, -1))


def kernel(x, c, w_mod, b_mod, norm_g, w_ffn_gate, w_ffn_up, w_ffn_down, w_in_even, w_pool, pool_scale, diff_lambda, diff_subln_g, w_out_even, w_in_odd, b_gates_odd, mlstm_norm_g, w_out_odd, final_g):
    depth = w_mod.shape[0]
    mod = _modulation(c, w_mod, b_mod)
    pool_width = pool_scale.shape[-1]
    qk_width = DIFF_HEADS * 2 * DIFF_HEAD_DIM
    ffn_stacked = (w_ffn_gate, w_ffn_up, w_ffn_down)
    ffn_w = tuple(w[0, 0].astype(BF16) for w in ffn_stacked)
    for l in range(depth):
        casts = [(w, (l, 1)) for w in ffn_stacked]
        if l % 2 == 1:
            casts.append((w_in_odd, (l // 2,)))
        x, cast = _ffn(x, mod, l, 0, norm_g[l, 0], ffn_w, final_g, False, casts=casts)
        ffn_w = cast[:3]
        if l % 2 == 0:
            e = l // 2
            lambda_init = 0.8 - 0.6 * math.exp(-0.3 * l)
            u, k, qt, vt = _even_in_proj(x, mod, l, norm_g[l, 1], w_in_even[e], pool_width, qk_width)
            y_diff = _diff_attention(qt, k, vt, diff_lambda[e], diff_subln_g[e], lambda_init)
            mix = ("even", u, y_diff, w_pool[e], pool_scale[e], w_out_even[e])
        else:
            o = l // 2
            q, k, v, og, gates = _odd_in_proj(x, mod, l, norm_g[l, 1], cast[3],
                                              MLSTM_HEADS * MLSTM_QK_DIM, MLSTM_HEADS * MLSTM_V_DIM)
            y = _mlstm(q, k, v, og, gates, b_gates_odd[o], mlstm_norm_g[o])
            mix = ("odd", y, w_out_odd[o])
        last = l == depth - 1
        x, cast = _ffn(x, mod, l, 2, norm_g[l, 2], ffn_w, final_g, last, mix,
                       casts=() if last else [(w, (l + 1, 0)) for w in ffn_stacked])
        ffn_w = cast[:3]
    return x
```

```python
import functools
import math

import jax
import jax.numpy as jnp
from jax import lax
from jax.experimental import pallas as pl
from jax.experimental.pallas import tpu as pltpu

F32 = jnp.float32
BF16 = jnp.bfloat16

EPS = 1e-6
N_SUB = 3
POOL_WINDOWS = (2, 4, 8, 16)
POOL_HALO = 16
DIFF_HEADS = 4
DIFF_HEAD_DIM = 64
MLSTM_HEADS = 4
MLSTM_QK_DIM = 128
MLSTM_V_DIM = 256
MLSTM_CHUNK = 128
LANES = 128

VMEM_LIMIT_BYTES = 56 * 1024 * 1024

MOD_ROWS = 256
FFN_ROWS = 1024
FFN_MIX_ROWS = 512
FFN_CAST_CHUNKS = 16
FFN_SUB_ROWS = 256
PROJ_ROWS = 1024
ATTN_ROWS = 512
ATTN_COL_GROUP = 256
ATTN_UNROLL = 4
LOG2_E = math.log2(math.e)
MLSTM_CHUNKS_PER_STEP = 32


def _params(*semantics):
    return pltpu.CompilerParams(dimension_semantics=semantics,
                                vmem_limit_bytes=VMEM_LIMIT_BYTES)


def _resident(block_shape, index_map):
    return pl.BlockSpec(block_shape, index_map, pipeline_mode=pl.Buffered(1))


def _dot(a, b):
    return jnp.dot(a, b, preferred_element_type=F32)


def _modnorm(x, g, mod, sub):
    shift = mod[3 * sub:3 * sub + 1]
    scale = mod[3 * sub + 1:3 * sub + 2]
    y = x * lax.rsqrt(jnp.mean(x * x, axis=-1, keepdims=True) + EPS)
    return (y * g) * (1.0 + scale) + shift


def _gate(mod, sub):
    return mod[3 * sub + 2:3 * sub + 3]


def _mod_kernel(c_ref, w_ref, b_ref, o_ref):
    @pl.when(pl.program_id(1) == 0)
    def _():
        o_ref[0] = jnp.broadcast_to(b_ref[0], o_ref.shape[1:])

    c = c_ref[...]
    c_act = c * jax.nn.sigmoid(c)
    w = w_ref[0]
    c_hi = c_act.astype(BF16)
    c_lo = (c_act - c_hi.astype(F32)).astype(BF16)
    w_hi = w.astype(BF16)
    w_lo = (w - w_hi.astype(F32)).astype(BF16)
    o_ref[0] += _dot(c_hi, w_hi) + (_dot(c_hi, w_lo) + _dot(c_lo, w_hi))


def _modulation(c, w_mod, b_mod):
    depth, d, n = w_mod.shape
    b = c.shape[0]
    tk = MOD_ROWS
    out = pl.pallas_call(
        _mod_kernel,
        grid=(depth, d // tk),
        in_specs=[pl.BlockSpec((b, tk), lambda l, j: (0, j)),
                  pl.BlockSpec((1, tk, n), lambda l, j: (l, j, 0)),
                  pl.BlockSpec((1, 1, n), lambda l, j: (l, 0, 0))],
        out_specs=pl.BlockSpec((1, b, n), lambda l, j: (l, 0, 0)),
        out_shape=jax.ShapeDtypeStruct((depth, b, n), F32),
        compiler_params=_params("arbitrary", "arbitrary"),
        name="modulation",
    )(c, w_mod, b_mod.reshape(depth, 1, n))
    return out.reshape(depth, b, 3 * N_SUB, d)


def _swiglu_residual(x, mod, sub, g, wg_ref, wu_ref, wd_ref, fg, final_norm):
    outs = []
    for r in range(0, x.shape[0], FFN_SUB_ROWS):
        xs = x[r:r + FFN_SUB_ROWS]
        h = _modnorm(xs, g, mod, sub).astype(BF16)
        gate_act = _dot(h, wg_ref[...])
        up = _dot(h, wu_ref[...])
        a = (gate_act * jax.nn.sigmoid(gate_act) * up).astype(BF16)
        out = xs + (0.5 * _gate(mod, sub)) * _dot(a, wd_ref[...])
        if final_norm:
            out = out * lax.rsqrt(jnp.mean(out * out, axis=-1, keepdims=True) + EPS) * fg
        outs.append(out)
    return jnp.concatenate(outs, axis=0)


def _pool_mixer(u, halo, first_row, wp_ref, scale):
    tm = u.shape[0]
    ext = jnp.concatenate([halo, u], axis=0)
    pos = first_row + lax.broadcasted_iota(jnp.int32, (tm, 1), 0)
    gw = wp_ref.shape[-1]
    sums = ext
    shift = 1
    parts = []
    for g, w in enumerate(POOL_WINDOWS):
        while shift < w:
            sums = sums + pltpu.roll(sums, shift, 0)
            shift *= 2
        total = sums[POOL_HALO:, g * gw:(g + 1) * gw]
        count = jnp.minimum(pos + 1, w).astype(F32)
        pooled = total / count - u[:, g * gw:(g + 1) * gw]
        parts.append(_dot(pooled.astype(BF16), wp_ref[g]))
    return (jnp.concatenate(parts, axis=1) * scale).astype(BF16)


def _mixed_input(mix, x_ref, mod, mix_refs):
    if mix is None:
        return x_ref[0]
    if mix == "odd":
        y_ref, wo_ref = mix_refs
        return x_ref[0] + _gate(mod, 1) * _dot(y_ref[0], wo_ref[...])
    u_ref, halo_ref, yd_ref, wp_ref, ps_ref, wo_ref = mix_refs
    i = pl.program_id(1)
    tm = u_ref.shape[1]
    halo = halo_ref[0] * (i > 0).astype(F32)
    y_pool = _pool_mixer(u_ref[0], halo, i * tm, wp_ref, ps_ref[...])
    pw = y_pool.shape[1]
    y = _dot(y_pool, wo_ref[:pw, :]) + _dot(yd_ref[0], wo_ref[pw:, :])
    return x_ref[0] + _gate(mod, 1) * y


_N_MIX_REFS = {None: 0, "odd": 2, "even": 6}


def _ffn_kernel(*refs, sub, final_norm, mix, n_cast):
    x_ref, mod_ref, g_ref, wg_ref, wu_ref, wd_ref, fg_ref = refs[:7]
    n_mix = _N_MIX_REFS[mix]
    mix_refs = refs[7:7 + n_mix]
    cast_in = refs[7 + n_mix:7 + n_mix + n_cast]
    o_ref = refs[7 + n_mix + n_cast]
    cast_out = refs[8 + n_mix + n_cast:]
    mod = mod_ref[0, 0]
    x = _mixed_input(mix, x_ref, mod, mix_refs)
    o_ref[0] = _swiglu_residual(x, mod, sub, g_ref[...], wg_ref, wu_ref, wd_ref, fg_ref[...], final_norm)
    for src_ref, dst_ref in zip(cast_in, cast_out):
        dst_ref[...] = src_ref[(0,) * (len(src_ref.shape) - 2)].astype(BF16)


def _ffn(x, mod, layer, sub, g, weights, final_g, final_norm, mix=None, casts=()):
    b, s, d = x.shape
    f = weights[0].shape[-1]
    tm = FFN_ROWS if mix is None else FFN_MIX_ROWS
    n_i = s // tm
    row = lambda bi, i: (bi, i, 0)
    const2 = lambda bi, i: (0, 0)
    in_specs = [pl.BlockSpec((1, tm, d), row),
                pl.BlockSpec((1, 1, 3 * N_SUB, d), lambda bi, i: (layer, bi, 0, 0)),
                pl.BlockSpec((1, d), const2),
                _resident((d, f), const2),
                _resident((d, f), const2),
                _resident((f, d), const2),
                pl.BlockSpec((1, d), const2)]
    args = [x, mod, g.reshape(1, d), *weights, final_g.reshape(1, d)]
    kind = None if mix is None else mix[0]
    if kind == "odd":
        _, y, w_out = mix
        in_specs += [pl.BlockSpec((1, tm, y.shape[-1]), row), _resident(w_out.shape, const2)]
        args += [y, w_out.astype(BF16)]
    elif kind == "even":
        _, u, y_diff, w_pool, pool_scale, w_out = mix
        pw = u.shape[-1]
        halo_blocks = tm // POOL_HALO
        in_specs += [pl.BlockSpec((1, tm, pw), row),
                     pl.BlockSpec((1, POOL_HALO, pw),
                                  lambda bi, i: (bi, jnp.maximum(i * halo_blocks - 1, 0), 0)),
                     pl.BlockSpec((1, tm, y_diff.shape[-1]), row),
                     _resident(w_pool.shape, lambda bi, i: (0, 0, 0)),
                     pl.BlockSpec((1, pw), const2),
                     _resident(w_out.shape, const2)]
        args += [u, u, y_diff, w_pool.astype(BF16), pool_scale.reshape(1, pw), w_out.astype(BF16)]
    out_specs = [pl.BlockSpec((1, tm, d), row)]
    out_shape = [jax.ShapeDtypeStruct((b, s, d), F32)]
    steps = b * n_i
    assert steps % FFN_CAST_CHUNKS == 0
    chunk_of = lambda bi, i: (bi * n_i + i) // (steps // FFN_CAST_CHUNKS)
    for w, lead in casts:
        rows, cols = w.shape[-2:]
        assert rows % (FFN_CAST_CHUNKS * 16) == 0
        chunk = rows // FFN_CAST_CHUNKS
        in_specs.append(pl.BlockSpec((1,) * len(lead) + (chunk, cols),
                                     lambda bi, i, lead=lead: (*lead, chunk_of(bi, i), 0)))
        args.append(w)
        out_specs.append(pl.BlockSpec((chunk, cols), lambda bi, i: (chunk_of(bi, i), 0)))
        out_shape.append(jax.ShapeDtypeStruct((rows, cols), BF16))
    outs = pl.pallas_call(
        functools.partial(_ffn_kernel, sub=sub, final_norm=final_norm, mix=kind,
                          n_cast=len(casts)),
        grid=(b, n_i),
        in_specs=in_specs,
        out_specs=out_specs,
        out_shape=out_shape,
        compiler_params=_params("arbitrary", "arbitrary"),
        name="ffn" if kind is None else kind + "_mix_ffn",
    )(*args)
    return outs[0], tuple(outs[1:])


def _even_in_kernel(x_ref, mod_ref, g_ref, w_ref, wt_ref, u_ref, k_ref, qt_ref, vt_ref, *, sub):
    h = _modnorm(x_ref[0], g_ref[...], mod_ref[0, 0], sub).astype(BF16)
    p = _dot(h, w_ref[...])
    pw = u_ref.shape[-1]
    u_ref[0] = p[:, :pw]
    k_ref[0] = p[:, pw:].astype(BF16)
    pt = lax.dot_general(wt_ref[...], h, (((1,), (1,)), ((), ())), preferred_element_type=F32)
    qw = qt_ref.shape[2]
    qt_ref[0, 0] = (pt[:qw] * (DIFF_HEAD_DIM ** -0.5 * LOG2_E)).astype(BF16)
    vt_ref[0, 0] = pt[qw:].astype(BF16)


def _even_in_proj(x, mod, layer, g, w_in, pool_width, qk_width):
    b, s, d = x.shape
    n = w_in.shape[1]
    v_width = n - pool_width - 2 * qk_width
    tm = ATTN_ROWS
    k0 = pool_width + qk_width
    w_uk = jnp.concatenate([w_in[:, :pool_width], w_in[:, k0:k0 + qk_width]], axis=1).astype(BF16)
    w_qv_t = jnp.concatenate([w_in[:, pool_width:k0], w_in[:, k0 + qk_width:]], axis=1).T.astype(BF16)
    row = lambda bi, i: (bi, i, 0)
    return pl.pallas_call(
        functools.partial(_even_in_kernel, sub=1),
        grid=(b, s // tm),
        in_specs=[pl.BlockSpec((1, tm, d), row),
                  pl.BlockSpec((1, 1, 3 * N_SUB, d), lambda bi, i: (layer, bi, 0, 0)),
                  pl.BlockSpec((1, d), lambda bi, i: (0, 0)),
                  _resident(w_uk.shape, lambda bi, i: (0, 0)),
                  _resident(w_qv_t.shape, lambda bi, i: (0, 0))],
        out_specs=[pl.BlockSpec((1, tm, pool_width), row),
                   pl.BlockSpec((1, tm, qk_width), row),
                   pl.BlockSpec((1, 1, qk_width, tm), lambda bi, i: (bi, i, 0, 0)),
                   pl.BlockSpec((1, 1, v_width, tm), lambda bi, i: (bi, i, 0, 0))],
        out_shape=[jax.ShapeDtypeStruct((b, s, pool_width), F32),
                   jax.ShapeDtypeStruct((b, s, qk_width), BF16),
                   jax.ShapeDtypeStruct((b, s // tm, qk_width, tm), BF16),
                   jax.ShapeDtypeStruct((b, s // tm, v_width, tm), BF16)],
        compiler_params=_params("arbitrary", "arbitrary"),
        name="even_in_proj",
    )(x, mod, g.reshape(1, d), w_uk, w_qv_t)


def _diff_attn_kernel(qt_ref, k_ref, vt_ref, lam_ref, sg_ref, o_ref, m_ref, l_ref, acc_ref,
                      s0_ref, s1_ref, cm0_ref, cm1_ref, *, lambda_init, t):
    def query_tile(i, carry):
        rows = pl.ds(pl.multiple_of(i * t, t), t)
        _diff_attn_query_tile(i, qt_ref[0, i], k_ref, vt_ref, lam_ref, sg_ref, o_ref.at[0, rows, :],
                              m_ref, l_ref, acc_ref, (s0_ref, s1_ref), (cm0_ref, cm1_ref),
                              lambda_init, t)
        return carry

    lax.fori_loop(0, qt_ref.shape[1], query_tile, 0)


def _diff_attn_query_tile(i, qt, k_ref, vt_ref, lam_ref, sg_ref, o_ref, m_ref, l_ref, acc_ref, s_refs,
                          cm_refs, lambda_init, t):
    qt = qt.astype(F32)
    feat = lax.broadcasted_iota(jnp.int32, qt.shape, 0)
    qs = jnp.concatenate([jnp.where(feat < DIFF_HEAD_DIM, qt, 0.0),
                          jnp.where(feat >= DIFF_HEAD_DIM, qt, 0.0)], axis=1).astype(BF16)
    m_ref[...] = jnp.full_like(m_ref, -jnp.inf)
    l_ref[...] = jnp.zeros_like(l_ref)
    acc_ref[...] = jnp.zeros_like(acc_ref)
    groups = [slice(c, c + ATTN_COL_GROUP) for c in range(0, 2 * t, ATTN_COL_GROUP)]

    def produce(j, slot, cols):
        kb = k_ref[0, pl.ds(pl.multiple_of(j * t, t), t), :]
        s = _dot(kb, qs[:, cols])
        s_refs[slot][:, cols] = s
        cm_refs[slot][:, cols] = jnp.max(s, axis=0, keepdims=True)

    def consume(j, slot, cols, diagonal):
        if diagonal:
            first_q = cols.start % t
            n_keys = first_q + ATTN_COL_GROUP
            s = s_refs[slot][:n_keys, cols]
            key = lax.broadcasted_iota(jnp.int32, s.shape, 0)
            qry = lax.broadcasted_iota(jnp.int32, s.shape, 1) + first_q
            s = jnp.where(key <= qry, s, -jnp.inf)
            cm = jnp.max(s, axis=0, keepdims=True)
            vt = vt_ref[0, j][:, :n_keys]
        else:
            s = s_refs[slot][:, cols]
            cm = cm_refs[slot][:, cols]
            vt = vt_ref[0, j]
        m = m_ref[:, cols]
        m_new = jnp.maximum(m, cm)
        alpha = jnp.exp2(m - m_new)
        p = jnp.exp2(s - m_new)
        l_ref[:, cols] = alpha * l_ref[:, cols] + jnp.sum(p, axis=0, keepdims=True)
        acc_ref[:, cols] = alpha * acc_ref[:, cols] + _dot(vt, p.astype(BF16))
        m_ref[:, cols] = m_new

    def advance(j, slot):
        for cols in groups:
            produce(j + 1, 1 - slot, cols)
            consume(j, slot, cols, False)

    def diagonal(slot):
        for cols in groups:
            consume(i, slot, cols, True)

    for cols in groups:
        produce(0, 0, cols)

    def body(jj, carry):
        for d in range(ATTN_UNROLL):
            advance(ATTN_UNROLL * jj + d, d % 2)
        return carry

    lax.fori_loop(0, i // ATTN_UNROLL, body, 0)
    rest = i % ATTN_UNROLL

    @pl.when(rest >= 2)
    def _():
        advance(i - rest, 0)
        advance(i - rest + 1, 1)

    @pl.when(rest % 2 == 0)
    def _():
        diagonal(0)

    @pl.when(rest % 2 == 1)
    def _():
        advance(i - 1, 0)
        diagonal(1)

    o = acc_ref[...] / l_ref[...]
    lp = lam_ref[...]
    lam = (jnp.exp(jnp.sum(lp[0:1] * lp[1:2], axis=1, keepdims=True))
           - jnp.exp(jnp.sum(lp[2:3] * lp[3:4], axis=1, keepdims=True)) + lambda_init)
    o = o[:, :t] - lam * o[:, t:]
    o = o * lax.rsqrt(jnp.mean(o * o, axis=0, keepdims=True) + EPS) * sg_ref[...]
    o_ref[...] = (o * (1.0 - lambda_init)).T.astype(o_ref.dtype)


def _diff_attention(qt, k, vt, lam_params, subln_g, lambda_init):
    b, s, _ = k.shape
    hd = 2 * DIFF_HEAD_DIM
    t = ATTN_ROWS
    assert s % t == 0 and vt.shape[-1] == t and (2 * t) % ATTN_COL_GROUP == 0
    return pl.pallas_call(
        functools.partial(_diff_attn_kernel, lambda_init=lambda_init, t=t),
        grid=(b, DIFF_HEADS),
        in_specs=[pl.BlockSpec((1, s // t, hd, t), lambda bi, h: (bi, 0, h, 0)),
                  pl.BlockSpec((1, s, hd), lambda bi, h: (bi, 0, h)),
                  pl.BlockSpec((1, s // t, hd, t), lambda bi, h: (bi, 0, h, 0)),
                  pl.BlockSpec(lam_params.shape, lambda bi, h: (0, 0)),
                  pl.BlockSpec((hd, 1), lambda bi, h: (0, 0))],
        out_specs=pl.BlockSpec((1, s, hd), lambda bi, h: (bi, 0, h)),
        out_shape=jax.ShapeDtypeStruct((b, s, DIFF_HEADS * hd), BF16),
        scratch_shapes=[pltpu.VMEM((1, 2 * t), F32), pltpu.VMEM((1, 2 * t), F32),
                        pltpu.VMEM((hd, 2 * t), F32),
                        pltpu.VMEM((t, 2 * t), F32), pltpu.VMEM((t, 2 * t), F32),
                        pltpu.VMEM((1, 2 * t), F32), pltpu.VMEM((1, 2 * t), F32)],
        compiler_params=_params("arbitrary", "arbitrary"),
        name="diff_attention",
    )(qt, k, vt, lam_params, subln_g.reshape(hd, 1))


def _odd_in_kernel(x_ref, mod_ref, g_ref, w_ref, wgate_ref, q_ref, k_ref, v_ref, o_ref, gates_ref, *, sub):
    h = _modnorm(x_ref[0], g_ref[...], mod_ref[0, 0], sub).astype(BF16)
    p = _dot(h, w_ref[...])
    nq = q_ref.shape[-1]
    nv = v_ref.shape[-1]
    q_ref[0] = p[:, :nq].astype(BF16)
    k_ref[0] = (p[:, nq:2 * nq] * (MLSTM_QK_DIM ** -0.5)).astype(BF16)
    v_ref[0] = p[:, 2 * nq:2 * nq + nv].astype(BF16)
    o_ref[0] = jax.nn.sigmoid(p[:, 2 * nq + nv:]).astype(BF16)
    gates_ref[0] = _dot(h, wgate_ref[...]).T[:gates_ref.shape[1]]


def _odd_in_proj(x, mod, layer, g, w_in, nq, nv):
    b, s, d = x.shape
    n_main = 2 * nq + 2 * nv
    n_gates = w_in.shape[1] - n_main
    w_gate = jnp.pad(w_in[:, n_main:], ((0, 0), (0, LANES - n_gates)))
    tm = PROJ_ROWS
    row = lambda bi, i: (bi, i, 0)
    return pl.pallas_call(
        functools.partial(_odd_in_kernel, sub=1),
        grid=(b, s // tm),
        in_specs=[pl.BlockSpec((1, tm, d), row),
                  pl.BlockSpec((1, 1, 3 * N_SUB, d), lambda bi, i: (layer, bi, 0, 0)),
                  pl.BlockSpec((1, d), lambda bi, i: (0, 0)),
                  _resident((d, n_main), lambda bi, i: (0, 0)),
                  _resident((d, LANES), lambda bi, i: (0, 0))],
        out_specs=[pl.BlockSpec((1, tm, nq), row),
                   pl.BlockSpec((1, tm, nq), row),
                   pl.BlockSpec((1, tm, nv), row),
                   pl.BlockSpec((1, tm, nv), row),
                   pl.BlockSpec((1, n_gates, tm), lambda bi, i: (bi, 0, i))],
        out_shape=[jax.ShapeDtypeStruct((b, s, nq), BF16),
                   jax.ShapeDtypeStruct((b, s, nq), BF16),
                   jax.ShapeDtypeStruct((b, s, nv), BF16),
                   jax.ShapeDtypeStruct((b, s, nv), BF16),
                   jax.ShapeDtypeStruct((b, n_gates, s), F32)],
        compiler_params=_params("arbitrary", "arbitrary"),
        name="odd_in_proj",
    )(x, mod, g.reshape(1, d), w_in, w_gate)


def _log_sigmoid(x):
    return jnp.minimum(x, 0.0) - jnp.log1p(jnp.exp(-jnp.abs(x)))


def _lane_scan(x, combine, fill):
    lane = lax.broadcasted_iota(jnp.int32, x.shape, 1)
    shift = 1
    while shift < x.shape[-1]:
        x = combine(x, jnp.where(lane >= shift, pltpu.roll(x, shift, 1), fill))
        shift *= 2
    return x


def _mlstm_kernel(bg_ref, q_ref, k_ref, v_ref, og_ref, ig_ref, fg_ref, ig_next_ref, fg_next_ref,
                  ng_ref, y_ref, c_ref, m_ref, cols_ref, src_ref, w_ref, sold_ref, *, chunks):
    head = pl.program_id(1)
    step = pl.program_id(2)
    L = MLSTM_CHUNK
    G = chunks

    def store_gate_stats(i_ref, f_ref):
        i_all = i_ref[0, 0] + bg_ref[head]
        logf = _log_sigmoid(f_ref[0, 0] + bg_ref[MLSTM_HEADS + head])
        b_all = _lane_scan(logf, jnp.add, 0.0)
        b_last = b_all[:, L - 1:L]
        a_all = b_last - b_all + i_all
        m_loc = jnp.max(a_all, axis=1, keepdims=True)

        m_in = []
        m = m_ref[...]
        for g in range(G):
            m_in.append(m)
            m = jnp.maximum(b_last[g:g + 1] + m, m_loc[g:g + 1])
        m_out = m_in[1:] + [m]
        m_ref[...] = m
        m_prev = jnp.concatenate(m_in, axis=0)
        m_next = jnp.concatenate(m_out, axis=0)
        sold_ref[...] = jnp.exp(b_last + m_prev - m_next)
        w_ref[...] = jnp.exp(a_all - m_next)

        src = i_all - b_all
        src_ref[...] = src
        inter_log = b_all + m_prev
        m_t = jnp.maximum(inter_log, b_all + _lane_scan(src, jnp.maximum, -jnp.inf))
        per_row = jnp.concatenate([b_all - m_t, jnp.exp(inter_log - m_t), jnp.exp(-m_t)], axis=0)
        per_row = jnp.concatenate([per_row, jnp.zeros((LANES - 3 * G, L), F32)], axis=0)
        cols_ref[...] = per_row.T

    @pl.when(step == 0)
    def _():
        c_ref[...] = jnp.zeros_like(c_ref)
        m_ref[...] = jnp.zeros_like(m_ref)
        store_gate_stats(ig_ref, fg_ref)

    cols = cols_ref[...]
    src = src_ref[...]
    w_all = w_ref[...]
    s_old = sold_ref[...]
    store_gate_stats(ig_next_ref, fg_next_ref)

    causal = (lax.broadcasted_iota(jnp.int32, (L, L), 1) <= lax.broadcasted_iota(jnp.int32, (L, L), 0))
    dv = v_ref.shape[-1]
    ones_col = (lax.broadcasted_iota(jnp.int32, (L, LANES), 1) == 0).astype(BF16)
    state = c_ref[...]
    for g in range(G):
        rows = slice(g * L, (g + 1) * L)
        q = q_ref[0, rows, :]
        k = k_ref[0, rows, :]
        v_ext = jnp.concatenate([v_ref[0, rows, :], ones_col], axis=1)
        dest = cols[:, g:g + 1]
        inter_w = cols[:, G + g:G + g + 1]
        floor = cols[:, 2 * G + g:2 * G + g + 1]

        dw = jnp.exp(jnp.where(causal, dest + src[g:g + 1], -jnp.inf))
        s = lax.dot_general(q, k, (((1,), (1,)), ((), ())), preferred_element_type=F32) * dw
        q_w = (inter_w * q.astype(F32)).astype(BF16)
        nd = _dot(s.astype(BF16), v_ext) + _dot(q_w, state.astype(BF16))
        num = nd[:, :dv]
        den = nd[:, dv:dv + 1]
        r = 1.0 / jnp.maximum(jnp.abs(den), floor)
        scale = r * lax.rsqrt(r * r * jnp.mean(num * num, axis=-1, keepdims=True) + EPS)
        hn = num * scale * ng_ref[...]
        y_ref[0, rows, :] = (og_ref[0, rows, :].astype(F32) * hn).astype(y_ref.dtype)

        kw_t = (k.astype(F32).T * w_all[g:g + 1]).astype(BF16)
        state = s_old[g:g + 1] * state + _dot(kw_t, v_ext)
    c_ref[...] = state


def _mlstm(q, k, v, o, gates, b_gates, norm_g):
    b, s, _ = q.shape
    dk, dv, L = MLSTM_QK_DIM, MLSTM_V_DIM, MLSTM_CHUNK
    nc = s // L
    g = MLSTM_CHUNKS_PER_STEP
    rows = g * L
    gates = gates.reshape(b, 2 * MLSTM_HEADS, nc, L)
    last = nc // g - 1

    def gate_spec(first, ahead):
        return pl.BlockSpec((1, 1, g, L),
                            lambda bi, h, c: (bi, first + h, jnp.minimum(c + ahead, last), 0))

    return pl.pallas_call(
        functools.partial(_mlstm_kernel, chunks=g),
        grid=(b, MLSTM_HEADS, nc // g),
        in_specs=[pl.BlockSpec(memory_space=pltpu.SMEM),
                  pl.BlockSpec((1, rows, dk), lambda bi, h, c: (bi, c, h)),
                  pl.BlockSpec((1, rows, dk), lambda bi, h, c: (bi, c, h)),
                  pl.BlockSpec((1, rows, dv), lambda bi, h, c: (bi, c, h)),
                  pl.BlockSpec((1, rows, dv), lambda bi, h, c: (bi, c, h)),
                  gate_spec(0, 0), gate_spec(MLSTM_HEADS, 0),
                  gate_spec(0, 1), gate_spec(MLSTM_HEADS, 1),
                  pl.BlockSpec((1, dv), lambda bi, h, c: (0, h))],
        out_specs=pl.BlockSpec((1, rows, dv), lambda bi, h, c: (bi, c, h)),
        out_shape=jax.ShapeDtypeStruct((b, s, MLSTM_HEADS * dv), BF16),
        scratch_shapes=[pltpu.VMEM((dk, dv + LANES), F32), pltpu.VMEM((1, 1), F32),
                        pltpu.VMEM((L, LANES), F32), pltpu.VMEM((g, L), F32), pltpu.VMEM((g, L), F32),
                        pltpu.VMEM((g, 1), F32)],
        compiler_params=_params("arbitrary", "arbitrary", "arbitrary"),
        name="mlstm",
    )(b_gates.reshape(-1), q, k, v, o, gates, gates, gates, gates, norm_g.reshape(1, -1))


def kernel(x, c, w_mod, b_mod, norm_g, w_ffn_gate, w_ffn_up, w_ffn_down, w_in_even, w_pool, pool_scale, diff_lambda, diff_subln_g, w_out_even, w_in_odd, b_gates_odd, mlstm_norm_g, w_out_odd, final_g):
    depth = w_mod.shape[0]
    mod = _modulation(c, w_mod, b_mod)
    pool_width = pool_scale.shape[-1]
    qk_width = DIFF_HEADS * 2 * DIFF_HEAD_DIM
    ffn_stacked = (w_ffn_gate, w_ffn_up, w_ffn_down)
    ffn_w = tuple(w[0, 0].astype(BF16) for w in ffn_stacked)
    for l in range(depth):
        casts = [(w, (l, 1)) for w in ffn_stacked]
        if l % 2 == 1:
            casts.append((w_in_odd, (l // 2,)))
        x, cast = _ffn(x, mod, l, 0, norm_g[l, 0], ffn_w, final_g, False, casts=casts)
        ffn_w = cast[:3]
        if l % 2 == 0:
            e = l // 2
            lambda_init = 0.8 - 0.6 * math.exp(-0.3 * l)
            u, k, qt, vt = _even_in_proj(x, mod, l, norm_g[l, 1], w_in_even[e], pool_width, qk_width)
            y_diff = _diff_attention(qt, k, vt, diff_lambda[e], diff_subln_g[e], lambda_init)
            mix = ("even", u, y_diff, w_pool[e], pool_scale[e], w_out_even[e])
        else:
            o = l // 2
            q, k, v, og, gates = _odd_in_proj(x, mod, l, norm_g[l, 1], cast[3],
                                              MLSTM_HEADS * MLSTM_QK_DIM, MLSTM_HEADS * MLSTM_V_DIM)
            y = _mlstm(q, k, v, og, gates, b_gates_odd[o], mlstm_norm_g[o])
            mix = ("odd", y, w_out_odd[o])
        last = l == depth - 1
        x, cast = _ffn(x, mod, l, 2, norm_g[l, 2], ffn_w, final_g, last, mix,
                       casts=() if last else [(w, (l + 1, 0)) for w in ffn_stacked])
        ffn_w = cast[:3]
    return x
```

```python
import functools
import math

import jax
import jax.numpy as jnp
from jax import lax
from jax.experimental import pallas as pl
from jax.experimental.pallas import tpu as pltpu

F32 = jnp.float32
BF16 = jnp.bfloat16

EPS = 1e-6
N_SUB = 3
POOL_WINDOWS = (2, 4, 8, 16)
POOL_HALO = 16
DIFF_HEADS = 4
DIFF_HEAD_DIM = 64
MLSTM_HEADS = 4
MLSTM_QK_DIM = 128
MLSTM_V_DIM = 256
MLSTM_CHUNK = 128
LANES = 128

VMEM_LIMIT_BYTES = 56 * 1024 * 1024

MOD_ROWS = 256
FFN_ROWS = 1024
FFN_MIX_ROWS = 512
FFN_CAST_CHUNKS = 16
FFN_SUB_ROWS = 256
PROJ_ROWS = 1024
ATTN_ROWS = 512
ATTN_COL_GROUP = 256
ATTN_TILES_PER_STEP = 4
ATTN_UNROLL = 4
LOG2_E = math.log2(math.e)
MLSTM_CHUNKS_PER_STEP = 32


def _params(*semantics):
    return pltpu.CompilerParams(dimension_semantics=semantics,
                                vmem_limit_bytes=VMEM_LIMIT_BYTES)


def _resident(block_shape, index_map):
    return pl.BlockSpec(block_shape, index_map, pipeline_mode=pl.Buffered(1))


def _dot(a, b):
    return jnp.dot(a, b, preferred_element_type=F32)


def _modnorm(x, g, mod, sub):
    shift = mod[3 * sub:3 * sub + 1]
    scale = mod[3 * sub + 1:3 * sub + 2]
    y = x * lax.rsqrt(jnp.mean(x * x, axis=-1, keepdims=True) + EPS)
    return (y * g) * (1.0 + scale) + shift


def _gate(mod, sub):
    return mod[3 * sub + 2:3 * sub + 3]


def _mod_kernel(c_ref, w_ref, b_ref, o_ref):
    @pl.when(pl.program_id(1) == 0)
    def _():
        o_ref[0] = jnp.broadcast_to(b_ref[0], o_ref.shape[1:])

    c = c_ref[...]
    c_act = c * jax.nn.sigmoid(c)
    w = w_ref[0]
    c_hi = c_act.astype(BF16)
    c_lo = (c_act - c_hi.astype(F32)).astype(BF16)
    w_hi = w.astype(BF16)
    w_lo = (w - w_hi.astype(F32)).astype(BF16)
    o_ref[0] += _dot(c_hi, w_hi) + (_dot(c_hi, w_lo) + _dot(c_lo, w_hi))


def _modulation(c, w_mod, b_mod):
    depth, d, n = w_mod.shape
    b = c.shape[0]
    tk = MOD_ROWS
    out = pl.pallas_call(
        _mod_kernel,
        grid=(depth, d // tk),
        in_specs=[pl.BlockSpec((b, tk), lambda l, j: (0, j)),
                  pl.BlockSpec((1, tk, n), lambda l, j: (l, j, 0)),
                  pl.BlockSpec((1, 1, n), lambda l, j: (l, 0, 0))],
        out_specs=pl.BlockSpec((1, b, n), lambda l, j: (l, 0, 0)),
        out_shape=jax.ShapeDtypeStruct((depth, b, n), F32),
        compiler_params=_params("arbitrary", "arbitrary"),
        name="modulation",
    )(c, w_mod, b_mod.reshape(depth, 1, n))
    return out.reshape(depth, b, 3 * N_SUB, d)


def _swiglu_residual(x, mod, sub, g, wg_ref, wu_ref, wd_ref, fg, final_norm):
    outs = []
    for r in range(0, x.shape[0], FFN_SUB_ROWS):
        xs = x[r:r + FFN_SUB_ROWS]
        h = _modnorm(xs, g, mod, sub).astype(BF16)
        gate_act = _dot(h, wg_ref[...])
        up = _dot(h, wu_ref[...])
        a = (gate_act * jax.nn.sigmoid(gate_act) * up).astype(BF16)
        out = xs + (0.5 * _gate(mod, sub)) * _dot(a, wd_ref[...])
        if final_norm:
            out = out * lax.rsqrt(jnp.mean(out * out, axis=-1, keepdims=True) + EPS) * fg
        outs.append(out)
    return jnp.concatenate(outs, axis=0)


def _pool_mixer(u, halo, first_row, wp_ref, scale):
    tm = u.shape[0]
    ext = jnp.concatenate([halo, u], axis=0)
    pos = first_row + lax.broadcasted_iota(jnp.int32, (tm, 1), 0)
    gw = wp_ref.shape[-1]
    sums = ext
    shift = 1
    parts = []
    for g, w in enumerate(POOL_WINDOWS):
        while shift < w:
            sums = sums + pltpu.roll(sums, shift, 0)
            shift *= 2
        total = sums[POOL_HALO:, g * gw:(g + 1) * gw]
        count = jnp.minimum(pos + 1, w).astype(F32)
        pooled = total / count - u[:, g * gw:(g + 1) * gw]
        parts.append(_dot(pooled.astype(BF16), wp_ref[g]))
    return (jnp.concatenate(parts, axis=1) * scale).astype(BF16)


def _mixed_input(mix, x_ref, mod, mix_refs):
    if mix is None:
        return x_ref[0]
    if mix == "odd":
        y_ref, wo_ref = mix_refs
        return x_ref[0] + _gate(mod, 1) * _dot(y_ref[0], wo_ref[...])
    u_ref, halo_ref, yd_ref, wp_ref, ps_ref, wo_ref = mix_refs
    i = pl.program_id(1)
    tm = u_ref.shape[1]
    halo = halo_ref[0] * (i > 0).astype(F32)
    y_pool = _pool_mixer(u_ref[0], halo, i * tm, wp_ref, ps_ref[...])
    pw = y_pool.shape[1]
    y = _dot(y_pool, wo_ref[:pw, :]) + _dot(yd_ref[0], wo_ref[pw:, :])
    return x_ref[0] + _gate(mod, 1) * y


_N_MIX_REFS = {None: 0, "odd": 2, "even": 6}


def _ffn_kernel(*refs, sub, final_norm, mix, n_cast):
    x_ref, mod_ref, g_ref, wg_ref, wu_ref, wd_ref, fg_ref = refs[:7]
    n_mix = _N_MIX_REFS[mix]
    mix_refs = refs[7:7 + n_mix]
    cast_in = refs[7 + n_mix:7 + n_mix + n_cast]
    o_ref = refs[7 + n_mix + n_cast]
    cast_out = refs[8 + n_mix + n_cast:]
    mod = mod_ref[0, 0]
    x = _mixed_input(mix, x_ref, mod, mix_refs)
    o_ref[0] = _swiglu_residual(x, mod, sub, g_ref[...], wg_ref, wu_ref, wd_ref, fg_ref[...], final_norm)
    for src_ref, dst_ref in zip(cast_in, cast_out):
        dst_ref[...] = src_ref[(0,) * (len(src_ref.shape) - 2)].astype(BF16)


def _ffn(x, mod, layer, sub, g, weights, final_g, final_norm, mix=None, casts=()):
    b, s, d = x.shape
    f = weights[0].shape[-1]
    tm = FFN_ROWS if mix is None else FFN_MIX_ROWS
    n_i = s // tm
    row = lambda bi, i: (bi, i, 0)
    const2 = lambda bi, i: (0, 0)
    in_specs = [pl.BlockSpec((1, tm, d), row),
                pl.BlockSpec((1, 1, 3 * N_SUB, d), lambda bi, i: (layer, bi, 0, 0)),
                pl.BlockSpec((1, d), const2),
                _resident((d, f), const2),
                _resident((d, f), const2),
                _resident((f, d), const2),
                pl.BlockSpec((1, d), const2)]
    args = [x, mod, g.reshape(1, d), *weights, final_g.reshape(1, d)]
    kind = None if mix is None else mix[0]
    if kind == "odd":
        _, y, w_out = mix
        in_specs += [pl.BlockSpec((1, tm, y.shape[-1]), row), _resident(w_out.shape, const2)]
        args += [y, w_out.astype(BF16)]
    elif kind == "even":
        _, u, y_diff, w_pool, pool_scale, w_out = mix
        pw = u.shape[-1]
        halo_blocks = tm // POOL_HALO
        in_specs += [pl.BlockSpec((1, tm, pw), row),
                     pl.BlockSpec((1, POOL_HALO, pw),
                                  lambda bi, i: (bi, jnp.maximum(i * halo_blocks - 1, 0), 0)),
                     pl.BlockSpec((1, tm, y_diff.shape[-1]), row),
                     _resident(w_pool.shape, lambda bi, i: (0, 0, 0)),
                     pl.BlockSpec((1, pw), const2),
                     _resident(w_out.shape, const2)]
        args += [u, u, y_diff, w_pool.astype(BF16), pool_scale.reshape(1, pw), w_out.astype(BF16)]
    out_specs = [pl.BlockSpec((1, tm, d), row)]
    out_shape = [jax.ShapeDtypeStruct((b, s, d), F32)]
    steps = b * n_i
    assert steps % FFN_CAST_CHUNKS == 0
    chunk_of = lambda bi, i: (bi * n_i + i) // (steps // FFN_CAST_CHUNKS)
    for w, lead in casts:
        rows, cols = w.shape[-2:]
        assert rows % (FFN_CAST_CHUNKS * 16) == 0
        chunk = rows // FFN_CAST_CHUNKS
        in_specs.append(pl.BlockSpec((1,) * len(lead) + (chunk, cols),
                                     lambda bi, i, lead=lead: (*lead, chunk_of(bi, i), 0)))
        args.append(w)
        out_specs.append(pl.BlockSpec((chunk, cols), lambda bi, i: (chunk_of(bi, i), 0)))
        out_shape.append(jax.ShapeDtypeStruct((rows, cols), BF16))
    outs = pl.pallas_call(
        functools.partial(_ffn_kernel, sub=sub, final_norm=final_norm, mix=kind,
                          n_cast=len(casts)),
        grid=(b, n_i),
        in_specs=in_specs,
        out_specs=out_specs,
        out_shape=out_shape,
        compiler_params=_params("arbitrary", "arbitrary"),
        name="ffn" if kind is None else kind + "_mix_ffn",
    )(*args)
    return outs[0], tuple(outs[1:])


def _even_in_kernel(x_ref, mod_ref, g_ref, w_ref, wt_ref, u_ref, k_ref, qt_ref, vt_ref, *, sub):
    h = _modnorm(x_ref[0], g_ref[...], mod_ref[0, 0], sub).astype(BF16)
    p = _dot(h, w_ref[...])
    pw = u_ref.shape[-1]
    u_ref[0] = p[:, :pw]
    k_ref[0] = p[:, pw:].astype(BF16)
    pt = lax.dot_general(wt_ref[...], h, (((1,), (1,)), ((), ())), preferred_element_type=F32)
    qw = qt_ref.shape[1]
    qt_ref[0] = (pt[:qw] * (DIFF_HEAD_DIM ** -0.5 * LOG2_E)).astype(BF16)
    vt_ref[0, 0] = pt[qw:].astype(BF16)


def _even_in_proj(x, mod, layer, g, w_in, pool_width, qk_width):
    b, s, d = x.shape
    n = w_in.shape[1]
    v_width = n - pool_width - 2 * qk_width
    tm = ATTN_ROWS
    k0 = pool_width + qk_width
    w_uk = jnp.concatenate([w_in[:, :pool_width], w_in[:, k0:k0 + qk_width]], axis=1).astype(BF16)
    w_qv_t = jnp.concatenate([w_in[:, pool_width:k0], w_in[:, k0 + qk_width:]], axis=1).T.astype(BF16)
    row = lambda bi, i: (bi, i, 0)
    return pl.pallas_call(
        functools.partial(_even_in_kernel, sub=1),
        grid=(b, s // tm),
        in_specs=[pl.BlockSpec((1, tm, d), row),
                  pl.BlockSpec((1, 1, 3 * N_SUB, d), lambda bi, i: (layer, bi, 0, 0)),
                  pl.BlockSpec((1, d), lambda bi, i: (0, 0)),
                  _resident(w_uk.shape, lambda bi, i: (0, 0)),
                  _resident(w_qv_t.shape, lambda bi, i: (0, 0))],
        out_specs=[pl.BlockSpec((1, tm, pool_width), row),
                   pl.BlockSpec((1, tm, qk_width), row),
                   pl.BlockSpec((1, qk_width, tm), lambda bi, i: (bi, 0, i)),
                   pl.BlockSpec((1, 1, v_width, tm), lambda bi, i: (bi, i, 0, 0))],
        out_shape=[jax.ShapeDtypeStruct((b, s, pool_width), F32),
                   jax.ShapeDtypeStruct((b, s, qk_width), BF16),
                   jax.ShapeDtypeStruct((b, qk_width, s), BF16),
                   jax.ShapeDtypeStruct((b, s // tm, v_width, tm), BF16)],
        compiler_params=_params("arbitrary", "arbitrary"),
        name="even_in_proj",
    )(x, mod, g.reshape(1, d), w_uk, w_qv_t)


def _diff_attn_kernel(qt_ref, k_ref, vt_ref, lam_ref, sg_ref, o_ref, m_ref, l_ref, acc_ref,
                      s0_ref, s1_ref, cm0_ref, cm1_ref, *, lambda_init, t):
    for part in range(ATTN_TILES_PER_STEP):
        rows = slice(part * t, (part + 1) * t)
        _diff_attn_query_tile(pl.program_id(2) * ATTN_TILES_PER_STEP + part, qt_ref[0, :, rows], k_ref,
                              vt_ref, lam_ref, sg_ref, o_ref.at[0, rows, :], m_ref, l_ref, acc_ref,
                              (s0_ref, s1_ref), (cm0_ref, cm1_ref), lambda_init, t)


def _diff_attn_query_tile(i, qt, k_ref, vt_ref, lam_ref, sg_ref, o_ref, m_ref, l_ref, acc_ref, s_refs,
                          cm_refs, lambda_init, t):
    qt = qt.astype(F32)
    feat = lax.broadcasted_iota(jnp.int32, qt.shape, 0)
    qs = jnp.concatenate([jnp.where(feat < DIFF_HEAD_DIM, qt, 0.0),
                          jnp.where(feat >= DIFF_HEAD_DIM, qt, 0.0)], axis=1).astype(BF16)
    m_ref[...] = jnp.full_like(m_ref, -jnp.inf)
    l_ref[...] = jnp.zeros_like(l_ref)
    acc_ref[...] = jnp.zeros_like(acc_ref)
    groups = [slice(c, c + ATTN_COL_GROUP) for c in range(0, 2 * t, ATTN_COL_GROUP)]

    def produce(j, slot, cols):
        kb = k_ref[0, pl.ds(pl.multiple_of(j * t, t), t), :]
        s = _dot(kb, qs[:, cols])
        s_refs[slot][:, cols] = s
        cm_refs[slot][:, cols] = jnp.max(s, axis=0, keepdims=True)

    def consume(j, slot, cols, diagonal):
        if diagonal:
            first_q = cols.start % t
            n_keys = first_q + ATTN_COL_GROUP
            s = s_refs[slot][:n_keys, cols]
            key = lax.broadcasted_iota(jnp.int32, s.shape, 0)
            qry = lax.broadcasted_iota(jnp.int32, s.shape, 1) + first_q
            s = jnp.where(key <= qry, s, -jnp.inf)
            cm = jnp.max(s, axis=0, keepdims=True)
            vt = vt_ref[0, j][:, :n_keys]
        else:
            s = s_refs[slot][:, cols]
            cm = cm_refs[slot][:, cols]
            vt = vt_ref[0, j]
        m = m_ref[:, cols]
        m_new = jnp.maximum(m, cm)
        alpha = jnp.exp2(m - m_new)
        p = jnp.exp2(s - m_new)
        l_ref[:, cols] = alpha * l_ref[:, cols] + jnp.sum(p, axis=0, keepdims=True)
        acc_ref[:, cols] = alpha * acc_ref[:, cols] + _dot(vt, p.astype(BF16))
        m_ref[:, cols] = m_new

    def advance(j, slot):
        for cols in groups:
            produce(j + 1, 1 - slot, cols)
            consume(j, slot, cols, False)

    def diagonal(slot):
        for cols in groups:
            consume(i, slot, cols, True)

    for cols in groups:
        produce(0, 0, cols)

    def body(jj, carry):
        for d in range(ATTN_UNROLL):
            advance(ATTN_UNROLL * jj + d, d % 2)
        return carry

    lax.fori_loop(0, i // ATTN_UNROLL, body, 0)
    rest = i % ATTN_UNROLL

    @pl.when(rest >= 2)
    def _():
        advance(i - rest, 0)
        advance(i - rest + 1, 1)

    @pl.when(rest % 2 == 0)
    def _():
        diagonal(0)

    @pl.when(rest % 2 == 1)
    def _():
        advance(i - 1, 0)
        diagonal(1)

    o = acc_ref[...] / l_ref[...]
    lp = lam_ref[...]
    lam = (jnp.exp(jnp.sum(lp[0:1] * lp[1:2], axis=1, keepdims=True))
           - jnp.exp(jnp.sum(lp[2:3] * lp[3:4], axis=1, keepdims=True)) + lambda_init)
    o = o[:, :t] - lam * o[:, t:]
    o = o * lax.rsqrt(jnp.mean(o * o, axis=0, keepdims=True) + EPS) * sg_ref[...]
    o_ref[...] = (o * (1.0 - lambda_init)).T.astype(o_ref.dtype)


def _diff_attention(qt, k, vt, lam_params, subln_g, lambda_init):
    b, s, _ = k.shape
    hd = 2 * DIFF_HEAD_DIM
    t = ATTN_ROWS
    tq = ATTN_TILES_PER_STEP * t
    assert s % tq == 0 and vt.shape[-1] == t and (2 * t) % ATTN_COL_GROUP == 0
    return pl.pallas_call(
        functools.partial(_diff_attn_kernel, lambda_init=lambda_init, t=t),
        grid=(b, DIFF_HEADS, s // tq),
        in_specs=[pl.BlockSpec((1, hd, tq), lambda bi, h, i: (bi, h, i)),
                  pl.BlockSpec((1, s, hd), lambda bi, h, i: (bi, 0, h)),
                  pl.BlockSpec((1, s // t, hd, t), lambda bi, h, i: (bi, 0, h, 0)),
                  pl.BlockSpec(lam_params.shape, lambda bi, h, i: (0, 0)),
                  pl.BlockSpec((hd, 1), lambda bi, h, i: (0, 0))],
        out_specs=pl.BlockSpec((1, tq, hd), lambda bi, h, i: (bi, i, h)),
        out_shape=jax.ShapeDtypeStruct((b, s, DIFF_HEADS * hd), BF16),
        scratch_shapes=[pltpu.VMEM((1, 2 * t), F32), pltpu.VMEM((1, 2 * t), F32),
                        pltpu.VMEM((hd, 2 * t), F32),
                        pltpu.VMEM((t, 2 * t), F32), pltpu.VMEM((t, 2 * t), F32),
                        pltpu.VMEM((1, 2 * t), F32), pltpu.VMEM((1, 2 * t), F32)],
        compiler_params=_params("arbitrary", "arbitrary", "arbitrary"),
        name="diff_attention",
    )(qt, k, vt, lam_params, subln_g.reshape(hd, 1))


def _odd_in_kernel(x_ref, mod_ref, g_ref, w_ref, wgate_ref, q_ref, k_ref, v_ref, o_ref, gates_ref, *, sub):
    h = _modnorm(x_ref[0], g_ref[...], mod_ref[0, 0], sub).astype(BF16)
    p = _dot(h, w_ref[...])
    nq = q_ref.shape[-1]
    nv = v_ref.shape[-1]
    q_ref[0] = p[:, :nq].astype(BF16)
    k_ref[0] = (p[:, nq:2 * nq] * (MLSTM_QK_DIM ** -0.5)).astype(BF16)
    v_ref[0] = p[:, 2 * nq:2 * nq + nv].astype(BF16)
    o_ref[0] = jax.nn.sigmoid(p[:, 2 * nq + nv:]).astype(BF16)
    gates_ref[0] = _dot(h, wgate_ref[...]).T[:gates_ref.shape[1]]


def _odd_in_proj(x, mod, layer, g, w_in, nq, nv):
    b, s, d = x.shape
    n_main = 2 * nq + 2 * nv
    n_gates = w_in.shape[1] - n_main
    w_gate = jnp.pad(w_in[:, n_main:], ((0, 0), (0, LANES - n_gates)))
    tm = PROJ_ROWS
    row = lambda bi, i: (bi, i, 0)
    return pl.pallas_call(
        functools.partial(_odd_in_kernel, sub=1),
        grid=(b, s // tm),
        in_specs=[pl.BlockSpec((1, tm, d), row),
                  pl.BlockSpec((1, 1, 3 * N_SUB, d), lambda bi, i: (layer, bi, 0, 0)),
                  pl.BlockSpec((1, d), lambda bi, i: (0, 0)),
                  _resident((d, n_main), lambda bi, i: (0, 0)),
                  _resident((d, LANES), lambda bi, i: (0, 0))],
        out_specs=[pl.BlockSpec((1, tm, nq), row),
                   pl.BlockSpec((1, tm, nq), row),
                   pl.BlockSpec((1, tm, nv), row),
                   pl.BlockSpec((1, tm, nv), row),
                   pl.BlockSpec((1, n_gates, tm), lambda bi, i: (bi, 0, i))],
        out_shape=[jax.ShapeDtypeStruct((b, s, nq), BF16),
                   jax.ShapeDtypeStruct((b, s, nq), BF16),
                   jax.ShapeDtypeStruct((b, s, nv), BF16),
                   jax.ShapeDtypeStruct((b, s, nv), BF16),
                   jax.ShapeDtypeStruct((b, n_gates, s), F32)],
        compiler_params=_params("arbitrary", "arbitrary"),
        name="odd_in_proj",
    )(x, mod, g.reshape(1, d), w_in, w_gate)


def _log_sigmoid(x):
    return jnp.minimum(x, 0.0) - jnp.log1p(jnp.exp(-jnp.abs(x)))


def _lane_scan(x, combine, fill):
    lane = lax.broadcasted_iota(jnp.int32, x.shape, 1)
    shift = 1
    while shift < x.shape[-1]:
        x = combine(x, jnp.where(lane >= shift, pltpu.roll(x, shift, 1), fill))
        shift *= 2
    return x


def _mlstm_kernel(bg_ref, q_ref, k_ref, v_ref, og_ref, ig_ref, fg_ref, ig_next_ref, fg_next_ref,
                  ng_ref, y_ref, c_ref, m_ref, cols_ref, src_ref, w_ref, sold_ref, *, chunks):
    head = pl.program_id(1)
    step = pl.program_id(2)
    L = MLSTM_CHUNK
    G = chunks

    def store_gate_stats(i_ref, f_ref):
        i_all = i_ref[0, 0] + bg_ref[head]
        logf = _log_sigmoid(f_ref[0, 0] + bg_ref[MLSTM_HEADS + head])
        b_all = _lane_scan(logf, jnp.add, 0.0)
        b_last = b_all[:, L - 1:L]
        a_all = b_last - b_all + i_all
        m_loc = jnp.max(a_all, axis=1, keepdims=True)

        m_in = []
        m = m_ref[...]
        for g in range(G):
            m_in.append(m)
            m = jnp.maximum(b_last[g:g + 1] + m, m_loc[g:g + 1])
        m_out = m_in[1:] + [m]
        m_ref[...] = m
        m_prev = jnp.concatenate(m_in, axis=0)
        m_next = jnp.concatenate(m_out, axis=0)
        sold_ref[...] = jnp.exp(b_last + m_prev - m_next)
        w_ref[...] = jnp.exp(a_all - m_next)

        src = i_all - b_all
        src_ref[...] = src
        inter_log = b_all + m_prev
        m_t = jnp.maximum(inter_log, b_all + _lane_scan(src, jnp.maximum, -jnp.inf))
        per_row = jnp.concatenate([b_all - m_t, jnp.exp(inter_log - m_t), jnp.exp(-m_t)], axis=0)
        per_row = jnp.concatenate([per_row, jnp.zeros((LANES - 3 * G, L), F32)], axis=0)
        cols_ref[...] = per_row.T

    @pl.when(step == 0)
    def _():
        c_ref[...] = jnp.zeros_like(c_ref)
        m_ref[...] = jnp.zeros_like(m_ref)
        store_gate_stats(ig_ref, fg_ref)

    cols = cols_ref[...]
    src = src_ref[...]
    w_all = w_ref[...]
    s_old = sold_ref[...]
    store_gate_stats(ig_next_ref, fg_next_ref)

    causal = (lax.broadcasted_iota(jnp.int32, (L, L), 1) <= lax.broadcasted_iota(jnp.int32, (L, L), 0))
    dv = v_ref.shape[-1]
    ones_col = (lax.broadcasted_iota(jnp.int32, (L, LANES), 1) == 0).astype(BF16)
    state = c_ref[...]
    for g in range(G):
        rows = slice(g * L, (g + 1) * L)
        q = q_ref[0, rows, :]
        k = k_ref[0, rows, :]
        v_ext = jnp.concatenate([v_ref[0, rows, :], ones_col], axis=1)
        dest = cols[:, g:g + 1]
        inter_w = cols[:, G + g:G + g + 1]
        floor = cols[:, 2 * G + g:2 * G + g + 1]

        dw = jnp.exp(jnp.where(causal, dest + src[g:g + 1], -jnp.inf))
        s = lax.dot_general(q, k, (((1,), (1,)), ((), ())), preferred_element_type=F32) * dw
        q_w = (inter_w * q.astype(F32)).astype(BF16)
        nd = _dot(s.astype(BF16), v_ext) + _dot(q_w, state.astype(BF16))
        num = nd[:, :dv]
        den = nd[:, dv:dv + 1]
        r = 1.0 / jnp.maximum(jnp.abs(den), floor)
        scale = r * lax.rsqrt(r * r * jnp.mean(num * num, axis=-1, keepdims=True) + EPS)
        hn = num * scale * ng_ref[...]
        y_ref[0, rows, :] = (og_ref[0, rows, :].astype(F32) * hn).astype(y_ref.dtype)

        kw_t = (k.astype(F32).T * w_all[g:g + 1]).astype(BF16)
        state = s_old[g:g + 1] * state + _dot(kw_t, v_ext)
    c_ref[...] = state


def _mlstm(q, k, v, o, gates, b_gates, norm_g):
    b, s, _ = q.shape
    dk, dv, L = MLSTM_QK_DIM, MLSTM_V_DIM, MLSTM_CHUNK
    nc = s // L
    g = MLSTM_CHUNKS_PER_STEP
    rows = g * L
    gates = gates.reshape(b, 2 * MLSTM_HEADS, nc, L)
    last = nc // g - 1

    def gate_spec(first, ahead):
        return pl.BlockSpec((1, 1, g, L),
                            lambda bi, h, c: (bi, first + h, jnp.minimum(c + ahead, last), 0))

    return pl.pallas_call(
        functools.partial(_mlstm_kernel, chunks=g),
        grid=(b, MLSTM_HEADS, nc // g),
        in_specs=[pl.BlockSpec(memory_space=pltpu.SMEM),
                  pl.BlockSpec((1, rows, dk), lambda bi, h, c: (bi, c, h)),
                  pl.BlockSpec((1, rows, dk), lambda bi, h, c: (bi, c, h)),
                  pl.BlockSpec((1, rows, dv), lambda bi, h, c: (bi, c, h)),
                  pl.BlockSpec((1, rows, dv), lambda bi, h, c: (bi, c, h)),
                  gate_spec(0, 0), gate_spec(MLSTM_HEADS, 0),
                  gate_spec(0, 1), gate_spec(MLSTM_HEADS, 1),
                  pl.BlockSpec((1, dv), lambda bi, h, c: (0, h))],
        out_specs=pl.BlockSpec((1, rows, dv), lambda bi, h, c: (bi, c, h)),
        out_shape=jax.ShapeDtypeStruct((b, s, MLSTM_HEADS * dv), BF16),
        scratch_shapes=[pltpu.VMEM((dk, dv + LANES), F32), pltpu.VMEM((1, 1), F32),
                        pltpu.VMEM((L, LANES), F32), pltpu.VMEM((g, L), F32), pltpu.VMEM((g, L), F32),
                        pltpu.VMEM((g, 1), F32)],
        compiler_params=_params("arbitrary", "arbitrary", "arbitrary"),
        name="mlstm",
    )(b_gates.reshape(-1), q, k, v, o, gates, gates, gates, gates, norm_g.reshape(1, -1))


def kernel(x, c, w_mod, b_mod, norm_g, w_ffn_gate, w_ffn_up, w_ffn_down, w_in_even, w_pool, pool_scale, diff_lambda, diff_subln_g, w_out_even, w_in_odd, b_gates_odd, mlstm_norm_g, w_out_odd, final_g):
    depth = w_mod.shape[0]
    mod = _modulation(c, w_mod, b_mod)
    pool_width = pool_scale.shape[-1]
    qk_width = DIFF_HEADS * 2 * DIFF_HEAD_DIM
    ffn_stacked = (w_ffn_gate, w_ffn_up, w_ffn_down)
    ffn_w = tuple(w[0, 0].astype(BF16) for w in ffn_stacked)
    for l in range(depth):
        casts = [(w, (l, 1)) for w in ffn_stacked]
        if l % 2 == 1:
            casts.append((w_in_odd, (l // 2,)))
        x, cast = _ffn(x, mod, l, 0, norm_g[l, 0], ffn_w, final_g, False, casts=casts)
        ffn_w = cast[:3]
        if l % 2 == 0:
            e = l // 2
            lambda_init = 0.8 - 0.6 * math.exp(-0.3 * l)
            u, k, qt, vt = _even_in_proj(x, mod, l, norm_g[l, 1], w_in_even[e], pool_width, qk_width)
            y_diff = _diff_attention(qt, k, vt, diff_lambda[e], diff_subln_g[e], lambda_init)
            mix = ("even", u, y_diff, w_pool[e], pool_scale[e], w_out_even[e])
        else:
            o = l // 2
            q, k, v, og, gates = _odd_in_proj(x, mod, l, norm_g[l, 1], cast[3],
                                              MLSTM_HEADS * MLSTM_QK_DIM, MLSTM_HEADS * MLSTM_V_DIM)
            y = _mlstm(q, k, v, og, gates, b_gates_odd[o], mlstm_norm_g[o])
            mix = ("odd", y, w_out_odd[o])
        last = l == depth - 1
        x, cast = _ffn(x, mod, l, 2, norm_g[l, 2], ffn_w, final_g, last, mix,
                       casts=() if last else [(w, (l + 1, 0)) for w in ffn_stacked])
        ffn_w = cast[:3]
    return x
```

```python
import functools
import math

import jax
import jax.numpy as jnp
from jax import lax
from jax.experimental import pallas as pl
from jax.experimental.pallas import tpu as pltpu

F32 = jnp.float32
BF16 = jnp.bfloat16

EPS = 1e-6
N_SUB = 3
POOL_WINDOWS = (2, 4, 8, 16)
POOL_HALO = 16
DIFF_HEADS = 4
DIFF_HEAD_DIM = 64
MLSTM_HEADS = 4
MLSTM_QK_DIM = 128
MLSTM_V_DIM = 256
MLSTM_CHUNK = 128
LANES = 128

MIB = 1024 * 1024
VMEM_CAP_BYTES = 56 * MIB
BF16_SUBLANES = 16

MOD_ROWS = 256
FFN_ROWS = 1024
FFN_MIX_ROWS = 512
FFN_CAST_CHUNKS = 16
FFN_SUB_ROWS = 256
PROJ_ROWS = 1024
ATTN_ROWS = 512
ATTN_COL_GROUP = 256
ATTN_TILES_PER_STEP = 2
ATTN_UNROLL = 4
LOG2_E = math.log2(math.e)
MLSTM_CHUNKS_PER_STEP = 32


def _params(vmem_mib, *semantics):
    return pltpu.CompilerParams(dimension_semantics=semantics,
                                vmem_limit_bytes=min(vmem_mib * MIB, VMEM_CAP_BYTES))


def _resident(block_shape, index_map):
    return pl.BlockSpec(block_shape, index_map, pipeline_mode=pl.Buffered(1))


def _dot(a, b):
    return jnp.dot(a, b, preferred_element_type=F32)


def _modnorm(x, g, mod, sub):
    shift = mod[3 * sub:3 * sub + 1]
    scale = mod[3 * sub + 1:3 * sub + 2]
    y = x * lax.rsqrt(jnp.mean(x * x, axis=-1, keepdims=True) + EPS)
    return (y * g) * (1.0 + scale) + shift


def _gate(mod, sub):
    return mod[3 * sub + 2:3 * sub + 3]


def _mod_kernel(c_ref, w_ref, b_ref, o_ref):
    @pl.when(pl.program_id(1) == 0)
    def _():
        o_ref[0] = jnp.broadcast_to(b_ref[0], o_ref.shape[1:])

    c = c_ref[...]
    c_act = c * jax.nn.sigmoid(c)
    w = w_ref[0]
    c_hi = c_act.astype(BF16)
    c_lo = (c_act - c_hi.astype(F32)).astype(BF16)
    w_hi = w.astype(BF16)
    w_lo = (w - w_hi.astype(F32)).astype(BF16)
    o_ref[0] += _dot(c_hi, w_hi) + (_dot(c_hi, w_lo) + _dot(c_lo, w_hi))


def _modulation(c, w_mod, b_mod):
    depth, d, n = w_mod.shape
    b = c.shape[0]
    tk = MOD_ROWS
    out = pl.pallas_call(
        _mod_kernel,
        grid=(depth, d // tk),
        in_specs=[pl.BlockSpec((b, tk), lambda l, j: (0, j)),
                  pl.BlockSpec((1, tk, n), lambda l, j: (l, j, 0)),
                  pl.BlockSpec((1, 1, n), lambda l, j: (l, 0, 0))],
        out_specs=pl.BlockSpec((1, b, n), lambda l, j: (l, 0, 0)),
        out_shape=jax.ShapeDtypeStruct((depth, b, n), F32),
        compiler_params=_params(48, "arbitrary", "arbitrary"),
        name="modulation",
    )(c, w_mod, b_mod.reshape(depth, 1, n))
    return out.reshape(depth, b, 3 * N_SUB, d)


def _swiglu_residual(x, mod, sub, g, wg_ref, wu_ref, wd_ref, fg, final_norm):
    outs = []
    for r in range(0, x.shape[0], FFN_SUB_ROWS):
        xs = x[r:r + FFN_SUB_ROWS]
        h = _modnorm(xs, g, mod, sub).astype(BF16)
        gate_act = _dot(h, wg_ref[...])
        up = _dot(h, wu_ref[...])
        a = (gate_act * jax.nn.sigmoid(gate_act) * up).astype(BF16)
        out = xs + (0.5 * _gate(mod, sub)) * _dot(a, wd_ref[...])
        if final_norm:
            out = out * lax.rsqrt(jnp.mean(out * out, axis=-1, keepdims=True) + EPS) * fg
        outs.append(out)
    return jnp.concatenate(outs, axis=0)


def _pool_mixer(u, halo, first_row, wp_ref, scale):
    tm = u.shape[0]
    ext = jnp.concatenate([halo, u], axis=0)
    pos = first_row + lax.broadcasted_iota(jnp.int32, (tm, 1), 0)
    gw = wp_ref.shape[-1]
    sums = ext
    shift = 1
    parts = []
    for g, w in enumerate(POOL_WINDOWS):
        while shift < w:
            sums = sums + pltpu.roll(sums, shift, 0)
            shift *= 2
        total = sums[POOL_HALO:, g * gw:(g + 1) * gw]
        count = jnp.minimum(pos + 1, w).astype(F32)
        pooled = total / count - u[:, g * gw:(g + 1) * gw]
        parts.append(_dot(pooled.astype(BF16), wp_ref[g]))
    return (jnp.concatenate(parts, axis=1) * scale).astype(BF16)


def _mixed_input(mix, x_ref, mod, mix_refs):
    if mix is None:
        return x_ref[0]
    if mix == "odd":
        y_ref, wo_ref = mix_refs
        return x_ref[0] + _gate(mod, 1) * _dot(y_ref[0], wo_ref[...])
    u_ref, halo_ref, yd_ref, wp_ref, ps_ref, wo_ref = mix_refs
    i = pl.program_id(1)
    tm = u_ref.shape[1]
    halo = halo_ref[0] * (i > 0).astype(F32)
    y_pool = _pool_mixer(u_ref[0], halo, i * tm, wp_ref, ps_ref[...])
    pw = y_pool.shape[1]
    y = _dot(y_pool, wo_ref[:pw, :]) + _dot(yd_ref[0], wo_ref[pw:, :])
    return x_ref[0] + _gate(mod, 1) * y


_N_MIX_REFS = {None: 0, "odd": 2, "even": 6}


def _ffn_kernel(*refs, sub, final_norm, mix, n_cast):
    x_ref, mod_ref, g_ref, wg_ref, wu_ref, wd_ref, fg_ref = refs[:7]
    n_mix = _N_MIX_REFS[mix]
    mix_refs = refs[7:7 + n_mix]
    cast_in = refs[7 + n_mix:7 + n_mix + n_cast]
    o_ref = refs[7 + n_mix + n_cast]
    cast_out = refs[8 + n_mix + n_cast:]
    mod = mod_ref[0, 0]
    x = _mixed_input(mix, x_ref, mod, mix_refs)
    o_ref[0] = _swiglu_residual(x, mod, sub, g_ref[...], wg_ref, wu_ref, wd_ref, fg_ref[...], final_norm)
    for src_ref, dst_ref in zip(cast_in, cast_out):
        dst_ref[...] = src_ref[(0,) * (len(src_ref.shape) - 2)].astype(BF16)


def _ffn(x, mod, layer, sub, g, weights, final_g, final_norm, mix=None, casts=()):
    b, s, d = x.shape
    f = weights[0].shape[-1]
    tm = FFN_ROWS if mix is None else FFN_MIX_ROWS
    n_i = s // tm
    row = lambda bi, i: (bi, i, 0)
    const2 = lambda bi, i: (0, 0)
    in_specs = [pl.BlockSpec((1, tm, d), row),
                pl.BlockSpec((1, 1, 3 * N_SUB, d), lambda bi, i: (layer, bi, 0, 0)),
                pl.BlockSpec((1, d), const2),
                _resident((d, f), const2),
                _resident((d, f), const2),
                _resident((f, d), const2),
                pl.BlockSpec((1, d), const2)]
    args = [x, mod, g.reshape(1, d), *weights, final_g.reshape(1, d)]
    kind = None if mix is None else mix[0]
    if kind == "odd":
        _, y, w_out = mix
        in_specs += [pl.BlockSpec((1, tm, y.shape[-1]), row), _resident(w_out.shape, const2)]
        args += [y, w_out.astype(BF16)]
    elif kind == "even":
        _, u, y_diff, w_pool, pool_scale, w_out = mix
        pw = u.shape[-1]
        halo_blocks = tm // POOL_HALO
        in_specs += [pl.BlockSpec((1, tm, pw), row),
                     pl.BlockSpec((1, POOL_HALO, pw),
                                  lambda bi, i: (bi, jnp.maximum(i * halo_blocks - 1, 0), 0)),
                     pl.BlockSpec((1, tm, y_diff.shape[-1]), row),
                     _resident(w_pool.shape, lambda bi, i: (0, 0, 0)),
                     pl.BlockSpec((1, pw), const2),
                     _resident(w_out.shape, const2)]
        args += [u, u, y_diff, w_pool.astype(BF16), pool_scale.reshape(1, pw), w_out.astype(BF16)]
    out_specs = [pl.BlockSpec((1, tm, d), row)]
    out_shape = [jax.ShapeDtypeStruct((b, s, d), F32)]
    steps = b * n_i
    assert steps % FFN_CAST_CHUNKS == 0
    chunk_of = lambda bi, i: (bi * n_i + i) // (steps // FFN_CAST_CHUNKS)
    for w, lead in casts:
        rows, cols = w.shape[-2:]
        assert rows % (FFN_CAST_CHUNKS * BF16_SUBLANES) == 0
        chunk = rows // FFN_CAST_CHUNKS
        in_specs.append(pl.BlockSpec((1,) * len(lead) + (chunk, cols),
                                     lambda bi, i, lead=lead: (*lead, chunk_of(bi, i), 0)))
        args.append(w)
        out_specs.append(pl.BlockSpec((chunk, cols), lambda bi, i: (chunk_of(bi, i), 0)))
        out_shape.append(jax.ShapeDtypeStruct((rows, cols), BF16))
    outs = pl.pallas_call(
        functools.partial(_ffn_kernel, sub=sub, final_norm=final_norm, mix=kind,
                          n_cast=len(casts)),
        grid=(b, n_i),
        in_specs=in_specs,
        out_specs=out_specs,
        out_shape=out_shape,
        compiler_params=_params(56, "arbitrary", "arbitrary"),
        name="ffn" if kind is None else kind + "_mix_ffn",
    )(*args)
    return outs[0], tuple(outs[1:])


def _even_in_kernel(x_ref, mod_ref, g_ref, w_ref, wt_ref, u_ref, k_ref, qt_ref, vt_ref, *, sub):
    h = _modnorm(x_ref[0], g_ref[...], mod_ref[0, 0], sub).astype(BF16)
    p = _dot(h, w_ref[...])
    pw = u_ref.shape[-1]
    u_ref[0] = p[:, :pw]
    k_ref[0] = p[:, pw:].astype(BF16)
    pt = lax.dot_general(wt_ref[...], h, (((1,), (1,)), ((), ())), preferred_element_type=F32)
    qw = qt_ref.shape[1]
    qt_ref[0] = (pt[:qw] * (DIFF_HEAD_DIM ** -0.5 * LOG2_E)).astype(BF16)
    vt_ref[0, 0] = pt[qw:].astype(BF16)


def _even_in_proj(x, mod, layer, g, w_in, pool_width, qk_width):
    b, s, d = x.shape
    n = w_in.shape[1]
    v_width = n - pool_width - 2 * qk_width
    tm = ATTN_ROWS
    k0 = pool_width + qk_width
    w_uk = jnp.concatenate([w_in[:, :pool_width], w_in[:, k0:k0 + qk_width]], axis=1).astype(BF16)
    w_qv_t = jnp.concatenate([w_in[:, pool_width:k0], w_in[:, k0 + qk_width:]], axis=1).T.astype(BF16)
    row = lambda bi, i: (bi, i, 0)
    return pl.pallas_call(
        functools.partial(_even_in_kernel, sub=1),
        grid=(b, s // tm),
        in_specs=[pl.BlockSpec((1, tm, d), row),
                  pl.BlockSpec((1, 1, 3 * N_SUB, d), lambda bi, i: (layer, bi, 0, 0)),
                  pl.BlockSpec((1, d), lambda bi, i: (0, 0)),
                  _resident(w_uk.shape, lambda bi, i: (0, 0)),
                  _resident(w_qv_t.shape, lambda bi, i: (0, 0))],
        out_specs=[pl.BlockSpec((1, tm, pool_width), row),
                   pl.BlockSpec((1, tm, qk_width), row),
                   pl.BlockSpec((1, qk_width, tm), lambda bi, i: (bi, 0, i)),
                   pl.BlockSpec((1, 1, v_width, tm), lambda bi, i: (bi, i, 0, 0))],
        out_shape=[jax.ShapeDtypeStruct((b, s, pool_width), F32),
                   jax.ShapeDtypeStruct((b, s, qk_width), BF16),
                   jax.ShapeDtypeStruct((b, qk_width, s), BF16),
                   jax.ShapeDtypeStruct((b, s // tm, v_width, tm), BF16)],
        compiler_params=_params(32, "arbitrary", "arbitrary"),
        name="even_in_proj",
    )(x, mod, g.reshape(1, d), w_uk, w_qv_t)


def _diff_attn_kernel(qt_ref, k_ref, vt_ref, lam_ref, sg_ref, o_ref, m_ref, l_ref, acc_ref,
                      s0_ref, s1_ref, cm0_ref, cm1_ref, *, lambda_init, t):
    for part in range(ATTN_TILES_PER_STEP):
        rows = slice(part * t, (part + 1) * t)
        _diff_attn_query_tile(pl.program_id(2) * ATTN_TILES_PER_STEP + part, qt_ref[0, :, rows], k_ref,
                              vt_ref, lam_ref, sg_ref, o_ref.at[0, rows, :], m_ref, l_ref, acc_ref,
                              (s0_ref, s1_ref), (cm0_ref, cm1_ref), lambda_init, t)


def _diff_attn_query_tile(i, qt, k_ref, vt_ref, lam_ref, sg_ref, o_ref, m_ref, l_ref, acc_ref, s_refs,
                          cm_refs, lambda_init, t):
    qt = qt.astype(F32)
    feat = lax.broadcasted_iota(jnp.int32, qt.shape, 0)
    qs = jnp.concatenate([jnp.where(feat < DIFF_HEAD_DIM, qt, 0.0),
                          jnp.where(feat >= DIFF_HEAD_DIM, qt, 0.0)], axis=1).astype(BF16)
    m_ref[...] = jnp.full_like(m_ref, -jnp.inf)
    l_ref[...] = jnp.zeros_like(l_ref)
    acc_ref[...] = jnp.zeros_like(acc_ref)
    groups = [slice(c, c + ATTN_COL_GROUP) for c in range(0, 2 * t, ATTN_COL_GROUP)]

    def produce(j, slot, cols):
        kb = k_ref[0, pl.ds(pl.multiple_of(j * t, t), t), :]
        s = _dot(kb, qs[:, cols])
        s_refs[slot][:, cols] = s
        cm_refs[slot][:, cols] = jnp.max(s, axis=0, keepdims=True)

    def consume(j, slot, cols, diagonal):
        if diagonal:
            first_q = cols.start % t
            n_keys = first_q + ATTN_COL_GROUP
            s = s_refs[slot][:n_keys, cols]
            key = lax.broadcasted_iota(jnp.int32, s.shape, 0)
            qry = lax.broadcasted_iota(jnp.int32, s.shape, 1) + first_q
            s = jnp.where(key <= qry, s, -jnp.inf)
            cm = jnp.max(s, axis=0, keepdims=True)
            vt = vt_ref[0, j][:, :n_keys]
        else:
            s = s_refs[slot][:, cols]
            cm = cm_refs[slot][:, cols]
            vt = vt_ref[0, j]
        m = m_ref[:, cols]
        m_new = jnp.maximum(m, cm)
        alpha = jnp.exp2(m - m_new)
        p = jnp.exp2(s - m_new)
        l_ref[:, cols] = alpha * l_ref[:, cols] + jnp.sum(p, axis=0, keepdims=True)
        acc_ref[:, cols] = alpha * acc_ref[:, cols] + _dot(vt, p.astype(BF16))
        m_ref[:, cols] = m_new

    def advance(j, slot):
        for cols in groups:
            produce(j + 1, 1 - slot, cols)
            consume(j, slot, cols, False)

    def diagonal(slot):
        for cols in groups:
            consume(i, slot, cols, True)

    for cols in groups:
        produce(0, 0, cols)

    def body(jj, carry):
        for d in range(ATTN_UNROLL):
            advance(ATTN_UNROLL * jj + d, d % 2)
        return carry

    lax.fori_loop(0, i // ATTN_UNROLL, body, 0)
    rest = i % ATTN_UNROLL

    @pl.when(rest >= 2)
    def _():
        advance(i - rest, 0)
        advance(i - rest + 1, 1)

    @pl.when(rest % 2 == 0)
    def _():
        diagonal(0)

    @pl.when(rest % 2 == 1)
    def _():
        advance(i - 1, 0)
        diagonal(1)

    o = acc_ref[...] / l_ref[...]
    lp = lam_ref[...]
    lam = (jnp.exp(jnp.sum(lp[0:1] * lp[1:2], axis=1, keepdims=True))
           - jnp.exp(jnp.sum(lp[2:3] * lp[3:4], axis=1, keepdims=True)) + lambda_init)
    o = o[:, :t] - lam * o[:, t:]
    o = o * lax.rsqrt(jnp.mean(o * o, axis=0, keepdims=True) + EPS) * sg_ref[...]
    o_ref[...] = (o * (1.0 - lambda_init)).T.astype(o_ref.dtype)


def _diff_attention(qt, k, vt, lam_params, subln_g, lambda_init):
    b, s, _ = k.shape
    hd = 2 * DIFF_HEAD_DIM
    t = ATTN_ROWS
    tq = ATTN_TILES_PER_STEP * t
    assert s % tq == 0 and vt.shape[-1] == t and (2 * t) % ATTN_COL_GROUP == 0
    return pl.pallas_call(
        functools.partial(_diff_attn_kernel, lambda_init=lambda_init, t=t),
        grid=(b, DIFF_HEADS, s // tq),
        in_specs=[pl.BlockSpec((1, hd, tq), lambda bi, h, i: (bi, h, i)),
                  pl.BlockSpec((1, s, hd), lambda bi, h, i: (bi, 0, h)),
                  pl.BlockSpec((1, s // t, hd, t), lambda bi, h, i: (bi, 0, h, 0)),
                  pl.BlockSpec(lam_params.shape, lambda bi, h, i: (0, 0)),
                  pl.BlockSpec((hd, 1), lambda bi, h, i: (0, 0))],
        out_specs=pl.BlockSpec((1, tq, hd), lambda bi, h, i: (bi, i, h)),
        out_shape=jax.ShapeDtypeStruct((b, s, DIFF_HEADS * hd), BF16),
        scratch_shapes=[pltpu.VMEM((1, 2 * t), F32), pltpu.VMEM((1, 2 * t), F32),
                        pltpu.VMEM((hd, 2 * t), F32),
                        pltpu.VMEM((t, 2 * t), F32), pltpu.VMEM((t, 2 * t), F32),
                        pltpu.VMEM((1, 2 * t), F32), pltpu.VMEM((1, 2 * t), F32)],
        compiler_params=_params(32, "arbitrary", "arbitrary", "arbitrary"),
        name="diff_attention",
    )(qt, k, vt, lam_params, subln_g.reshape(hd, 1))


def _odd_in_kernel(x_ref, mod_ref, g_ref, w_ref, wgate_ref, q_ref, k_ref, v_ref, o_ref, gates_ref, *, sub):
    h = _modnorm(x_ref[0], g_ref[...], mod_ref[0, 0], sub).astype(BF16)
    p = _dot(h, w_ref[...])
    nq = q_ref.shape[-1]
    nv = v_ref.shape[-1]
    q_ref[0] = p[:, :nq].astype(BF16)
    k_ref[0] = (p[:, nq:2 * nq] * (MLSTM_QK_DIM ** -0.5)).astype(BF16)
    v_ref[0] = p[:, 2 * nq:2 * nq + nv].astype(BF16)
    o_ref[0] = jax.nn.sigmoid(p[:, 2 * nq + nv:]).astype(BF16)
    gates_ref[0] = _dot(h, wgate_ref[...]).T[:gates_ref.shape[1]]


def _odd_in_proj(x, mod, layer, g, w_in, nq, nv):
    b, s, d = x.shape
    n_main = 2 * nq + 2 * nv
    n_gates = w_in.shape[1] - n_main
    w_gate = jnp.pad(w_in[:, n_main:], ((0, 0), (0, LANES - n_gates)))
    tm = PROJ_ROWS
    row = lambda bi, i: (bi, i, 0)
    return pl.pallas_call(
        functools.partial(_odd_in_kernel, sub=1),
        grid=(b, s // tm),
        in_specs=[pl.BlockSpec((1, tm, d), row),
                  pl.BlockSpec((1, 1, 3 * N_SUB, d), lambda bi, i: (layer, bi, 0, 0)),
                  pl.BlockSpec((1, d), lambda bi, i: (0, 0)),
                  _resident((d, n_main), lambda bi, i: (0, 0)),
                  _resident((d, LANES), lambda bi, i: (0, 0))],
        out_specs=[pl.BlockSpec((1, tm, nq), row),
                   pl.BlockSpec((1, tm, nq), row),
                   pl.BlockSpec((1, tm, nv), row),
                   pl.BlockSpec((1, tm, nv), row),
                   pl.BlockSpec((1, n_gates, tm), lambda bi, i: (bi, 0, i))],
        out_shape=[jax.ShapeDtypeStruct((b, s, nq), BF16),
                   jax.ShapeDtypeStruct((b, s, nq), BF16),
                   jax.ShapeDtypeStruct((b, s, nv), BF16),
                   jax.ShapeDtypeStruct((b, s, nv), BF16),
                   jax.ShapeDtypeStruct((b, n_gates, s), F32)],
        compiler_params=_params(48, "arbitrary", "arbitrary"),
        name="odd_in_proj",
    )(x, mod, g.reshape(1, d), w_in, w_gate)


def _log_sigmoid(x):
    return jnp.minimum(x, 0.0) - jnp.log1p(jnp.exp(-jnp.abs(x)))


def _lane_scan(x, combine, fill):
    lane = lax.broadcasted_iota(jnp.int32, x.shape, 1)
    shift = 1
    while shift < x.shape[-1]:
        x = combine(x, jnp.where(lane >= shift, pltpu.roll(x, shift, 1), fill))
        shift *= 2
    return x


def _mlstm_kernel(bg_ref, q_ref, k_ref, v_ref, og_ref, ig_ref, fg_ref, ig_next_ref, fg_next_ref,
                  ng_ref, y_ref, c_ref, m_ref, cols_ref, src_ref, w_ref, sold_ref, *, chunks):
    head = pl.program_id(1)
    step = pl.program_id(2)
    L = MLSTM_CHUNK
    G = chunks

    def store_gate_stats(i_ref, f_ref):
        i_all = i_ref[0, 0] + bg_ref[head]
        logf = _log_sigmoid(f_ref[0, 0] + bg_ref[MLSTM_HEADS + head])
        b_all = _lane_scan(logf, jnp.add, 0.0)
        b_last = b_all[:, L - 1:L]
        a_all = b_last - b_all + i_all
        m_loc = jnp.max(a_all, axis=1, keepdims=True)

        m_in = []
        m = m_ref[...]
        for g in range(G):
            m_in.append(m)
            m = jnp.maximum(b_last[g:g + 1] + m, m_loc[g:g + 1])
        m_out = m_in[1:] + [m]
        m_ref[...] = m
        m_prev = jnp.concatenate(m_in, axis=0)
        m_next = jnp.concatenate(m_out, axis=0)
        sold_ref[...] = jnp.exp(b_last + m_prev - m_next)
        w_ref[...] = jnp.exp(a_all - m_next)

        src = i_all - b_all
        src_ref[...] = src
        inter_log = b_all + m_prev
        m_t = jnp.maximum(inter_log, b_all + _lane_scan(src, jnp.maximum, -jnp.inf))
        per_row = jnp.concatenate([b_all - m_t, jnp.exp(inter_log - m_t), jnp.exp(-m_t)], axis=0)
        per_row = jnp.concatenate([per_row, jnp.zeros((LANES - 3 * G, L), F32)], axis=0)
        cols_ref[...] = per_row.T

    @pl.when(step == 0)
    def _():
        c_ref[...] = jnp.zeros_like(c_ref)
        m_ref[...] = jnp.zeros_like(m_ref)
        store_gate_stats(ig_ref, fg_ref)

    cols = cols_ref[...]
    src = src_ref[...]
    w_all = w_ref[...]
    s_old = sold_ref[...]
    store_gate_stats(ig_next_ref, fg_next_ref)

    causal = (lax.broadcasted_iota(jnp.int32, (L, L), 1) <= lax.broadcasted_iota(jnp.int32, (L, L), 0))
    dv = v_ref.shape[-1]
    ones_col = (lax.broadcasted_iota(jnp.int32, (L, LANES), 1) == 0).astype(BF16)
    state = c_ref[...]
    for g in range(G):
        rows = slice(g * L, (g + 1) * L)
        q = q_ref[0, rows, :]
        k = k_ref[0, rows, :]
        v_ext = jnp.concatenate([v_ref[0, rows, :], ones_col], axis=1)
        dest = cols[:, g:g + 1]
        inter_w = cols[:, G + g:G + g + 1]
        floor = cols[:, 2 * G + g:2 * G + g + 1]

        dw = jnp.exp(jnp.where(causal, dest + src[g:g + 1], -jnp.inf))
        s = lax.dot_general(q, k, (((1,), (1,)), ((), ())), preferred_element_type=F32) * dw
        q_w = (inter_w * q.astype(F32)).astype(BF16)
        nd = _dot(s.astype(BF16), v_ext) + _dot(q_w, state.astype(BF16))
        num = nd[:, :dv]
        den = nd[:, dv:dv + 1]
        r = 1.0 / jnp.maximum(jnp.abs(den), floor)
        scale = r * lax.rsqrt(r * r * jnp.mean(num * num, axis=-1, keepdims=True) + EPS)
        hn = num * scale * ng_ref[...]
        y_ref[0, rows, :] = (og_ref[0, rows, :].astype(F32) * hn).astype(y_ref.dtype)

        kw_t = (k.astype(F32).T * w_all[g:g + 1]).astype(BF16)
        state = s_old[g:g + 1] * state + _dot(kw_t, v_ext)
    c_ref[...] = state


def _mlstm(q, k, v, o, gates, b_gates, norm_g):
    b, s, _ = q.shape
    dk, dv, L = MLSTM_QK_DIM, MLSTM_V_DIM, MLSTM_CHUNK
    nc = s // L
    g = MLSTM_CHUNKS_PER_STEP
    rows = g * L
    gates = gates.reshape(b, 2 * MLSTM_HEADS, nc, L)
    last = nc // g - 1

    def gate_spec(first, ahead):
        return pl.BlockSpec((1, 1, g, L),
                            lambda bi, h, c: (bi, first + h, jnp.minimum(c + ahead, last), 0))

    return pl.pallas_call(
        functools.partial(_mlstm_kernel, chunks=g),
        grid=(b, MLSTM_HEADS, nc // g),
        in_specs=[pl.BlockSpec(memory_space=pltpu.SMEM),
                  pl.BlockSpec((1, rows, dk), lambda bi, h, c: (bi, c, h)),
                  pl.BlockSpec((1, rows, dk), lambda bi, h, c: (bi, c, h)),
                  pl.BlockSpec((1, rows, dv), lambda bi, h, c: (bi, c, h)),
                  pl.BlockSpec((1, rows, dv), lambda bi, h, c: (bi, c, h)),
                  gate_spec(0, 0), gate_spec(MLSTM_HEADS, 0),
                  gate_spec(0, 1), gate_spec(MLSTM_HEADS, 1),
                  pl.BlockSpec((1, dv), lambda bi, h, c: (0, h))],
        out_specs=pl.BlockSpec((1, rows, dv), lambda bi, h, c: (bi, c, h)),
        out_shape=jax.ShapeDtypeStruct((b, s, MLSTM_HEADS * dv), BF16),
        scratch_shapes=[pltpu.VMEM((dk, dv + LANES), F32), pltpu.VMEM((1, 1), F32),
                        pltpu.VMEM((L, LANES), F32), pltpu.VMEM((g, L), F32), pltpu.VMEM((g, L), F32),
                        pltpu.VMEM((g, 1), F32)],
        compiler_params=_params(32, "arbitrary", "arbitrary", "arbitrary"),
        name="mlstm",
    )(b_gates.reshape(-1), q, k, v, o, gates, gates, gates, gates, norm_g.reshape(1, -1))


def kernel(x, c, w_mod, b_mod, norm_g, w_ffn_gate, w_ffn_up, w_ffn_down, w_in_even, w_pool, pool_scale, diff_lambda, diff_subln_g, w_out_even, w_in_odd, b_gates_odd, mlstm_norm_g, w_out_odd, final_g):
    depth = w_mod.shape[0]
    mod = _modulation(c, w_mod, b_mod)
    pool_width = pool_scale.shape[-1]
    qk_width = DIFF_HEADS * 2 * DIFF_HEAD_DIM
    ffn_stacked = (w_ffn_gate, w_ffn_up, w_ffn_down)
    ffn_w = tuple(w[0, 0].astype(BF16) for w in ffn_stacked)
    for l in range(depth):
        casts = [(w, (l, 1)) for w in ffn_stacked]
        if l % 2 == 1:
            casts.append((w_in_odd, (l // 2,)))
        x, cast = _ffn(x, mod, l, 0, norm_g[l, 0], ffn_w, final_g, False, casts=casts)
        ffn_w = cast[:3]
        if l % 2 == 0:
            e = l // 2
            lambda_init = 0.8 - 0.6 * math.exp(-0.3 * l)
            u, k, qt, vt = _even_in_proj(x, mod, l, norm_g[l, 1], w_in_even[e], pool_width, qk_width)
            y_diff = _diff_attention(qt, k, vt, diff_lambda[e], diff_subln_g[e], lambda_init)
            mix = ("even", u, y_diff, w_pool[e], pool_scale[e], w_out_even[e])
        else:
            o = l // 2
            q, k, v, og, gates = _odd_in_proj(x, mod, l, norm_g[l, 1], cast[3],
                                              MLSTM_HEADS * MLSTM_QK_DIM, MLSTM_HEADS * MLSTM_V_DIM)
            y = _mlstm(q, k, v, og, gates, b_gates_odd[o], mlstm_norm_g[o])
            mix = ("odd", y, w_out_odd[o])
        last = l == depth - 1
        x, cast = _ffn(x, mod, l, 2, norm_g[l, 2], ffn_w, final_g, last, mix,
                       casts=() if last else [(w, (l + 1, 0)) for w in ffn_stacked])
        ffn_w = cast[:3]
    return x
```

```python
import functools
import math

import jax
import jax.numpy as jnp
from jax import lax
from jax.experimental import pallas as pl
from jax.experimental.pallas import tpu as pltpu

F32 = jnp.float32
BF16 = jnp.bfloat16

EPS = 1e-6
N_SUB = 3
POOL_WINDOWS = (2, 4, 8, 16)
POOL_HALO = 16
DIFF_HEADS = 4
DIFF_HEAD_DIM = 64
MLSTM_HEADS = 4
MLSTM_QK_DIM = 128
MLSTM_V_DIM = 256
MLSTM_CHUNK = 128
LANES = 128
BF16_SUBLANES = 16

VMEM_LIMIT_BYTES = 56 * 1024 * 1024

MOD_ROWS = 256
FFN_ROWS = 512
FFN_CAST_CHUNKS = 16
FFN_SUB_ROWS = 256
ATTN_ROWS = 512
ATTN_COL_GROUP = 256
ATTN_TILES_PER_STEP = 2
ATTN_UNROLL = 4
LOG2_E = math.log2(math.e)
MLSTM_CHUNKS_PER_STEP = 32


def _params(*semantics):
    return pltpu.CompilerParams(dimension_semantics=semantics,
                                vmem_limit_bytes=VMEM_LIMIT_BYTES)


def _resident(block_shape, index_map):
    return pl.BlockSpec(block_shape, index_map, pipeline_mode=pl.Buffered(1))


def _dot(a, b):
    return jnp.dot(a, b, preferred_element_type=F32)


def _modnorm(x, g, mod, sub):
    shift = mod[3 * sub:3 * sub + 1]
    scale = mod[3 * sub + 1:3 * sub + 2]
    y = x * lax.rsqrt(jnp.mean(x * x, axis=-1, keepdims=True) + EPS)
    return (y * g) * (1.0 + scale) + shift


def _gate(mod, sub):
    return mod[3 * sub + 2:3 * sub + 3]


def _mod_kernel(c_ref, w_ref, b_ref, o_ref):
    @pl.when(pl.program_id(1) == 0)
    def _():
        o_ref[0] = jnp.broadcast_to(b_ref[0], o_ref.shape[1:])

    c = c_ref[...]
    c_act = c * jax.nn.sigmoid(c)
    w = w_ref[0]
    c_hi = c_act.astype(BF16)
    c_lo = (c_act - c_hi.astype(F32)).astype(BF16)
    w_hi = w.astype(BF16)
    w_lo = (w - w_hi.astype(F32)).astype(BF16)
    o_ref[0] += _dot(c_hi, w_hi) + (_dot(c_hi, w_lo) + _dot(c_lo, w_hi))


def _modulation(c, w_mod, b_mod):
    depth, d, n = w_mod.shape
    b = c.shape[0]
    tk = MOD_ROWS
    out = pl.pallas_call(
        _mod_kernel,
        grid=(depth, d // tk),
        in_specs=[pl.BlockSpec((b, tk), lambda l, j: (0, j)),
                  pl.BlockSpec((1, tk, n), lambda l, j: (l, j, 0)),
                  pl.BlockSpec((1, 1, n), lambda l, j: (l, 0, 0))],
        out_specs=pl.BlockSpec((1, b, n), lambda l, j: (l, 0, 0)),
        out_shape=jax.ShapeDtypeStruct((depth, b, n), F32),
        compiler_params=_params("arbitrary", "arbitrary"),
        name="modulation",
    )(c, w_mod, b_mod.reshape(depth, 1, n))
    return out.reshape(depth, b, 3 * N_SUB, d)


def _swiglu_residual(x, mod, sub, g, wg_ref, wu_ref, wd_ref, fg, final_norm):
    outs = []
    for r in range(0, x.shape[0], FFN_SUB_ROWS):
        xs = x[r:r + FFN_SUB_ROWS]
        h = _modnorm(xs, g, mod, sub).astype(BF16)
        gate_act = _dot(h, wg_ref[...])
        up = _dot(h, wu_ref[...])
        a = (gate_act * jax.nn.sigmoid(gate_act) * up).astype(BF16)
        out = xs + (0.5 * _gate(mod, sub)) * _dot(a, wd_ref[...])
        if final_norm:
            out = out * lax.rsqrt(jnp.mean(out * out, axis=-1, keepdims=True) + EPS) * fg
        outs.append(out)
    return jnp.concatenate(outs, axis=0)


def _pool_mixer(u, halo, first_row, wp_ref, scale):
    tm = u.shape[0]
    ext = jnp.concatenate([halo, u], axis=0)
    pos = first_row + lax.broadcasted_iota(jnp.int32, (tm, 1), 0)
    gw = wp_ref.shape[-1]
    sums = ext
    shift = 1
    parts = []
    for g, w in enumerate(POOL_WINDOWS):
        while shift < w:
            sums = sums + pltpu.roll(sums, shift, 0)
            shift *= 2
        total = sums[POOL_HALO:, g * gw:(g + 1) * gw]
        count = jnp.minimum(pos + 1, w).astype(F32)
        pooled = total / count - u[:, g * gw:(g + 1) * gw]
        parts.append(_dot(pooled.astype(BF16), wp_ref[g]))
    return (jnp.concatenate(parts, axis=1) * scale).astype(BF16)


def _mixed_input(mix, x_ref, mod, mix_refs):
    if mix is None:
        return x_ref[0]
    if mix == "odd":
        y_ref, wo_ref = mix_refs
        return x_ref[0] + _gate(mod, 1) * _dot(y_ref[0], wo_ref[...])
    u_ref, halo_ref, yd_ref, wp_ref, ps_ref, wo_ref = mix_refs
    i = pl.program_id(1)
    tm = u_ref.shape[1]
    halo = halo_ref[0] * (i > 0).astype(F32)
    y_pool = _pool_mixer(u_ref[0], halo, i * tm, wp_ref, ps_ref[...])
    pw = y_pool.shape[1]
    y = _dot(y_pool, wo_ref[:pw, :]) + _dot(yd_ref[0], wo_ref[pw:, :])
    return x_ref[0] + _gate(mod, 1) * y


_N_MIX_REFS = {None: 0, "odd": 2, "even": 6}


def _even_projection(h, w_ref, wt_ref, u_ref, k_ref, qt_ref, vt_ref):
    p = _dot(h, w_ref[...])
    pw = u_ref.shape[-1]
    u_ref[0] = p[:, :pw]
    k_ref[0] = p[:, pw:].astype(BF16)
    pt = lax.dot_general(wt_ref[...], h, (((1,), (1,)), ((), ())), preferred_element_type=F32)
    qw = qt_ref.shape[1]
    qt_ref[0] = (pt[:qw] * (DIFF_HEAD_DIM ** -0.5 * LOG2_E)).astype(BF16)
    vt_ref[0, 0] = pt[qw:].astype(BF16)


def _odd_projection(h, w_ref, wgate_ref, q_ref, k_ref, v_ref, o_ref, gates_ref):
    p = _dot(h, w_ref[...])
    nq = q_ref.shape[-1]
    nv = v_ref.shape[-1]
    q_ref[0] = p[:, :nq].astype(BF16)
    k_ref[0] = (p[:, nq:2 * nq] * (MLSTM_QK_DIM ** -0.5)).astype(BF16)
    v_ref[0] = p[:, 2 * nq:2 * nq + nv].astype(BF16)
    o_ref[0] = jax.nn.sigmoid(p[:, 2 * nq + nv:]).astype(BF16)
    gates_ref[0] = _dot(h, wgate_ref[...]).T[:gates_ref.shape[1]]


def _ffn_kernel(*refs, sub, final_norm, mix, n_cast, proj):
    x_ref, mod_ref, g_ref, wg_ref, wu_ref, wd_ref, fg_ref = refs[:7]
    pos = 7
    mix_refs = refs[pos:pos + _N_MIX_REFS[mix]]
    pos += len(mix_refs)
    cast_in = refs[pos:pos + n_cast]
    pos += n_cast
    proj_in = refs[pos:pos + (3 if proj else 0)]
    pos += len(proj_in)
    o_ref = refs[pos]
    cast_out = refs[pos + 1:pos + 1 + n_cast]
    proj_out = refs[pos + 1 + n_cast:]
    mod = mod_ref[0, 0]
    x = _mixed_input(mix, x_ref, mod, mix_refs)
    out = _swiglu_residual(x, mod, sub, g_ref[...], wg_ref, wu_ref, wd_ref, fg_ref[...], final_norm)
    o_ref[0] = out
    for src_ref, dst_ref in zip(cast_in, cast_out):
        dst_ref[...] = src_ref[(0,) * (len(src_ref.shape) - 2)].astype(BF16)
    if proj:
        g2_ref, w_a_ref, w_b_ref = proj_in
        h = _modnorm(out, g2_ref[...], mod, 1).astype(BF16)
        (_even_projection if proj == "even" else _odd_projection)(h, w_a_ref, w_b_ref, *proj_out)


def _ffn(x, mod, layer, sub, g, weights, final_g, final_norm, mix=None, casts=(), proj=None):
    b, s, d = x.shape
    f = weights[0].shape[-1]
    tm = FFN_ROWS
    n_i = s // tm
    row = lambda bi, i: (bi, i, 0)
    const2 = lambda bi, i: (0, 0)
    in_specs = [pl.BlockSpec((1, tm, d), row),
                pl.BlockSpec((1, 1, 3 * N_SUB, d), lambda bi, i: (layer, bi, 0, 0)),
                pl.BlockSpec((1, d), const2),
                _resident((d, f), const2),
                _resident((d, f), const2),
                _resident((f, d), const2),
                pl.BlockSpec((1, d), const2)]
    args = [x, mod, g.reshape(1, d), *weights, final_g.reshape(1, d)]
    kind = None if mix is None else mix[0]
    if kind == "odd":
        _, y, w_out = mix
        in_specs += [pl.BlockSpec((1, tm, y.shape[-1]), row), _resident(w_out.shape, const2)]
        args += [y, w_out.astype(BF16)]
    elif kind == "even":
        _, u, y_diff, w_pool, pool_scale, w_out = mix
        pw = u.shape[-1]
        halo_blocks = tm // POOL_HALO
        in_specs += [pl.BlockSpec((1, tm, pw), row),
                     pl.BlockSpec((1, POOL_HALO, pw),
                                  lambda bi, i: (bi, jnp.maximum(i * halo_blocks - 1, 0), 0)),
                     pl.BlockSpec((1, tm, y_diff.shape[-1]), row),
                     _resident(w_pool.shape, lambda bi, i: (0, 0, 0)),
                     pl.BlockSpec((1, pw), const2),
                     _resident(w_out.shape, const2)]
        args += [u, u, y_diff, w_pool.astype(BF16), pool_scale.reshape(1, pw), w_out.astype(BF16)]
    out_specs = [pl.BlockSpec((1, tm, d), row)]
    out_shape = [jax.ShapeDtypeStruct((b, s, d), F32)]
    steps = b * n_i
    assert steps % FFN_CAST_CHUNKS == 0
    chunk_of = lambda bi, i: (bi * n_i + i) // (steps // FFN_CAST_CHUNKS)
    for w, lead in casts:
        rows, cols = w.shape[-2:]
        assert rows % (FFN_CAST_CHUNKS * BF16_SUBLANES) == 0
        chunk = rows // FFN_CAST_CHUNKS
        in_specs.append(pl.BlockSpec((1,) * len(lead) + (chunk, cols),
                                     lambda bi, i, lead=lead: (*lead, chunk_of(bi, i), 0)))
        args.append(w)
        out_specs.append(pl.BlockSpec((chunk, cols), lambda bi, i: (chunk_of(bi, i), 0)))
        out_shape.append(jax.ShapeDtypeStruct((rows, cols), BF16))
    proj_kind = None if proj is None else proj[0]
    if proj_kind == "even":
        _, g2, w_in, pool_width, qk_width = proj
        assert tm == ATTN_ROWS
        v_width = w_in.shape[1] - pool_width - 2 * qk_width
        k0 = pool_width + qk_width
        w_uk = jnp.concatenate([w_in[:, :pool_width], w_in[:, k0:k0 + qk_width]], axis=1).astype(BF16)
        w_qv_t = jnp.concatenate([w_in[:, pool_width:k0], w_in[:, k0 + qk_width:]], axis=1).T.astype(BF16)
        in_specs += [pl.BlockSpec((1, d), const2), _resident(w_uk.shape, const2),
                     _resident(w_qv_t.shape, const2)]
        args += [g2.reshape(1, d), w_uk, w_qv_t]
        out_specs += [pl.BlockSpec((1, tm, pool_width), row),
                      pl.BlockSpec((1, tm, qk_width), row),
                      pl.BlockSpec((1, qk_width, tm), lambda bi, i: (bi, 0, i)),
                      pl.BlockSpec((1, 1, v_width, tm), lambda bi, i: (bi, i, 0, 0))]
        out_shape += [jax.ShapeDtypeStruct((b, s, pool_width), F32),
                      jax.ShapeDtypeStruct((b, s, qk_width), BF16),
                      jax.ShapeDtypeStruct((b, qk_width, s), BF16),
                      jax.ShapeDtypeStruct((b, s // tm, v_width, tm), BF16)]
    elif proj_kind == "odd":
        _, g2, w_in, nq, nv = proj
        n_main = 2 * nq + 2 * nv
        n_gates = w_in.shape[1] - n_main
        w_gate = jnp.pad(w_in[:, n_main:], ((0, 0), (0, LANES - n_gates)))
        in_specs += [pl.BlockSpec((1, d), const2), _resident((d, n_main), const2),
                     _resident((d, LANES), const2)]
        args += [g2.reshape(1, d), w_in, w_gate]
        out_specs += [pl.BlockSpec((1, tm, nq), row),
                      pl.BlockSpec((1, tm, nq), row),
                      pl.BlockSpec((1, tm, nv), row),
                      pl.BlockSpec((1, tm, nv), row),
                      pl.BlockSpec((1, n_gates, tm), lambda bi, i: (bi, 0, i))]
        out_shape += [jax.ShapeDtypeStruct((b, s, nq), BF16),
                      jax.ShapeDtypeStruct((b, s, nq), BF16),
                      jax.ShapeDtypeStruct((b, s, nv), BF16),
                      jax.ShapeDtypeStruct((b, s, nv), BF16),
                      jax.ShapeDtypeStruct((b, n_gates, s), F32)]
    name = "ffn" if kind is None else kind + "_mix_ffn"
    outs = pl.pallas_call(
        functools.partial(_ffn_kernel, sub=sub, final_norm=final_norm, mix=kind,
                          n_cast=len(casts), proj=proj_kind),
        grid=(b, n_i),
        in_specs=in_specs,
        out_specs=out_specs,
        out_shape=out_shape,
        compiler_params=_params("arbitrary", "arbitrary"),
        name=name if proj is None else name + "_" + proj_kind + "_proj",
    )(*args)
    n_cast = len(casts)
    return outs[0], tuple(outs[1:1 + n_cast]), tuple(outs[1 + n_cast:])


def _diff_attn_kernel(qt_ref, k_ref, vt_ref, lam_ref, sg_ref, o_ref, m_ref, l_ref, acc_ref,
                      s0_ref, s1_ref, cm0_ref, cm1_ref, *, lambda_init, t):
    for part in range(ATTN_TILES_PER_STEP):
        rows = slice(part * t, (part + 1) * t)
        _diff_attn_query_tile(pl.program_id(2) * ATTN_TILES_PER_STEP + part, qt_ref[0, :, rows], k_ref,
                              vt_ref, lam_ref, sg_ref, o_ref.at[0, rows, :], m_ref, l_ref, acc_ref,
                              (s0_ref, s1_ref), (cm0_ref, cm1_ref), lambda_init, t)


def _diff_attn_query_tile(i, qt, k_ref, vt_ref, lam_ref, sg_ref, o_ref, m_ref, l_ref, acc_ref, s_refs,
                          cm_refs, lambda_init, t):
    qt = qt.astype(F32)
    feat = lax.broadcasted_iota(jnp.int32, qt.shape, 0)
    qs = jnp.concatenate([jnp.where(feat < DIFF_HEAD_DIM, qt, 0.0),
                          jnp.where(feat >= DIFF_HEAD_DIM, qt, 0.0)], axis=1).astype(BF16)
    m_ref[...] = jnp.full_like(m_ref, -jnp.inf)
    l_ref[...] = jnp.zeros_like(l_ref)
    acc_ref[...] = jnp.zeros_like(acc_ref)
    groups = [slice(c, c + ATTN_COL_GROUP) for c in range(0, 2 * t, ATTN_COL_GROUP)]

    def produce(j, slot, cols):
        kb = k_ref[0, pl.ds(pl.multiple_of(j * t, t), t), :]
        s = _dot(kb, qs[:, cols])
        s_refs[slot][:, cols] = s
        cm_refs[slot][:, cols] = jnp.max(s, axis=0, keepdims=True)

    def consume(j, slot, cols, diagonal):
        if diagonal:
            first_q = cols.start % t
            n_keys = first_q + ATTN_COL_GROUP
            s = s_refs[slot][:n_keys, cols]
            key = lax.broadcasted_iota(jnp.int32, s.shape, 0)
            qry = lax.broadcasted_iota(jnp.int32, s.shape, 1) + first_q
            s = jnp.where(key <= qry, s, -jnp.inf)
            cm = jnp.max(s, axis=0, keepdims=True)
            vt = vt_ref[0, j][:, :n_keys]
        else:
            s = s_refs[slot][:, cols]
            cm = cm_refs[slot][:, cols]
            vt = vt_ref[0, j]
        m = m_ref[:, cols]
        m_new = jnp.maximum(m, cm)
        alpha = jnp.exp2(m - m_new)
        p = jnp.exp2(s - m_new)
        l_ref[:, cols] = alpha * l_ref[:, cols] + jnp.sum(p, axis=0, keepdims=True)
        acc_ref[:, cols] = alpha * acc_ref[:, cols] + _dot(vt, p.astype(BF16))
        m_ref[:, cols] = m_new

    def advance(j, slot):
        for cols in groups:
            produce(j + 1, 1 - slot, cols)
            consume(j, slot, cols, False)

    rest = i % ATTN_UNROLL
    for r in range(ATTN_UNROLL):
        @pl.when(rest == r)
        def _(r=r):
            for cols in groups:
                produce(0, r % 2, cols)
            for j in range(r):
                advance(j, (j - r) % 2)

    def body(jj, carry):
        for d in range(ATTN_UNROLL):
            advance(rest + ATTN_UNROLL * jj + d, d % 2)
        return carry

    lax.fori_loop(0, i // ATTN_UNROLL, body, 0)
    for cols in groups:
        consume(i, 0, cols, True)

    o = acc_ref[...] / l_ref[...]
    lp = lam_ref[...]
    lam = (jnp.exp(jnp.sum(lp[0:1] * lp[1:2], axis=1, keepdims=True))
           - jnp.exp(jnp.sum(lp[2:3] * lp[3:4], axis=1, keepdims=True)) + lambda_init)
    o = o[:, :t] - lam * o[:, t:]
    o = o * lax.rsqrt(jnp.mean(o * o, axis=0, keepdims=True) + EPS) * sg_ref[...]
    o_ref[...] = (o * (1.0 - lambda_init)).T.astype(o_ref.dtype)


def _diff_attention(qt, k, vt, lam_params, subln_g, lambda_init):
    b, s, _ = k.shape
    hd = 2 * DIFF_HEAD_DIM
    t = ATTN_ROWS
    tq = ATTN_TILES_PER_STEP * t
    assert s % tq == 0 and vt.shape[-1] == t and (2 * t) % ATTN_COL_GROUP == 0
    return pl.pallas_call(
        functools.partial(_diff_attn_kernel, lambda_init=lambda_init, t=t),
        grid=(b, DIFF_HEADS, s // tq),
        in_specs=[pl.BlockSpec((1, hd, tq), lambda bi, h, i: (bi, h, i)),
                  pl.BlockSpec((1, s, hd), lambda bi, h, i: (bi, 0, h)),
                  pl.BlockSpec((1, s // t, hd, t), lambda bi, h, i: (bi, 0, h, 0)),
                  pl.BlockSpec(lam_params.shape, lambda bi, h, i: (0, 0)),
                  pl.BlockSpec((hd, 1), lambda bi, h, i: (0, 0))],
        out_specs=pl.BlockSpec((1, tq, hd), lambda bi, h, i: (bi, i, h)),
        out_shape=jax.ShapeDtypeStruct((b, s, DIFF_HEADS * hd), BF16),
        scratch_shapes=[pltpu.VMEM((1, 2 * t), F32), pltpu.VMEM((1, 2 * t), F32),
                        pltpu.VMEM((hd, 2 * t), F32),
                        pltpu.VMEM((t, 2 * t), F32), pltpu.VMEM((t, 2 * t), F32),
                        pltpu.VMEM((1, 2 * t), F32), pltpu.VMEM((1, 2 * t), F32)],
        compiler_params=_params("arbitrary", "arbitrary", "arbitrary"),
        name="diff_attention",
    )(qt, k, vt, lam_params, subln_g.reshape(hd, 1))


def _log_sigmoid(x):
    return jnp.minimum(x, 0.0) - jnp.log1p(jnp.exp(-jnp.abs(x)))


def _lane_scan(x, combine, fill):
    lane = lax.broadcasted_iota(jnp.int32, x.shape, 1)
    shift = 1
    while shift < x.shape[-1]:
        x = combine(x, jnp.where(lane >= shift, pltpu.roll(x, shift, 1), fill))
        shift *= 2
    return x


def _mlstm_kernel(bg_ref, q_ref, k_ref, v_ref, og_ref, ig_ref, fg_ref, ig_next_ref, fg_next_ref,
                  ng_ref, y_ref, c_ref, m_ref, cols_ref, src_ref, w_ref, sold_ref, *, chunks):
    head = pl.program_id(1)
    step = pl.program_id(2)
    L = MLSTM_CHUNK
    G = chunks

    def store_gate_stats(i_ref, f_ref):
        i_all = i_ref[0, 0] + bg_ref[head]
        logf = _log_sigmoid(f_ref[0, 0] + bg_ref[MLSTM_HEADS + head])
        b_all = _lane_scan(logf, jnp.add, 0.0)
        b_last = b_all[:, L - 1:L]
        a_all = b_last - b_all + i_all
        m_loc = jnp.max(a_all, axis=1, keepdims=True)

        m_in = []
        m = m_ref[...]
        for g in range(G):
            m_in.append(m)
            m = jnp.maximum(b_last[g:g + 1] + m, m_loc[g:g + 1])
        m_out = m_in[1:] + [m]
        m_ref[...] = m
        m_prev = jnp.concatenate(m_in, axis=0)
        m_next = jnp.concatenate(m_out, axis=0)
        sold_ref[...] = jnp.exp(b_last + m_prev - m_next)
        w_ref[...] = jnp.exp(a_all - m_next)

        src = i_all - b_all
        src_ref[...] = src
        inter_log = b_all + m_prev
        m_t = jnp.maximum(inter_log, b_all + _lane_scan(src, jnp.maximum, -jnp.inf))
        per_row = jnp.concatenate([b_all - m_t, jnp.exp(inter_log - m_t), jnp.exp(-m_t)], axis=0)
        per_row = jnp.concatenate([per_row, jnp.zeros((LANES - 3 * G, L), F32)], axis=0)
        cols_ref[...] = per_row.T

    @pl.when(step == 0)
    def _():
        c_ref[...] = jnp.zeros_like(c_ref)
        m_ref[...] = jnp.zeros_like(m_ref)
        store_gate_stats(ig_ref, fg_ref)

    cols = cols_ref[...]
    src = src_ref[...]
    w_all = w_ref[...]
    s_old = sold_ref[...]
    store_gate_stats(ig_next_ref, fg_next_ref)

    causal = (lax.broadcasted_iota(jnp.int32, (L, L), 1) <= lax.broadcasted_iota(jnp.int32, (L, L), 0))
    dv = v_ref.shape[-1]
    ones_col = (lax.broadcasted_iota(jnp.int32, (L, LANES), 1) == 0).astype(BF16)
    state = c_ref[...]
    for g in range(G):
        rows = slice(g * L, (g + 1) * L)
        q = q_ref[0, rows, :]
        k = k_ref[0, rows, :]
        v_ext = jnp.concatenate([v_ref[0, rows, :], ones_col], axis=1)
        dest = cols[:, g:g + 1]
        inter_w = cols[:, G + g:G + g + 1]
        floor = cols[:, 2 * G + g:2 * G + g + 1]

        dw = jnp.exp(jnp.where(causal, dest + src[g:g + 1], -jnp.inf))
        s = lax.dot_general(q, k, (((1,), (1,)), ((), ())), preferred_element_type=F32) * dw
        q_w = (inter_w * q.astype(F32)).astype(BF16)
        nd = _dot(s.astype(BF16), v_ext) + _dot(q_w, state.astype(BF16))
        num = nd[:, :dv]
        den = nd[:, dv:dv + 1]
        r = 1.0 / jnp.maximum(jnp.abs(den), floor)
        scale = r * lax.rsqrt(r * r * jnp.mean(num * num, axis=-1, keepdims=True) + EPS)
        hn = num * scale * ng_ref[...]
        y_ref[0, rows, :] = (og_ref[0, rows, :].astype(F32) * hn).astype(y_ref.dtype)

        kw_t = (k.astype(F32).T * w_all[g:g + 1]).astype(BF16)
        state = s_old[g:g + 1] * state + _dot(kw_t, v_ext)
    c_ref[...] = state


def _mlstm(q, k, v, o, gates, b_gates, norm_g):
    b, s, _ = q.shape
    dk, dv, L = MLSTM_QK_DIM, MLSTM_V_DIM, MLSTM_CHUNK
    nc = s // L
    g = MLSTM_CHUNKS_PER_STEP
    rows = g * L
    gates = gates.reshape(b, 2 * MLSTM_HEADS, nc, L)
    last = nc // g - 1

    def gate_spec(first, ahead):
        return pl.BlockSpec((1, 1, g, L),
                            lambda bi, h, c: (bi, first + h, jnp.minimum(c + ahead, last), 0))

    return pl.pallas_call(
        functools.partial(_mlstm_kernel, chunks=g),
        grid=(b, MLSTM_HEADS, nc // g),
        in_specs=[pl.BlockSpec(memory_space=pltpu.SMEM),
                  pl.BlockSpec((1, rows, dk), lambda bi, h, c: (bi, c, h)),
                  pl.BlockSpec((1, rows, dk), lambda bi, h, c: (bi, c, h)),
                  pl.BlockSpec((1, rows, dv), lambda bi, h, c: (bi, c, h)),
                  pl.BlockSpec((1, rows, dv), lambda bi, h, c: (bi, c, h)),
                  gate_spec(0, 0), gate_spec(MLSTM_HEADS, 0),
                  gate_spec(0, 1), gate_spec(MLSTM_HEADS, 1),
                  pl.BlockSpec((1, dv), lambda bi, h, c: (0, h))],
        out_specs=pl.BlockSpec((1, rows, dv), lambda bi, h, c: (bi, c, h)),
        out_shape=jax.ShapeDtypeStruct((b, s, MLSTM_HEADS * dv), BF16),
        scratch_shapes=[pltpu.VMEM((dk, dv + LANES), F32), pltpu.VMEM((1, 1), F32),
                        pltpu.VMEM((L, LANES), F32), pltpu.VMEM((g, L), F32), pltpu.VMEM((g, L), F32),
                        pltpu.VMEM((g, 1), F32)],
        compiler_params=_params("arbitrary", "arbitrary", "arbitrary"),
        name="mlstm",
    )(b_gates.reshape(-1), q, k, v, o, gates, gates, gates, gates, norm_g.reshape(1, -1))


def kernel(x, c, w_mod, b_mod, norm_g, w_ffn_gate, w_ffn_up, w_ffn_down, w_in_even, w_pool, pool_scale, diff_lambda, diff_subln_g, w_out_even, w_in_odd, b_gates_odd, mlstm_norm_g, w_out_odd, final_g):
    depth = w_mod.shape[0]
    mod = _modulation(c, w_mod, b_mod)
    pool_width = pool_scale.shape[-1]
    qk_width = DIFF_HEADS * 2 * DIFF_HEAD_DIM
    ffn_stacked = (w_ffn_gate, w_ffn_up, w_ffn_down)
    ffn_w = tuple(w[0, 0].astype(BF16) for w in ffn_stacked)
    w_in_odd_bf16 = None
    for l in range(depth):
        if l % 2 == 0:
            proj = ("even", norm_g[l, 1], w_in_even[l // 2], pool_width, qk_width)
        else:
            proj = ("odd", norm_g[l, 1], w_in_odd_bf16,
                    MLSTM_HEADS * MLSTM_QK_DIM, MLSTM_HEADS * MLSTM_V_DIM)
        x, cast, mixer_in = _ffn(x, mod, l, 0, norm_g[l, 0], ffn_w, final_g, False,
                                 casts=[(w, (l, 1)) for w in ffn_stacked], proj=proj)
        ffn_w = cast[:3]
        if l % 2 == 0:
            e = l // 2
            lambda_init = 0.8 - 0.6 * math.exp(-0.3 * l)
            u, k, qt, vt = mixer_in
            y_diff = _diff_attention(qt, k, vt, diff_lambda[e], diff_subln_g[e], lambda_init)
            mix = ("even", u, y_diff, w_pool[e], pool_scale[e], w_out_even[e])
        else:
            o = l // 2
            q, k, v, og, gates = mixer_in
            y = _mlstm(q, k, v, og, gates, b_gates_odd[o], mlstm_norm_g[o])
            mix = ("odd", y, w_out_odd[o])
        last = l == depth - 1
        casts = [] if last else [(w, (l + 1, 0)) for w in ffn_stacked]
        if not last and (l + 1) % 2 == 1:
            casts.append((w_in_odd, ((l + 1) // 2,)))
        x, cast, _ = _ffn(x, mod, l, 2, norm_g[l, 2], ffn_w, final_g, last, mix, casts=casts)
        ffn_w = cast[:3]
        w_in_odd_bf16 = cast[3] if len(cast) > 3 else None
    return x
```

```python
import functools
import math

import jax
import jax.numpy as jnp
from jax import lax
from jax.experimental import pallas as pl
from jax.experimental.pallas import tpu as pltpu

F32 = jnp.float32
BF16 = jnp.bfloat16

EPS = 1e-6
N_SUB = 3
POOL_WINDOWS = (2, 4, 8, 16)
POOL_HALO = 16
DIFF_HEADS = 4
DIFF_HEAD_DIM = 64
MLSTM_HEADS = 4
MLSTM_QK_DIM = 128
MLSTM_V_DIM = 256
MLSTM_CHUNK = 128
LANES = 128
BF16_SUBLANES = 16

VMEM_LIMIT_BYTES = 56 * 1024 * 1024

MOD_ROWS = 256
FFN_ROWS = 512
FFN_CAST_CHUNKS = 16
FFN_SUB_ROWS = 256
ATTN_ROWS = 512
ATTN_COL_GROUP = 256
ATTN_TILES_PER_STEP = 2
ATTN_UNROLL = 4
LOG2_E = math.log2(math.e)
MLSTM_CHUNKS_PER_STEP = 32


def _params(*semantics):
    return pltpu.CompilerParams(dimension_semantics=semantics,
                                vmem_limit_bytes=VMEM_LIMIT_BYTES)


def _resident(block_shape, index_map):
    return pl.BlockSpec(block_shape, index_map, pipeline_mode=pl.Buffered(1))


def _dot(a, b):
    return jnp.dot(a, b, preferred_element_type=F32)


def _modnorm(x, g, mod, sub):
    shift = mod[3 * sub:3 * sub + 1]
    scale = mod[3 * sub + 1:3 * sub + 2]
    y = x * lax.rsqrt(jnp.mean(x * x, axis=-1, keepdims=True) + EPS)
    return (y * g) * (1.0 + scale) + shift


def _gate(mod, sub):
    return mod[3 * sub + 2:3 * sub + 3]


def _mod_kernel(c_ref, w_ref, b_ref, o_ref):
    @pl.when(pl.program_id(1) == 0)
    def _():
        o_ref[0] = jnp.broadcast_to(b_ref[0], o_ref.shape[1:])

    c = c_ref[...]
    c_act = c * jax.nn.sigmoid(c)
    w = w_ref[0]
    c_hi = c_act.astype(BF16)
    c_lo = (c_act - c_hi.astype(F32)).astype(BF16)
    w_hi = w.astype(BF16)
    w_lo = (w - w_hi.astype(F32)).astype(BF16)
    o_ref[0] += _dot(c_hi, w_hi) + (_dot(c_hi, w_lo) + _dot(c_lo, w_hi))


def _modulation(c, w_mod, b_mod):
    depth, d, n = w_mod.shape
    b = c.shape[0]
    tk = MOD_ROWS
    out = pl.pallas_call(
        _mod_kernel,
        grid=(depth, d // tk),
        in_specs=[pl.BlockSpec((b, tk), lambda l, j: (0, j)),
                  pl.BlockSpec((1, tk, n), lambda l, j: (l, j, 0)),
                  pl.BlockSpec((1, 1, n), lambda l, j: (l, 0, 0))],
        out_specs=pl.BlockSpec((1, b, n), lambda l, j: (l, 0, 0)),
        out_shape=jax.ShapeDtypeStruct((depth, b, n), F32),
        compiler_params=_params("arbitrary", "arbitrary"),
        name="modulation",
    )(c, w_mod, b_mod.reshape(depth, 1, n))
    return out.reshape(depth, b, 3 * N_SUB, d)


def _swiglu_residual(x, mod, sub, g, wg_ref, wu_ref, wd_ref, fg, final_norm):
    outs = []
    for r in range(0, x.shape[0], FFN_SUB_ROWS):
        xs = x[r:r + FFN_SUB_ROWS]
        h = _modnorm(xs, g, mod, sub).astype(BF16)
        gate_act = _dot(h, wg_ref[...])
        up = _dot(h, wu_ref[...])
        a = (gate_act * jax.nn.sigmoid(gate_act) * up).astype(BF16)
        out = xs + (0.5 * _gate(mod, sub)) * _dot(a, wd_ref[...])
        if final_norm:
            out = out * lax.rsqrt(jnp.mean(out * out, axis=-1, keepdims=True) + EPS) * fg
        outs.append(out)
    return jnp.concatenate(outs, axis=0)


def _pool_mixer(u, halo, first_row, wp_ref, scale):
    tm = u.shape[0]
    ext = jnp.concatenate([halo, u], axis=0)
    pos = first_row + lax.broadcasted_iota(jnp.int32, (tm, 1), 0)
    gw = wp_ref.shape[-1]
    sums = ext
    shift = 1
    parts = []
    for g, w in enumerate(POOL_WINDOWS):
        while shift < w:
            sums = sums + pltpu.roll(sums, shift, 0)
            shift *= 2
        total = sums[POOL_HALO:, g * gw:(g + 1) * gw]
        count = jnp.minimum(pos + 1, w).astype(F32)
        pooled = total / count - u[:, g * gw:(g + 1) * gw]
        parts.append(_dot(pooled.astype(BF16), wp_ref[g]))
    return (jnp.concatenate(parts, axis=1) * scale).astype(BF16)


def _mixed_input(mix, x_ref, mod, mix_refs):
    if mix is None:
        return x_ref[0]
    if mix == "odd":
        y_ref, wo_ref = mix_refs
        return x_ref[0] + _gate(mod, 1) * _dot(y_ref[0], wo_ref[...])
    u_ref, halo_ref, yd_ref, wp_ref, ps_ref, wo_ref = mix_refs
    i = pl.program_id(1)
    tm = u_ref.shape[1]
    halo = halo_ref[0] * (i > 0).astype(F32)
    y_pool = _pool_mixer(u_ref[0], halo, i * tm, wp_ref, ps_ref[...])
    pw = y_pool.shape[1]
    y = _dot(y_pool, wo_ref[:pw, :]) + _dot(yd_ref[0], wo_ref[pw:, :])
    return x_ref[0] + _gate(mod, 1) * y


_N_MIX_REFS = {None: 0, "odd": 2, "even": 6}


def _even_projection(h, w_ref, wt_ref, u_ref, k_ref, qt_ref, vt_ref):
    p = _dot(h, w_ref[...])
    pw = u_ref.shape[-1]
    u_ref[0] = p[:, :pw]
    k_ref[0] = p[:, pw:].astype(BF16)
    pt = lax.dot_general(wt_ref[...], h, (((1,), (1,)), ((), ())), preferred_element_type=F32)
    qw = qt_ref.shape[1]
    qt_ref[0] = (pt[:qw] * (DIFF_HEAD_DIM ** -0.5 * LOG2_E)).astype(BF16)
    vt_ref[0, 0] = pt[qw:].astype(BF16)


def _odd_projection(h, w_ref, wgate_ref, q_ref, k_ref, v_ref, o_ref, gates_ref):
    p = _dot(h, w_ref[...])
    nq = q_ref.shape[-1]
    nv = v_ref.shape[-1]
    q_ref[0] = p[:, :nq].astype(BF16)
    k_ref[0] = (p[:, nq:2 * nq] * (MLSTM_QK_DIM ** -0.5)).astype(BF16)
    v_ref[0] = p[:, 2 * nq:2 * nq + nv].astype(BF16)
    o_ref[0] = jax.nn.sigmoid(p[:, 2 * nq + nv:]).astype(BF16)
    gates_ref[0] = _dot(h, wgate_ref[...]).T[:gates_ref.shape[1]]


def _ffn_kernel(*refs, sub, final_norm, mix, n_cast, proj):
    x_ref, mod_ref, g_ref, wg_ref, wu_ref, wd_ref, fg_ref = refs[:7]
    pos = 7
    mix_refs = refs[pos:pos + _N_MIX_REFS[mix]]
    pos += len(mix_refs)
    cast_in = refs[pos:pos + n_cast]
    pos += n_cast
    proj_in = refs[pos:pos + (3 if proj else 0)]
    pos += len(proj_in)
    o_ref = refs[pos]
    cast_out = refs[pos + 1:pos + 1 + n_cast]
    proj_out = refs[pos + 1 + n_cast:]
    mod = mod_ref[0, 0]
    x = _mixed_input(mix, x_ref, mod, mix_refs)
    out = _swiglu_residual(x, mod, sub, g_ref[...], wg_ref, wu_ref, wd_ref, fg_ref[...], final_norm)
    o_ref[0] = out
    for src_ref, dst_ref in zip(cast_in, cast_out):
        dst_ref[...] = src_ref[(0,) * (len(src_ref.shape) - 2)].astype(BF16)
    if proj:
        g2_ref, w_a_ref, w_b_ref = proj_in
        h = _modnorm(out, g2_ref[...], mod, 1).astype(BF16)
        (_even_projection if proj == "even" else _odd_projection)(h, w_a_ref, w_b_ref, *proj_out)


def _ffn(x, mod, layer, sub, g, weights, final_g, final_norm, mix=None, casts=(), proj=None):
    b, s, d = x.shape
    f = weights[0].shape[-1]
    tm = FFN_ROWS
    n_i = s // tm
    row = lambda bi, i: (bi, i, 0)
    const2 = lambda bi, i: (0, 0)
    in_specs = [pl.BlockSpec((1, tm, d), row),
                pl.BlockSpec((1, 1, 3 * N_SUB, d), lambda bi, i: (layer, bi, 0, 0)),
                pl.BlockSpec((1, d), const2),
                _resident((d, f), const2),
                _resident((d, f), const2),
                _resident((f, d), const2),
                pl.BlockSpec((1, d), const2)]
    args = [x, mod, g.reshape(1, d), *weights, final_g.reshape(1, d)]
    kind = None if mix is None else mix[0]
    if kind == "odd":
        _, y, w_out = mix
        in_specs += [pl.BlockSpec((1, tm, y.shape[-1]), row), _resident(w_out.shape, const2)]
        args += [y, w_out.astype(BF16)]
    elif kind == "even":
        _, u, y_diff, w_pool, pool_scale, w_out = mix
        pw = u.shape[-1]
        halo_blocks = tm // POOL_HALO
        in_specs += [pl.BlockSpec((1, tm, pw), row),
                     pl.BlockSpec((1, POOL_HALO, pw),
                                  lambda bi, i: (bi, jnp.maximum(i * halo_blocks - 1, 0), 0)),
                     pl.BlockSpec((1, tm, y_diff.shape[-1]), row),
                     _resident(w_pool.shape, lambda bi, i: (0, 0, 0)),
                     pl.BlockSpec((1, pw), const2),
                     _resident(w_out.shape, const2)]
        args += [u, u, y_diff, w_pool.astype(BF16), pool_scale.reshape(1, pw), w_out.astype(BF16)]
    out_specs = [pl.BlockSpec((1, tm, d), row)]
    out_shape = [jax.ShapeDtypeStruct((b, s, d), F32)]
    steps = b * n_i
    assert steps % FFN_CAST_CHUNKS == 0
    chunk_of = lambda bi, i: (bi * n_i + i) // (steps // FFN_CAST_CHUNKS)
    for w, lead in casts:
        rows, cols = w.shape[-2:]
        assert rows % (FFN_CAST_CHUNKS * BF16_SUBLANES) == 0
        chunk = rows // FFN_CAST_CHUNKS
        in_specs.append(pl.BlockSpec((1,) * len(lead) + (chunk, cols),
                                     lambda bi, i, lead=lead: (*lead, chunk_of(bi, i), 0)))
        args.append(w)
        out_specs.append(pl.BlockSpec((chunk, cols), lambda bi, i: (chunk_of(bi, i), 0)))
        out_shape.append(jax.ShapeDtypeStruct((rows, cols), BF16))
    proj_kind = None if proj is None else proj[0]
    if proj_kind == "even":
        _, g2, w_in, pool_width, qk_width = proj
        assert tm == ATTN_ROWS
        v_width = w_in.shape[1] - pool_width - 2 * qk_width
        k0 = pool_width + qk_width
        w_uk = jnp.concatenate([w_in[:, :pool_width], w_in[:, k0:k0 + qk_width]], axis=1).astype(BF16)
        w_qv_t = jnp.concatenate([w_in[:, pool_width:k0], w_in[:, k0 + qk_width:]], axis=1).T.astype(BF16)
        in_specs += [pl.BlockSpec((1, d), const2), _resident(w_uk.shape, const2),
                     _resident(w_qv_t.shape, const2)]
        args += [g2.reshape(1, d), w_uk, w_qv_t]
        out_specs += [pl.BlockSpec((1, tm, pool_width), row),
                      pl.BlockSpec((1, tm, qk_width), row),
                      pl.BlockSpec((1, qk_width, tm), lambda bi, i: (bi, 0, i)),
                      pl.BlockSpec((1, 1, v_width, tm), lambda bi, i: (bi, i, 0, 0))]
        out_shape += [jax.ShapeDtypeStruct((b, s, pool_width), F32),
                      jax.ShapeDtypeStruct((b, s, qk_width), BF16),
                      jax.ShapeDtypeStruct((b, qk_width, s), BF16),
                      jax.ShapeDtypeStruct((b, s // tm, v_width, tm), BF16)]
    elif proj_kind == "odd":
        _, g2, w_in, nq, nv = proj
        n_main = 2 * nq + 2 * nv
        n_gates = w_in.shape[1] - n_main
        w_gate = jnp.pad(w_in[:, n_main:], ((0, 0), (0, LANES - n_gates)))
        in_specs += [pl.BlockSpec((1, d), const2), _resident((d, n_main), const2),
                     _resident((d, LANES), const2)]
        args += [g2.reshape(1, d), w_in, w_gate]
        out_specs += [pl.BlockSpec((1, tm, nq), row),
                      pl.BlockSpec((1, tm, nq), row),
                      pl.BlockSpec((1, tm, nv), row),
                      pl.BlockSpec((1, tm, nv), row),
                      pl.BlockSpec((1, n_gates, tm), lambda bi, i: (bi, 0, i))]
        out_shape += [jax.ShapeDtypeStruct((b, s, nq), BF16),
                      jax.ShapeDtypeStruct((b, s, nq), BF16),
                      jax.ShapeDtypeStruct((b, s, nv), BF16),
                      jax.ShapeDtypeStruct((b, s, nv), BF16),
                      jax.ShapeDtypeStruct((b, n_gates, s), F32)]
    name = "ffn" if kind is None else kind + "_mix_ffn"
    outs = pl.pallas_call(
        functools.partial(_ffn_kernel, sub=sub, final_norm=final_norm, mix=kind,
                          n_cast=len(casts), proj=proj_kind),
        grid=(b, n_i),
        in_specs=in_specs,
        out_specs=out_specs,
        out_shape=out_shape,
        compiler_params=_params("arbitrary", "arbitrary"),
        name=name if proj is None else name + "_" + proj_kind + "_proj",
    )(*args)
    n_cast = len(casts)
    return outs[0], tuple(outs[1:1 + n_cast]), tuple(outs[1 + n_cast:])


def _diff_attn_kernel(qt_ref, k_ref, vt_ref, lam_ref, sg_ref, o_ref, m_ref, l_ref, acc_ref,
                      s0_ref, s1_ref, cm0_ref, cm1_ref, *, lambda_init, t):
    for part in range(ATTN_TILES_PER_STEP):
        rows = slice(part * t, (part + 1) * t)
        _diff_attn_query_tile(pl.program_id(2) * ATTN_TILES_PER_STEP + part, qt_ref[0, :, rows], k_ref,
                              vt_ref, lam_ref, sg_ref, o_ref.at[0, rows, :], m_ref, l_ref, acc_ref,
                              (s0_ref, s1_ref), (cm0_ref, cm1_ref), lambda_init, t)


def _diff_attn_query_tile(i, qt, k_ref, vt_ref, lam_ref, sg_ref, o_ref, m_ref, l_ref, acc_ref, s_refs,
                          cm_refs, lambda_init, t):
    qt = qt.astype(F32)
    feat = lax.broadcasted_iota(jnp.int32, qt.shape, 0)
    qs = jnp.concatenate([jnp.where(feat < DIFF_HEAD_DIM, qt, 0.0),
                          jnp.where(feat >= DIFF_HEAD_DIM, qt, 0.0)], axis=1).astype(BF16)
    m_ref[...] = jnp.full_like(m_ref, -jnp.inf)
    l_ref[...] = jnp.zeros_like(l_ref)
    acc_ref[...] = jnp.zeros_like(acc_ref)
    groups = [slice(c, c + ATTN_COL_GROUP) for c in range(0, 2 * t, ATTN_COL_GROUP)]

    def produce(j, slot, cols):
        kb = k_ref[0, pl.ds(pl.multiple_of(j * t, t), t), :]
        s = _dot(kb, qs[:, cols])
        s_refs[slot][:, cols] = s
        cm_refs[slot][:, cols] = jnp.max(s, axis=0, keepdims=True)

    def consume(j, slot, cols, diagonal):
        if diagonal:
            first_q = cols.start % t
            n_keys = first_q + ATTN_COL_GROUP
            s = s_refs[slot][:n_keys, cols]
            key = lax.broadcasted_iota(jnp.int32, s.shape, 0)
            qry = lax.broadcasted_iota(jnp.int32, s.shape, 1) + first_q
            s = jnp.where(key <= qry, s, -jnp.inf)
            cm = jnp.max(s, axis=0, keepdims=True)
            vt = vt_ref[0, j][:, :n_keys]
        else:
            s = s_refs[slot][:, cols]
            cm = cm_refs[slot][:, cols]
            vt = vt_ref[0, j]
        m = m_ref[:, cols]
        m_new = jnp.maximum(m, cm)
        alpha = jnp.exp2(m - m_new)
        p = jnp.exp2(s - m_new)
        l_ref[:, cols] = alpha * l_ref[:, cols] + jnp.sum(p, axis=0, keepdims=True)
        acc_ref[:, cols] = alpha * acc_ref[:, cols] + _dot(vt, p.astype(BF16))
        m_ref[:, cols] = m_new

    def advance(j, slot):
        for cols in groups:
            produce(j + 1, 1 - slot, cols)
            consume(j, slot, cols, False)

    rest = i % ATTN_UNROLL
    for r in range(ATTN_UNROLL):
        @pl.when(rest == r)
        def _(r=r):
            for cols in groups:
                produce(0, r % 2, cols)
            for j in range(r):
                advance(j, (j - r) % 2)

    def body(jj, carry):
        for d in range(ATTN_UNROLL):
            advance(rest + ATTN_UNROLL * jj + d, d % 2)
        return carry

    lax.fori_loop(0, i // ATTN_UNROLL, body, 0)
    for cols in groups:
        consume(i, 0, cols, True)

    o = acc_ref[...] / l_ref[...]
    lp = lam_ref[...]
    lam = (jnp.exp(jnp.sum(lp[0:1] * lp[1:2], axis=1, keepdims=True))
           - jnp.exp(jnp.sum(lp[2:3] * lp[3:4], axis=1, keepdims=True)) + lambda_init)
    o = o[:, :t] - lam * o[:, t:]
    o = o * lax.rsqrt(jnp.mean(o * o, axis=0, keepdims=True) + EPS) * sg_ref[...]
    o_ref[...] = (o * (1.0 - lambda_init)).T.astype(o_ref.dtype)


def _diff_attention(qt, k, vt, lam_params, subln_g, lambda_init):
    b, s, _ = k.shape
    hd = 2 * DIFF_HEAD_DIM
    t = ATTN_ROWS
    tq = ATTN_TILES_PER_STEP * t
    assert s % tq == 0 and vt.shape[-1] == t and (2 * t) % ATTN_COL_GROUP == 0
    return pl.pallas_call(
        functools.partial(_diff_attn_kernel, lambda_init=lambda_init, t=t),
        grid=(b, DIFF_HEADS, s // tq),
        in_specs=[pl.BlockSpec((1, hd, tq), lambda bi, h, i: (bi, h, i)),
                  pl.BlockSpec((1, s, hd), lambda bi, h, i: (bi, 0, h)),
                  pl.BlockSpec((1, s // t, hd, t), lambda bi, h, i: (bi, 0, h, 0)),
                  pl.BlockSpec(lam_params.shape, lambda bi, h, i: (0, 0)),
                  pl.BlockSpec((hd, 1), lambda bi, h, i: (0, 0))],
        out_specs=pl.BlockSpec((1, tq, hd), lambda bi, h, i: (bi, i, h)),
        out_shape=jax.ShapeDtypeStruct((b, s, DIFF_HEADS * hd), BF16),
        scratch_shapes=[pltpu.VMEM((1, 2 * t), F32), pltpu.VMEM((1, 2 * t), F32),
                        pltpu.VMEM((hd, 2 * t), F32),
                        pltpu.VMEM((t, 2 * t), F32), pltpu.VMEM((t, 2 * t), F32),
                        pltpu.VMEM((1, 2 * t), F32), pltpu.VMEM((1, 2 * t), F32)],
        compiler_params=_params("arbitrary", "arbitrary", "arbitrary"),
        name="diff_attention",
    )(qt, k, vt, lam_params, subln_g.reshape(hd, 1))


def _log_sigmoid(x):
    return jnp.minimum(x, 0.0) - jnp.log1p(jnp.exp(-jnp.abs(x)))


def _lane_scan(x, combine, fill):
    lane = lax.broadcasted_iota(jnp.int32, x.shape, 1)
    shift = 1
    while shift < x.shape[-1]:
        x = combine(x, jnp.where(lane >= shift, pltpu.roll(x, shift, 1), fill))
        shift *= 2
    return x


def _mlstm_kernel(bg_ref, q_ref, k_ref, v_ref, og_ref, ig_ref, fg_ref, ig_next_ref, fg_next_ref,
                  ng_ref, y_ref, c_ref, m_ref, cols_ref, src_ref, w_ref, sold_ref, *, chunks):
    head = pl.program_id(1)
    step = pl.program_id(2)
    L = MLSTM_CHUNK
    G = chunks
    wraps = step == pl.num_programs(2) - 1
    next_head = jnp.where(wraps, (head + 1) % MLSTM_HEADS, head)

    def store_gate_stats(i_ref, f_ref, hd, fresh):
        i_all = i_ref[0, 0] + bg_ref[hd]
        logf = _log_sigmoid(f_ref[0, 0] + bg_ref[MLSTM_HEADS + hd])
        b_all = _lane_scan(logf, jnp.add, 0.0)
        b_last = b_all[:, L - 1:L]
        a_all = b_last - b_all + i_all
        m_loc = jnp.max(a_all, axis=1, keepdims=True)

        m_in = []
        m = jnp.where(fresh, 0.0, m_ref[...])
        for g in range(G):
            m_in.append(m)
            m = jnp.maximum(b_last[g:g + 1] + m, m_loc[g:g + 1])
        m_out = m_in[1:] + [m]
        m_ref[...] = m
        m_prev = jnp.concatenate(m_in, axis=0)
        m_next = jnp.concatenate(m_out, axis=0)
        sold_ref[...] = jnp.exp(b_last + m_prev - m_next)
        w_ref[...] = jnp.exp(a_all - m_next)

        src = i_all - b_all
        src_ref[...] = src
        inter_log = b_all + m_prev
        m_t = jnp.maximum(inter_log, b_all + _lane_scan(src, jnp.maximum, -jnp.inf))
        per_row = jnp.concatenate([b_all - m_t, jnp.exp(inter_log - m_t), jnp.exp(-m_t)], axis=0)
        per_row = jnp.concatenate([per_row, jnp.zeros((LANES - 3 * G, L), F32)], axis=0)
        cols_ref[...] = per_row.T

    @pl.when(step == 0)
    def _():
        c_ref[...] = jnp.zeros_like(c_ref)

    @pl.when((pl.program_id(0) == 0) & (head == 0) & (step == 0))
    def _():
        m_ref[...] = jnp.zeros_like(m_ref)
        store_gate_stats(ig_ref, fg_ref, head, True)

    cols = cols_ref[...]
    src = src_ref[...]
    w_all = w_ref[...]
    s_old = sold_ref[...]
    store_gate_stats(ig_next_ref, fg_next_ref, next_head, wraps)

    causal = (lax.broadcasted_iota(jnp.int32, (L, L), 1) <= lax.broadcasted_iota(jnp.int32, (L, L), 0))
    dv = v_ref.shape[-1]
    ones_col = (lax.broadcasted_iota(jnp.int32, (L, LANES), 1) == 0).astype(BF16)
    state = c_ref[...]
    for g in range(G):
        rows = slice(g * L, (g + 1) * L)
        q = q_ref[0, rows, :]
        k = k_ref[0, rows, :]
        v_ext = jnp.concatenate([v_ref[0, rows, :], ones_col], axis=1)
        dest = cols[:, g:g + 1]
        inter_w = cols[:, G + g:G + g + 1]
        floor = cols[:, 2 * G + g:2 * G + g + 1]

        dw = jnp.exp(jnp.where(causal, dest + src[g:g + 1], -jnp.inf))
        s = lax.dot_general(q, k, (((1,), (1,)), ((), ())), preferred_element_type=F32) * dw
        q_w = (inter_w * q.astype(F32)).astype(BF16)
        nd = _dot(s.astype(BF16), v_ext) + _dot(q_w, state.astype(BF16))
        num = nd[:, :dv]
        den = nd[:, dv:dv + 1]
        r = 1.0 / jnp.maximum(jnp.abs(den), floor)
        scale = r * lax.rsqrt(r * r * jnp.mean(num * num, axis=-1, keepdims=True) + EPS)
        hn = num * scale * ng_ref[...]
        y_ref[0, rows, :] = (og_ref[0, rows, :].astype(F32) * hn).astype(y_ref.dtype)

        kw_t = (k.astype(F32).T * w_all[g:g + 1]).astype(BF16)
        state = s_old[g:g + 1] * state + _dot(kw_t, v_ext)
    c_ref[...] = state


def _mlstm(q, k, v, o, gates, b_gates, norm_g):
    b, s, _ = q.shape
    dk, dv, L = MLSTM_QK_DIM, MLSTM_V_DIM, MLSTM_CHUNK
    nc = s // L
    g = MLSTM_CHUNKS_PER_STEP
    rows = g * L
    gates = gates.reshape(b, 2 * MLSTM_HEADS, nc, L)
    steps = nc // g
    blocks = b * MLSTM_HEADS * steps

    def gate_spec(first, ahead):
        def index(bi, h, c):
            flat = jnp.minimum((bi * MLSTM_HEADS + h) * steps + c + ahead, blocks - 1)
            return (flat // (MLSTM_HEADS * steps), first + (flat // steps) % MLSTM_HEADS, flat % steps, 0)
        return pl.BlockSpec((1, 1, g, L), index)

    return pl.pallas_call(
        functools.partial(_mlstm_kernel, chunks=g),
        grid=(b, MLSTM_HEADS, nc // g),
        in_specs=[pl.BlockSpec(memory_space=pltpu.SMEM),
                  pl.BlockSpec((1, rows, dk), lambda bi, h, c: (bi, c, h)),
                  pl.BlockSpec((1, rows, dk), lambda bi, h, c: (bi, c, h)),
                  pl.BlockSpec((1, rows, dv), lambda bi, h, c: (bi, c, h)),
                  pl.BlockSpec((1, rows, dv), lambda bi, h, c: (bi, c, h)),
                  gate_spec(0, 0), gate_spec(MLSTM_HEADS, 0),
                  gate_spec(0, 1), gate_spec(MLSTM_HEADS, 1),
                  pl.BlockSpec((1, dv), lambda bi, h, c: (0, h))],
        out_specs=pl.BlockSpec((1, rows, dv), lambda bi, h, c: (bi, c, h)),
        out_shape=jax.ShapeDtypeStruct((b, s, MLSTM_HEADS * dv), BF16),
        scratch_shapes=[pltpu.VMEM((dk, dv + LANES), F32), pltpu.VMEM((1, 1), F32),
                        pltpu.VMEM((L, LANES), F32), pltpu.VMEM((g, L), F32), pltpu.VMEM((g, L), F32),
                        pltpu.VMEM((g, 1), F32)],
        compiler_params=_params("arbitrary", "arbitrary", "arbitrary"),
        name="mlstm",
    )(b_gates.reshape(-1), q, k, v, o, gates, gates, gates, gates, norm_g.reshape(1, -1))


def kernel(x, c, w_mod, b_mod, norm_g, w_ffn_gate, w_ffn_up, w_ffn_down, w_in_even, w_pool, pool_scale, diff_lambda, diff_subln_g, w_out_even, w_in_odd, b_gates_odd, mlstm_norm_g, w_out_odd, final_g):
    depth = w_mod.shape[0]
    mod = _modulation(c, w_mod, b_mod)
    pool_width = pool_scale.shape[-1]
    qk_width = DIFF_HEADS * 2 * DIFF_HEAD_DIM
    ffn_stacked = (w_ffn_gate, w_ffn_up, w_ffn_down)
    ffn_w = tuple(w[0, 0].astype(BF16) for w in ffn_stacked)
    w_in_odd_bf16 = None
    for l in range(depth):
        if l % 2 == 0:
            proj = ("even", norm_g[l, 1], w_in_even[l // 2], pool_width, qk_width)
        else:
            proj = ("odd", norm_g[l, 1], w_in_odd_bf16,
                    MLSTM_HEADS * MLSTM_QK_DIM, MLSTM_HEADS * MLSTM_V_DIM)
        x, cast, mixer_in = _ffn(x, mod, l, 0, norm_g[l, 0], ffn_w, final_g, False,
                                 casts=[(w, (l, 1)) for w in ffn_stacked], proj=proj)
        ffn_w = cast[:3]
        if l % 2 == 0:
            e = l // 2
            lambda_init = 0.8 - 0.6 * math.exp(-0.3 * l)
            u, k, qt, vt = mixer_in
            y_diff = _diff_attention(qt, k, vt, diff_lambda[e], diff_subln_g[e], lambda_init)
            mix = ("even", u, y_diff, w_pool[e], pool_scale[e], w_out_even[e])
        else:
            o = l // 2
            q, k, v, og, gates = mixer_in
            y = _mlstm(q, k, v, og, gates, b_gates_odd[o], mlstm_norm_g[o])
            mix = ("odd", y, w_out_odd[o])
        last = l == depth - 1
        casts = [] if last else [(w, (l + 1, 0)) for w in ffn_stacked]
        if not last and (l + 1) % 2 == 1:
            casts.append((w_in_odd, ((l + 1) // 2,)))
        x, cast, _ = _ffn(x, mod, l, 2, norm_g[l, 2], ffn_w, final_g, last, mix, casts=casts)
        ffn_w = cast[:3]
        w_in_odd_bf16 = cast[3] if len(cast) > 3 else None
    return x
```

```python
import functools
import math

import jax
import jax.numpy as jnp
from jax import lax
from jax.experimental import pallas as pl
from jax.experimental.pallas import tpu as pltpu

F32 = jnp.float32
BF16 = jnp.bfloat16

EPS = 1e-6
N_SUB = 3
POOL_WINDOWS = (2, 4, 8, 16)
POOL_HALO = 16
DIFF_HEADS = 4
DIFF_HEAD_DIM = 64
MLSTM_HEADS = 4
MLSTM_QK_DIM = 128
MLSTM_V_DIM = 256
MLSTM_CHUNK = 128
LANES = 128
BF16_SUBLANES = 16

VMEM_LIMIT_BYTES = 56 * 1024 * 1024

MOD_ROWS = 256
FFN_ROWS = 512
FFN_CAST_CHUNKS = 16
FFN_SUB_ROWS = 256
ATTN_ROWS = 512
ATTN_COL_GROUP = 256
ATTN_TILES_PER_STEP = 2
ATTN_UNROLL = 4
LOG2_E = math.log2(math.e)
MLSTM_CHUNKS_PER_STEP = 64


def _params(*semantics):
    return pltpu.CompilerParams(dimension_semantics=semantics,
                                vmem_limit_bytes=VMEM_LIMIT_BYTES)


def _resident(block_shape, index_map):
    return pl.BlockSpec(block_shape, index_map, pipeline_mode=pl.Buffered(1))


def _dot(a, b):
    return jnp.dot(a, b, preferred_element_type=F32)


def _modnorm(x, g, mod, sub):
    shift = mod[3 * sub:3 * sub + 1]
    scale = mod[3 * sub + 1:3 * sub + 2]
    y = x * lax.rsqrt(jnp.mean(x * x, axis=-1, keepdims=True) + EPS)
    return (y * g) * (1.0 + scale) + shift


def _gate(mod, sub):
    return mod[3 * sub + 2:3 * sub + 3]


def _mod_kernel(c_ref, w_ref, b_ref, o_ref):
    @pl.when(pl.program_id(1) == 0)
    def _():
        o_ref[0] = jnp.broadcast_to(b_ref[0], o_ref.shape[1:])

    c = c_ref[...]
    c_act = c * jax.nn.sigmoid(c)
    w = w_ref[0]
    c_hi = c_act.astype(BF16)
    c_lo = (c_act - c_hi.astype(F32)).astype(BF16)
    w_hi = w.astype(BF16)
    w_lo = (w - w_hi.astype(F32)).astype(BF16)
    o_ref[0] += _dot(c_hi, w_hi) + (_dot(c_hi, w_lo) + _dot(c_lo, w_hi))


def _modulation(c, w_mod, b_mod):
    depth, d, n = w_mod.shape
    b = c.shape[0]
    tk = MOD_ROWS
    out = pl.pallas_call(
        _mod_kernel,
        grid=(depth, d // tk),
        in_specs=[pl.BlockSpec((b, tk), lambda l, j: (0, j)),
                  pl.BlockSpec((1, tk, n), lambda l, j: (l, j, 0)),
                  pl.BlockSpec((1, 1, n), lambda l, j: (l, 0, 0))],
        out_specs=pl.BlockSpec((1, b, n), lambda l, j: (l, 0, 0)),
        out_shape=jax.ShapeDtypeStruct((depth, b, n), F32),
        compiler_params=_params("arbitrary", "arbitrary"),
        name="modulation",
    )(c, w_mod, b_mod.reshape(depth, 1, n))
    return out.reshape(depth, b, 3 * N_SUB, d)


def _swiglu_residual(x, mod, sub, g, wg_ref, wu_ref, wd_ref, fg, final_norm):
    outs = []
    for r in range(0, x.shape[0], FFN_SUB_ROWS):
        xs = x[r:r + FFN_SUB_ROWS]
        h = _modnorm(xs, g, mod, sub).astype(BF16)
        gate_act = _dot(h, wg_ref[...])
        up = _dot(h, wu_ref[...])
        a = (gate_act * jax.nn.sigmoid(gate_act) * up).astype(BF16)
        out = xs + (0.5 * _gate(mod, sub)) * _dot(a, wd_ref[...])
        if final_norm:
            out = out * lax.rsqrt(jnp.mean(out * out, axis=-1, keepdims=True) + EPS) * fg
        outs.append(out)
    return jnp.concatenate(outs, axis=0)


def _pool_mixer(u, halo, first_row, wp_ref, scale):
    tm = u.shape[0]
    ext = jnp.concatenate([halo, u], axis=0)
    pos = first_row + lax.broadcasted_iota(jnp.int32, (tm, 1), 0)
    gw = wp_ref.shape[-1]
    sums = ext
    shift = 1
    parts = []
    for g, w in enumerate(POOL_WINDOWS):
        while shift < w:
            sums = sums + pltpu.roll(sums, shift, 0)
            shift *= 2
        total = sums[POOL_HALO:, g * gw:(g + 1) * gw]
        count = jnp.minimum(pos + 1, w).astype(F32)
        pooled = total / count - u[:, g * gw:(g + 1) * gw]
        parts.append(_dot(pooled.astype(BF16), wp_ref[g]))
    return (jnp.concatenate(parts, axis=1) * scale).astype(BF16)


def _mixed_input(mix, x_ref, mod, mix_refs):
    if mix is None:
        return x_ref[0]
    if mix == "odd":
        y_ref, wo_ref = mix_refs
        return x_ref[0] + _gate(mod, 1) * _dot(y_ref[0], wo_ref[...])
    u_ref, halo_ref, yd_ref, wp_ref, ps_ref, wo_ref = mix_refs
    i = pl.program_id(1)
    tm = u_ref.shape[1]
    halo = halo_ref[0] * (i > 0).astype(F32)
    y_pool = _pool_mixer(u_ref[0], halo, i * tm, wp_ref, ps_ref[...])
    pw = y_pool.shape[1]
    y = _dot(y_pool, wo_ref[:pw, :]) + _dot(yd_ref[0], wo_ref[pw:, :])
    return x_ref[0] + _gate(mod, 1) * y


_N_MIX_REFS = {None: 0, "odd": 2, "even": 6}


def _even_projection(h, w_ref, wt_ref, u_ref, k_ref, qt_ref, vt_ref):
    p = _dot(h, w_ref[...])
    pw = u_ref.shape[-1]
    u_ref[0] = p[:, :pw]
    k_ref[0] = p[:, pw:].astype(BF16)
    pt = lax.dot_general(wt_ref[...], h, (((1,), (1,)), ((), ())), preferred_element_type=F32)
    qw = qt_ref.shape[1]
    qt_ref[0] = (pt[:qw] * (DIFF_HEAD_DIM ** -0.5 * LOG2_E)).astype(BF16)
    vt_ref[0, 0] = pt[qw:].astype(BF16)


def _odd_projection(h, w_ref, wgate_ref, q_ref, k_ref, v_ref, o_ref, gates_ref):
    p = _dot(h, w_ref[...])
    nq = q_ref.shape[-1]
    nv = v_ref.shape[-1]
    q_ref[0] = p[:, :nq].astype(BF16)
    k_ref[0] = (p[:, nq:2 * nq] * (MLSTM_QK_DIM ** -0.5)).astype(BF16)
    v_ref[0] = p[:, 2 * nq:2 * nq + nv].astype(BF16)
    o_ref[0] = jax.nn.sigmoid(p[:, 2 * nq + nv:]).astype(BF16)
    gates_ref[0] = _dot(h, wgate_ref[...]).T[:gates_ref.shape[1]]


def _ffn_kernel(*refs, sub, final_norm, mix, n_cast, proj):
    x_ref, mod_ref, g_ref, wg_ref, wu_ref, wd_ref, fg_ref = refs[:7]
    pos = 7
    mix_refs = refs[pos:pos + _N_MIX_REFS[mix]]
    pos += len(mix_refs)
    cast_in = refs[pos:pos + n_cast]
    pos += n_cast
    proj_in = refs[pos:pos + (3 if proj else 0)]
    pos += len(proj_in)
    o_ref = refs[pos]
    cast_out = refs[pos + 1:pos + 1 + n_cast]
    proj_out = refs[pos + 1 + n_cast:]
    mod = mod_ref[0, 0]
    x = _mixed_input(mix, x_ref, mod, mix_refs)
    out = _swiglu_residual(x, mod, sub, g_ref[...], wg_ref, wu_ref, wd_ref, fg_ref[...], final_norm)
    o_ref[0] = out
    for src_ref, dst_ref in zip(cast_in, cast_out):
        dst_ref[...] = src_ref[(0,) * (len(src_ref.shape) - 2)].astype(BF16)
    if proj:
        g2_ref, w_a_ref, w_b_ref = proj_in
        h = _modnorm(out, g2_ref[...], mod, 1).astype(BF16)
        (_even_projection if proj == "even" else _odd_projection)(h, w_a_ref, w_b_ref, *proj_out)


def _ffn(x, mod, layer, sub, g, weights, final_g, final_norm, mix=None, casts=(), proj=None):
    b, s, d = x.shape
    f = weights[0].shape[-1]
    tm = FFN_ROWS
    n_i = s // tm
    row = lambda bi, i: (bi, i, 0)
    const2 = lambda bi, i: (0, 0)
    in_specs = [pl.BlockSpec((1, tm, d), row),
                pl.BlockSpec((1, 1, 3 * N_SUB, d), lambda bi, i: (layer, bi, 0, 0)),
                pl.BlockSpec((1, d), const2),
                _resident((d, f), const2),
                _resident((d, f), const2),
                _resident((f, d), const2),
                pl.BlockSpec((1, d), const2)]
    args = [x, mod, g.reshape(1, d), *weights, final_g.reshape(1, d)]
    kind = None if mix is None else mix[0]
    if kind == "odd":
        _, y, w_out = mix
        in_specs += [pl.BlockSpec((1, tm, y.shape[-1]), row), _resident(w_out.shape, const2)]
        args += [y, w_out.astype(BF16)]
    elif kind == "even":
        _, u, y_diff, w_pool, pool_scale, w_out = mix
        pw = u.shape[-1]
        halo_blocks = tm // POOL_HALO
        in_specs += [pl.BlockSpec((1, tm, pw), row),
                     pl.BlockSpec((1, POOL_HALO, pw),
                                  lambda bi, i: (bi, jnp.maximum(i * halo_blocks - 1, 0), 0)),
                     pl.BlockSpec((1, tm, y_diff.shape[-1]), row),
                     _resident(w_pool.shape, lambda bi, i: (0, 0, 0)),
                     pl.BlockSpec((1, pw), const2),
                     _resident(w_out.shape, const2)]
        args += [u, u, y_diff, w_pool.astype(BF16), pool_scale.reshape(1, pw), w_out.astype(BF16)]
    out_specs = [pl.BlockSpec((1, tm, d), row)]
    out_shape = [jax.ShapeDtypeStruct((b, s, d), F32)]
    steps = b * n_i
    assert steps % FFN_CAST_CHUNKS == 0
    chunk_of = lambda bi, i: (bi * n_i + i) // (steps // FFN_CAST_CHUNKS)
    for w, lead in casts:
        rows, cols = w.shape[-2:]
        assert rows % (FFN_CAST_CHUNKS * BF16_SUBLANES) == 0
        chunk = rows // FFN_CAST_CHUNKS
        in_specs.append(pl.BlockSpec((1,) * len(lead) + (chunk, cols),
                                     lambda bi, i, lead=lead: (*lead, chunk_of(bi, i), 0)))
        args.append(w)
        out_specs.append(pl.BlockSpec((chunk, cols), lambda bi, i: (chunk_of(bi, i), 0)))
        out_shape.append(jax.ShapeDtypeStruct((rows, cols), BF16))
    proj_kind = None if proj is None else proj[0]
    if proj_kind == "even":
        _, g2, w_in, pool_width, qk_width = proj
        assert tm == ATTN_ROWS
        v_width = w_in.shape[1] - pool_width - 2 * qk_width
        k0 = pool_width + qk_width
        w_uk = jnp.concatenate([w_in[:, :pool_width], w_in[:, k0:k0 + qk_width]], axis=1).astype(BF16)
        w_qv_t = jnp.concatenate([w_in[:, pool_width:k0], w_in[:, k0 + qk_width:]], axis=1).T.astype(BF16)
        in_specs += [pl.BlockSpec((1, d), const2), _resident(w_uk.shape, const2),
                     _resident(w_qv_t.shape, const2)]
        args += [g2.reshape(1, d), w_uk, w_qv_t]
        out_specs += [pl.BlockSpec((1, tm, pool_width), row),
                      pl.BlockSpec((1, tm, qk_width), row),
                      pl.BlockSpec((1, qk_width, tm), lambda bi, i: (bi, 0, i)),
                      pl.BlockSpec((1, 1, v_width, tm), lambda bi, i: (bi, i, 0, 0))]
        out_shape += [jax.ShapeDtypeStruct((b, s, pool_width), F32),
                      jax.ShapeDtypeStruct((b, s, qk_width), BF16),
                      jax.ShapeDtypeStruct((b, qk_width, s), BF16),
                      jax.ShapeDtypeStruct((b, s // tm, v_width, tm), BF16)]
    elif proj_kind == "odd":
        _, g2, w_in, nq, nv = proj
        n_main = 2 * nq + 2 * nv
        n_gates = w_in.shape[1] - n_main
        w_gate = jnp.pad(w_in[:, n_main:], ((0, 0), (0, LANES - n_gates)))
        in_specs += [pl.BlockSpec((1, d), const2), _resident((d, n_main), const2),
                     _resident((d, LANES), const2)]
        args += [g2.reshape(1, d), w_in, w_gate]
        out_specs += [pl.BlockSpec((1, tm, nq), row),
                      pl.BlockSpec((1, tm, nq), row),
                      pl.BlockSpec((1, tm, nv), row),
                      pl.BlockSpec((1, tm, nv), row),
                      pl.BlockSpec((1, n_gates, tm), lambda bi, i: (bi, 0, i))]
        out_shape += [jax.ShapeDtypeStruct((b, s, nq), BF16),
                      jax.ShapeDtypeStruct((b, s, nq), BF16),
                      jax.ShapeDtypeStruct((b, s, nv), BF16),
                      jax.ShapeDtypeStruct((b, s, nv), BF16),
                      jax.ShapeDtypeStruct((b, n_gates, s), F32)]
    name = "ffn" if kind is None else kind + "_mix_ffn"
    outs = pl.pallas_call(
        functools.partial(_ffn_kernel, sub=sub, final_norm=final_norm, mix=kind,
                          n_cast=len(casts), proj=proj_kind),
        grid=(b, n_i),
        in_specs=in_specs,
        out_specs=out_specs,
        out_shape=out_shape,
        compiler_params=_params("arbitrary", "arbitrary"),
        name=name if proj is None else name + "_" + proj_kind + "_proj",
    )(*args)
    n_cast = len(casts)
    return outs[0], tuple(outs[1:1 + n_cast]), tuple(outs[1 + n_cast:])


def _diff_attn_kernel(qt_ref, k_ref, vt_ref, lam_ref, sg_ref, o_ref, m_ref, l_ref, acc_ref,
                      s0_ref, s1_ref, cm0_ref, cm1_ref, *, lambda_init, t):
    for part in range(ATTN_TILES_PER_STEP):
        rows = slice(part * t, (part + 1) * t)
        _diff_attn_query_tile(pl.program_id(2) * ATTN_TILES_PER_STEP + part, qt_ref[0, :, rows], k_ref,
                              vt_ref, lam_ref, sg_ref, o_ref.at[0, rows, :], m_ref, l_ref, acc_ref,
                              (s0_ref, s1_ref), (cm0_ref, cm1_ref), lambda_init, t)


def _diff_attn_query_tile(i, qt, k_ref, vt_ref, lam_ref, sg_ref, o_ref, m_ref, l_ref, acc_ref, s_refs,
                          cm_refs, lambda_init, t):
    qt = qt.astype(F32)
    feat = lax.broadcasted_iota(jnp.int32, qt.shape, 0)
    qs = jnp.concatenate([jnp.where(feat < DIFF_HEAD_DIM, qt, 0.0),
                          jnp.where(feat >= DIFF_HEAD_DIM, qt, 0.0)], axis=1).astype(BF16)
    m_ref[...] = jnp.full_like(m_ref, -jnp.inf)
    l_ref[...] = jnp.zeros_like(l_ref)
    acc_ref[...] = jnp.zeros_like(acc_ref)
    groups = [slice(c, c + ATTN_COL_GROUP) for c in range(0, 2 * t, ATTN_COL_GROUP)]

    def produce(j, slot, cols):
        kb = k_ref[0, pl.ds(pl.multiple_of(j * t, t), t), :]
        s = _dot(kb, qs[:, cols])
        s_refs[slot][:, cols] = s
        cm_refs[slot][:, cols] = jnp.max(s, axis=0, keepdims=True)

    def consume(j, slot, cols, diagonal):
        if diagonal:
            first_q = cols.start % t
            n_keys = first_q + ATTN_COL_GROUP
            s = s_refs[slot][:n_keys, cols]
            key = lax.broadcasted_iota(jnp.int32, s.shape, 0)
            qry = lax.broadcasted_iota(jnp.int32, s.shape, 1) + first_q
            s = jnp.where(key <= qry, s, -jnp.inf)
            cm = jnp.max(s, axis=0, keepdims=True)
            vt = vt_ref[0, j][:, :n_keys]
        else:
            s = s_refs[slot][:, cols]
            cm = cm_refs[slot][:, cols]
            vt = vt_ref[0, j]
        m = m_ref[:, cols]
        m_new = jnp.maximum(m, cm)
        alpha = jnp.exp2(m - m_new)
        p = jnp.exp2(s - m_new)
        l_ref[:, cols] = alpha * l_ref[:, cols] + jnp.sum(p, axis=0, keepdims=True)
        acc_ref[:, cols] = alpha * acc_ref[:, cols] + _dot(vt, p.astype(BF16))
        m_ref[:, cols] = m_new

    def advance(j, slot):
        for cols in groups:
            produce(j + 1, 1 - slot, cols)
            consume(j, slot, cols, False)

    rest = i % ATTN_UNROLL
    for r in range(ATTN_UNROLL):
        @pl.when(rest == r)
        def _(r=r):
            for cols in groups:
                produce(0, r % 2, cols)
            for j in range(r):
                advance(j, (j - r) % 2)

    def body(jj, carry):
        for d in range(ATTN_UNROLL):
            advance(rest + ATTN_UNROLL * jj + d, d % 2)
        return carry

    lax.fori_loop(0, i // ATTN_UNROLL, body, 0)
    for cols in groups:
        consume(i, 0, cols, True)

    o = acc_ref[...] / l_ref[...]
    lp = lam_ref[...]
    lam = (jnp.exp(jnp.sum(lp[0:1] * lp[1:2], axis=1, keepdims=True))
           - jnp.exp(jnp.sum(lp[2:3] * lp[3:4], axis=1, keepdims=True)) + lambda_init)
    o = o[:, :t] - lam * o[:, t:]
    o = o * lax.rsqrt(jnp.mean(o * o, axis=0, keepdims=True) + EPS) * sg_ref[...]
    o_ref[...] = (o * (1.0 - lambda_init)).T.astype(o_ref.dtype)


def _diff_attention(qt, k, vt, lam_params, subln_g, lambda_init):
    b, s, _ = k.shape
    hd = 2 * DIFF_HEAD_DIM
    t = ATTN_ROWS
    tq = ATTN_TILES_PER_STEP * t
    assert s % tq == 0 and vt.shape[-1] == t and (2 * t) % ATTN_COL_GROUP == 0
    return pl.pallas_call(
        functools.partial(_diff_attn_kernel, lambda_init=lambda_init, t=t),
        grid=(b, DIFF_HEADS, s // tq),
        in_specs=[pl.BlockSpec((1, hd, tq), lambda bi, h, i: (bi, h, i)),
                  pl.BlockSpec((1, s, hd), lambda bi, h, i: (bi, 0, h)),
                  pl.BlockSpec((1, s // t, hd, t), lambda bi, h, i: (bi, 0, h, 0)),
                  pl.BlockSpec(lam_params.shape, lambda bi, h, i: (0, 0)),
                  pl.BlockSpec((hd, 1), lambda bi, h, i: (0, 0))],
        out_specs=pl.BlockSpec((1, tq, hd), lambda bi, h, i: (bi, i, h)),
        out_shape=jax.ShapeDtypeStruct((b, s, DIFF_HEADS * hd), BF16),
        scratch_shapes=[pltpu.VMEM((1, 2 * t), F32), pltpu.VMEM((1, 2 * t), F32),
                        pltpu.VMEM((hd, 2 * t), F32),
                        pltpu.VMEM((t, 2 * t), F32), pltpu.VMEM((t, 2 * t), F32),
                        pltpu.VMEM((1, 2 * t), F32), pltpu.VMEM((1, 2 * t), F32)],
        compiler_params=_params("arbitrary", "arbitrary", "arbitrary"),
        name="diff_attention",
    )(qt, k, vt, lam_params, subln_g.reshape(hd, 1))


def _log_sigmoid(x):
    return jnp.minimum(x, 0.0) - jnp.log1p(jnp.exp(-jnp.abs(x)))


def _lane_scan(x, combine, fill):
    lane = lax.broadcasted_iota(jnp.int32, x.shape, 1)
    shift = 1
    while shift < x.shape[-1]:
        x = combine(x, jnp.where(lane >= shift, pltpu.roll(x, shift, 1), fill))
        shift *= 2
    return x


def _mlstm_kernel(bg_ref, q_ref, k_ref, v_ref, og_ref, ig_ref, fg_ref, ig_next_ref, fg_next_ref,
                  ng_ref, y_ref, c_ref, m_ref, cols_ref, src_ref, w_ref, sold_ref, *, chunks):
    head = pl.program_id(1)
    step = pl.program_id(2)
    L = MLSTM_CHUNK
    G = chunks
    wraps = step == pl.num_programs(2) - 1
    next_head = jnp.where(wraps, (head + 1) % MLSTM_HEADS, head)

    def store_gate_stats(i_ref, f_ref, hd, fresh):
        i_all = i_ref[0, 0] + bg_ref[hd]
        logf = _log_sigmoid(f_ref[0, 0] + bg_ref[MLSTM_HEADS + hd])
        b_all = _lane_scan(logf, jnp.add, 0.0)
        b_last = b_all[:, L - 1:L]
        a_all = b_last - b_all + i_all
        m_loc = jnp.max(a_all, axis=1, keepdims=True)

        m_in = []
        m = jnp.where(fresh, 0.0, m_ref[...])
        for g in range(G):
            m_in.append(m)
            m = jnp.maximum(b_last[g:g + 1] + m, m_loc[g:g + 1])
        m_out = m_in[1:] + [m]
        m_ref[...] = m
        m_prev = jnp.concatenate(m_in, axis=0)
        m_next = jnp.concatenate(m_out, axis=0)
        sold_ref[...] = jnp.exp(b_last + m_prev - m_next)
        w_ref[...] = jnp.exp(a_all - m_next)

        src = i_all - b_all
        src_ref[...] = src
        inter_log = b_all + m_prev
        m_t = jnp.maximum(inter_log, b_all + _lane_scan(src, jnp.maximum, -jnp.inf))
        per_row = jnp.concatenate([b_all - m_t, jnp.exp(inter_log - m_t), jnp.exp(-m_t)], axis=0)
        per_row = jnp.concatenate([per_row, jnp.zeros((cols_ref.shape[1] - 3 * G, L), F32)], axis=0)
        cols_ref[...] = per_row.T

    @pl.when(step == 0)
    def _():
        c_ref[...] = jnp.zeros_like(c_ref)

    @pl.when((pl.program_id(0) == 0) & (head == 0) & (step == 0))
    def _():
        m_ref[...] = jnp.zeros_like(m_ref)
        store_gate_stats(ig_ref, fg_ref, head, True)

    cols = cols_ref[...]
    src = src_ref[...]
    w_all = w_ref[...]
    s_old = sold_ref[...]
    store_gate_stats(ig_next_ref, fg_next_ref, next_head, wraps)

    causal = (lax.broadcasted_iota(jnp.int32, (L, L), 1) <= lax.broadcasted_iota(jnp.int32, (L, L), 0))
    dv = v_ref.shape[-1]
    ones_col = (lax.broadcasted_iota(jnp.int32, (L, LANES), 1) == 0).astype(BF16)
    state = c_ref[...]
    for g in range(G):
        rows = slice(g * L, (g + 1) * L)
        q = q_ref[0, rows, :]
        k = k_ref[0, rows, :]
        v_ext = jnp.concatenate([v_ref[0, rows, :], ones_col], axis=1)
        dest = cols[:, g:g + 1]
        inter_w = cols[:, G + g:G + g + 1]
        floor = cols[:, 2 * G + g:2 * G + g + 1]

        dw = jnp.exp(jnp.where(causal, dest + src[g:g + 1], -jnp.inf))
        s = lax.dot_general(q, k, (((1,), (1,)), ((), ())), preferred_element_type=F32) * dw
        q_w = (inter_w * q.astype(F32)).astype(BF16)
        nd = _dot(s.astype(BF16), v_ext) + _dot(q_w, state.astype(BF16))
        num = nd[:, :dv]
        den = nd[:, dv:dv + 1]
        r = 1.0 / jnp.maximum(jnp.abs(den), floor)
        scale = r * lax.rsqrt(r * r * jnp.mean(num * num, axis=-1, keepdims=True) + EPS)
        hn = num * scale * ng_ref[...]
        y_ref[0, rows, :] = (og_ref[0, rows, :].astype(F32) * hn).astype(y_ref.dtype)

        kw_t = (k.astype(F32).T * w_all[g:g + 1]).astype(BF16)
        state = s_old[g:g + 1] * state + _dot(kw_t, v_ext)
    c_ref[...] = state


def _mlstm(q, k, v, o, gates, b_gates, norm_g):
    b, s, _ = q.shape
    dk, dv, L = MLSTM_QK_DIM, MLSTM_V_DIM, MLSTM_CHUNK
    nc = s // L
    g = MLSTM_CHUNKS_PER_STEP
    rows = g * L
    gates = gates.reshape(b, 2 * MLSTM_HEADS, nc, L)
    steps = nc // g
    blocks = b * MLSTM_HEADS * steps

    def gate_spec(first, ahead):
        def index(bi, h, c):
            flat = jnp.minimum((bi * MLSTM_HEADS + h) * steps + c + ahead, blocks - 1)
            return (flat // (MLSTM_HEADS * steps), first + (flat // steps) % MLSTM_HEADS, flat % steps, 0)
        return pl.BlockSpec((1, 1, g, L), index)

    return pl.pallas_call(
        functools.partial(_mlstm_kernel, chunks=g),
        grid=(b, MLSTM_HEADS, nc // g),
        in_specs=[pl.BlockSpec(memory_space=pltpu.SMEM),
                  pl.BlockSpec((1, rows, dk), lambda bi, h, c: (bi, c, h)),
                  pl.BlockSpec((1, rows, dk), lambda bi, h, c: (bi, c, h)),
                  pl.BlockSpec((1, rows, dv), lambda bi, h, c: (bi, c, h)),
                  pl.BlockSpec((1, rows, dv), lambda bi, h, c: (bi, c, h)),
                  gate_spec(0, 0), gate_spec(MLSTM_HEADS, 0),
                  gate_spec(0, 1), gate_spec(MLSTM_HEADS, 1),
                  pl.BlockSpec((1, dv), lambda bi, h, c: (0, h))],
        out_specs=pl.BlockSpec((1, rows, dv), lambda bi, h, c: (bi, c, h)),
        out_shape=jax.ShapeDtypeStruct((b, s, MLSTM_HEADS * dv), BF16),
        scratch_shapes=[pltpu.VMEM((dk, dv + LANES), F32), pltpu.VMEM((1, 1), F32),
                        pltpu.VMEM((L, -(-3 * g // LANES) * LANES), F32),
                        pltpu.VMEM((g, L), F32), pltpu.VMEM((g, L), F32),
                        pltpu.VMEM((g, 1), F32)],
        compiler_params=_params("arbitrary", "arbitrary", "arbitrary"),
        name="mlstm",
    )(b_gates.reshape(-1), q, k, v, o, gates, gates, gates, gates, norm_g.reshape(1, -1))


def kernel(x, c, w_mod, b_mod, norm_g, w_ffn_gate, w_ffn_up, w_ffn_down, w_in_even, w_pool, pool_scale, diff_lambda, diff_subln_g, w_out_even, w_in_odd, b_gates_odd, mlstm_norm_g, w_out_odd, final_g):
    depth = w_mod.shape[0]
    mod = _modulation(c, w_mod, b_mod)
    pool_width = pool_scale.shape[-1]
    qk_width = DIFF_HEADS * 2 * DIFF_HEAD_DIM
    ffn_stacked = (w_ffn_gate, w_ffn_up, w_ffn_down)
    ffn_w = tuple(w[0, 0].astype(BF16) for w in ffn_stacked)
    w_in_odd_bf16 = None
    for l in range(depth):
        if l % 2 == 0:
            proj = ("even", norm_g[l, 1], w_in_even[l // 2], pool_width, qk_width)
        else:
            proj = ("odd", norm_g[l, 1], w_in_odd_bf16,
                    MLSTM_HEADS * MLSTM_QK_DIM, MLSTM_HEADS * MLSTM_V_DIM)
        x, cast, mixer_in = _ffn(x, mod, l, 0, norm_g[l, 0], ffn_w, final_g, False,
                                 casts=[(w, (l, 1)) for w in ffn_stacked], proj=proj)
        ffn_w = cast[:3]
        if l % 2 == 0:
            e = l // 2
            lambda_init = 0.8 - 0.6 * math.exp(-0.3 * l)
            u, k, qt, vt = mixer_in
            y_diff = _diff_attention(qt, k, vt, diff_lambda[e], diff_subln_g[e], lambda_init)
            mix = ("even", u, y_diff, w_pool[e], pool_scale[e], w_out_even[e])
        else:
            o = l // 2
            q, k, v, og, gates = mixer_in
            y = _mlstm(q, k, v, og, gates, b_gates_odd[o], mlstm_norm_g[o])
            mix = ("odd", y, w_out_odd[o])
        last = l == depth - 1
        casts = [] if last else [(w, (l + 1, 0)) for w in ffn_stacked]
        if not last and (l + 1) % 2 == 1:
            casts.append((w_in_odd, ((l + 1) // 2,)))
        x, cast, _ = _ffn(x, mod, l, 2, norm_g[l, 2], ffn_w, final_g, last, mix, casts=casts)
        ffn_w = cast[:3]
        w_in_odd_bf16 = cast[3] if len(cast) > 3 else None
    return x
```

```python
import functools
import math

import jax
import jax.numpy as jnp
from jax import lax
from jax.experimental import pallas as pl
from jax.experimental.pallas import tpu as pltpu

F32 = jnp.float32
BF16 = jnp.bfloat16

EPS = 1e-6
N_SUB = 3
POOL_WINDOWS = (2, 4, 8, 16)
POOL_HALO = 16
DIFF_HEADS = 4
DIFF_HEAD_DIM = 64
MLSTM_HEADS = 4
MLSTM_QK_DIM = 128
MLSTM_V_DIM = 256
MLSTM_CHUNK = 128
LANES = 128
BF16_SUBLANES = 16

VMEM_LIMIT_BYTES = 56 * 1024 * 1024

MOD_ROWS = 256
FFN_ROWS = 512
FFN_CAST_CHUNKS = 16
FFN_SUB_ROWS = 256
ATTN_ROWS = 512
ATTN_COL_GROUP = 256
ATTN_TILES_PER_STEP = 2
ATTN_UNROLL = 4
LOG2_E = math.log2(math.e)
MLSTM_CHUNKS_PER_STEP = 64


def _params(*semantics):
    return pltpu.CompilerParams(dimension_semantics=semantics,
                                vmem_limit_bytes=VMEM_LIMIT_BYTES)


def _resident(block_shape, index_map):
    return pl.BlockSpec(block_shape, index_map, pipeline_mode=pl.Buffered(1))


def _dot(a, b):
    return jnp.dot(a, b, preferred_element_type=F32)


def _modnorm(x, g, mod, sub):
    shift = mod[3 * sub:3 * sub + 1]
    scale = mod[3 * sub + 1:3 * sub + 2]
    y = x * lax.rsqrt(jnp.mean(x * x, axis=-1, keepdims=True) + EPS)
    return (y * g) * (1.0 + scale) + shift


def _gate(mod, sub):
    return mod[3 * sub + 2:3 * sub + 3]


def _mod_kernel(c_ref, w_ref, b_ref, o_ref):
    @pl.when(pl.program_id(1) == 0)
    def _():
        o_ref[0] = jnp.broadcast_to(b_ref[0], o_ref.shape[1:])

    c = c_ref[...]
    c_act = c * jax.nn.sigmoid(c)
    w = w_ref[0]
    c_hi = c_act.astype(BF16)
    c_lo = (c_act - c_hi.astype(F32)).astype(BF16)
    w_hi = w.astype(BF16)
    w_lo = (w - w_hi.astype(F32)).astype(BF16)
    o_ref[0] += _dot(c_hi, w_hi) + (_dot(c_hi, w_lo) + _dot(c_lo, w_hi))


def _modulation(c, w_mod, b_mod):
    depth, d, n = w_mod.shape
    b = c.shape[0]
    tk = MOD_ROWS
    out = pl.pallas_call(
        _mod_kernel,
        grid=(depth, d // tk),
        in_specs=[pl.BlockSpec((b, tk), lambda l, j: (0, j)),
                  pl.BlockSpec((1, tk, n), lambda l, j: (l, j, 0)),
                  pl.BlockSpec((1, 1, n), lambda l, j: (l, 0, 0))],
        out_specs=pl.BlockSpec((1, b, n), lambda l, j: (l, 0, 0)),
        out_shape=jax.ShapeDtypeStruct((depth, b, n), F32),
        compiler_params=_params("arbitrary", "arbitrary"),
        name="modulation",
    )(c, w_mod, b_mod.reshape(depth, 1, n))
    return out.reshape(depth, b, 3 * N_SUB, d)


def _swiglu_residual(x, mod, sub, g, wg_ref, wu_ref, wd_ref, fg, final_norm):
    outs = []
    for r in range(0, x.shape[0], FFN_SUB_ROWS):
        xs = x[r:r + FFN_SUB_ROWS]
        h = _modnorm(xs, g, mod, sub).astype(BF16)
        gate_act = _dot(h, wg_ref[...])
        up = _dot(h, wu_ref[...])
        a = (gate_act * jax.nn.sigmoid(gate_act) * up).astype(BF16)
        out = xs + (0.5 * _gate(mod, sub)) * _dot(a, wd_ref[...])
        if final_norm:
            out = out * lax.rsqrt(jnp.mean(out * out, axis=-1, keepdims=True) + EPS) * fg
        outs.append(out)
    return jnp.concatenate(outs, axis=0)


def _pool_mixer(u, halo, first_row, wp_ref, scale):
    tm = u.shape[0]
    ext = jnp.concatenate([halo, u], axis=0)
    pos = first_row + lax.broadcasted_iota(jnp.int32, (tm, 1), 0)
    gw = wp_ref.shape[-1]
    sums = ext
    shift = 1
    parts = []
    for g, w in enumerate(POOL_WINDOWS):
        while shift < w:
            sums = sums + pltpu.roll(sums, shift, 0)
            shift *= 2
        total = sums[POOL_HALO:, g * gw:(g + 1) * gw]
        count = jnp.minimum(pos + 1, w).astype(F32)
        pooled = total / count - u[:, g * gw:(g + 1) * gw]
        parts.append(_dot(pooled.astype(BF16), wp_ref[g]))
    return (jnp.concatenate(parts, axis=1) * scale).astype(BF16)


def _mixed_input(mix, x_ref, mod, mix_refs):
    if mix is None:
        return x_ref[0]
    if mix == "odd":
        y_ref, wo_ref = mix_refs
        return x_ref[0] + _gate(mod, 1) * _dot(y_ref[0], wo_ref[...])
    u_ref, halo_ref, yd_ref, wp_ref, ps_ref, wo_ref = mix_refs
    i = pl.program_id(1)
    tm = u_ref.shape[1]
    halo = halo_ref[0] * (i > 0).astype(F32)
    y_pool = _pool_mixer(u_ref[0], halo, i * tm, wp_ref, ps_ref[...])
    pw = y_pool.shape[1]
    y = _dot(y_pool, wo_ref[:pw, :]) + _dot(yd_ref[0], wo_ref[pw:, :])
    return x_ref[0] + _gate(mod, 1) * y


_N_MIX_REFS = {None: 0, "odd": 2, "even": 6}


def _even_projection(h, w_ref, wt_ref, u_ref, k_ref, qt_ref, vt_ref):
    p = _dot(h, w_ref[...])
    pw = u_ref.shape[-1]
    u_ref[0] = p[:, :pw]
    k_ref[0] = p[:, pw:].astype(BF16)
    pt = lax.dot_general(wt_ref[...], h, (((1,), (1,)), ((), ())), preferred_element_type=F32)
    qw = qt_ref.shape[1]
    qt_ref[0] = (pt[:qw] * (DIFF_HEAD_DIM ** -0.5 * LOG2_E)).astype(BF16)
    vt_ref[0, 0] = pt[qw:].astype(BF16)


def _odd_projection(h, w_ref, wgate_ref, q_ref, k_ref, v_ref, o_ref, gates_ref):
    p = _dot(h, w_ref[...])
    nq = q_ref.shape[-1]
    nv = v_ref.shape[-1]
    q_ref[0] = p[:, :nq].astype(BF16)
    k_ref[0] = (p[:, nq:2 * nq] * (MLSTM_QK_DIM ** -0.5)).astype(BF16)
    v_ref[0] = p[:, 2 * nq:2 * nq + nv].astype(BF16)
    o_ref[0] = jax.nn.sigmoid(p[:, 2 * nq + nv:]).astype(BF16)
    gates_ref[0] = _dot(h, wgate_ref[...]).T[:gates_ref.shape[1]]


def _ffn_kernel(*refs, sub, final_norm, mix, n_cast, proj):
    x_ref, mod_ref, g_ref, wg_ref, wu_ref, wd_ref, fg_ref = refs[:7]
    pos = 7
    mix_refs = refs[pos:pos + _N_MIX_REFS[mix]]
    pos += len(mix_refs)
    cast_in = refs[pos:pos + n_cast]
    pos += n_cast
    proj_in = refs[pos:pos + (3 if proj else 0)]
    pos += len(proj_in)
    o_ref = refs[pos]
    cast_out = refs[pos + 1:pos + 1 + n_cast]
    proj_out = refs[pos + 1 + n_cast:]
    mod = mod_ref[0, 0]
    x = _mixed_input(mix, x_ref, mod, mix_refs)
    out = _swiglu_residual(x, mod, sub, g_ref[...], wg_ref, wu_ref, wd_ref, fg_ref[...], final_norm)
    o_ref[0] = out
    for src_ref, dst_ref in zip(cast_in, cast_out):
        dst_ref[...] = src_ref[(0,) * (len(src_ref.shape) - 2)].astype(BF16)
    if proj:
        g2_ref, w_a_ref, w_b_ref = proj_in
        h = _modnorm(out, g2_ref[...], mod, 1).astype(BF16)
        (_even_projection if proj == "even" else _odd_projection)(h, w_a_ref, w_b_ref, *proj_out)


def _ffn(x, mod, layer, sub, g, weights, final_g, final_norm, mix=None, casts=(), proj=None):
    b, s, d = x.shape
    f = weights[0].shape[-1]
    tm = FFN_ROWS
    n_i = s // tm
    row = lambda bi, i: (bi, i, 0)
    const2 = lambda bi, i: (0, 0)
    in_specs = [pl.BlockSpec((1, tm, d), row),
                pl.BlockSpec((1, 1, 3 * N_SUB, d), lambda bi, i: (layer, bi, 0, 0)),
                pl.BlockSpec((1, d), const2),
                _resident((d, f), const2),
                _resident((d, f), const2),
                _resident((f, d), const2),
                pl.BlockSpec((1, d), const2)]
    args = [x, mod, g.reshape(1, d), *weights, final_g.reshape(1, d)]
    kind = None if mix is None else mix[0]
    if kind == "odd":
        _, y, w_out = mix
        in_specs += [pl.BlockSpec((1, tm, y.shape[-1]), row), _resident(w_out.shape, const2)]
        args += [y, w_out.astype(BF16)]
    elif kind == "even":
        _, u, y_diff, w_pool, pool_scale, w_out = mix
        pw = u.shape[-1]
        halo_blocks = tm // POOL_HALO
        in_specs += [pl.BlockSpec((1, tm, pw), row),
                     pl.BlockSpec((1, POOL_HALO, pw),
                                  lambda bi, i: (bi, jnp.maximum(i * halo_blocks - 1, 0), 0)),
                     pl.BlockSpec((1, tm, y_diff.shape[-1]), row),
                     _resident(w_pool.shape, lambda bi, i: (0, 0, 0)),
                     pl.BlockSpec((1, pw), const2),
                     _resident(w_out.shape, const2)]
        args += [u, u, y_diff, w_pool.astype(BF16), pool_scale.reshape(1, pw), w_out.astype(BF16)]
    out_specs = [pl.BlockSpec((1, tm, d), row)]
    out_shape = [jax.ShapeDtypeStruct((b, s, d), F32)]
    steps = b * n_i
    assert steps % FFN_CAST_CHUNKS == 0
    chunk_of = lambda bi, i: (bi * n_i + i) // (steps // FFN_CAST_CHUNKS)
    for w, lead in casts:
        rows, cols = w.shape[-2:]
        assert rows % (FFN_CAST_CHUNKS * BF16_SUBLANES) == 0
        chunk = rows // FFN_CAST_CHUNKS
        in_specs.append(pl.BlockSpec((1,) * len(lead) + (chunk, cols),
                                     lambda bi, i, lead=lead: (*lead, chunk_of(bi, i), 0)))
        args.append(w)
        out_specs.append(pl.BlockSpec((chunk, cols), lambda bi, i: (chunk_of(bi, i), 0)))
        out_shape.append(jax.ShapeDtypeStruct((rows, cols), BF16))
    proj_kind = None if proj is None else proj[0]
    if proj_kind == "even":
        _, g2, w_in, pool_width, qk_width = proj
        assert tm == ATTN_ROWS
        v_width = w_in.shape[1] - pool_width - 2 * qk_width
        k0 = pool_width + qk_width
        w_uk = jnp.concatenate([w_in[:, :pool_width], w_in[:, k0:k0 + qk_width]], axis=1).astype(BF16)
        w_qv_t = jnp.concatenate([w_in[:, pool_width:k0], w_in[:, k0 + qk_width:]], axis=1).T.astype(BF16)
        in_specs += [pl.BlockSpec((1, d), const2), _resident(w_uk.shape, const2),
                     _resident(w_qv_t.shape, const2)]
        args += [g2.reshape(1, d), w_uk, w_qv_t]
        out_specs += [pl.BlockSpec((1, tm, pool_width), row),
                      pl.BlockSpec((1, tm, qk_width), row),
                      pl.BlockSpec((1, qk_width, tm), lambda bi, i: (bi, 0, i)),
                      pl.BlockSpec((1, 1, v_width, tm), lambda bi, i: (bi, i, 0, 0))]
        out_shape += [jax.ShapeDtypeStruct((b, s, pool_width), F32),
                      jax.ShapeDtypeStruct((b, s, qk_width), BF16),
                      jax.ShapeDtypeStruct((b, qk_width, s), BF16),
                      jax.ShapeDtypeStruct((b, s // tm, v_width, tm), BF16)]
    elif proj_kind == "odd":
        _, g2, w_in, nq, nv = proj
        n_main = 2 * nq + 2 * nv
        n_gates = w_in.shape[1] - n_main
        w_gate = jnp.pad(w_in[:, n_main:], ((0, 0), (0, LANES - n_gates)))
        in_specs += [pl.BlockSpec((1, d), const2), _resident((d, n_main), const2),
                     _resident((d, LANES), const2)]
        args += [g2.reshape(1, d), w_in, w_gate]
        out_specs += [pl.BlockSpec((1, tm, nq), row),
                      pl.BlockSpec((1, tm, nq), row),
                      pl.BlockSpec((1, tm, nv), row),
                      pl.BlockSpec((1, tm, nv), row),
                      pl.BlockSpec((1, n_gates, tm), lambda bi, i: (bi, 0, i))]
        out_shape += [jax.ShapeDtypeStruct((b, s, nq), BF16),
                      jax.ShapeDtypeStruct((b, s, nq), BF16),
                      jax.ShapeDtypeStruct((b, s, nv), BF16),
                      jax.ShapeDtypeStruct((b, s, nv), BF16),
                      jax.ShapeDtypeStruct((b, n_gates, s), F32)]
    name = "ffn" if kind is None else kind + "_mix_ffn"
    outs = pl.pallas_call(
        functools.partial(_ffn_kernel, sub=sub, final_norm=final_norm, mix=kind,
                          n_cast=len(casts), proj=proj_kind),
        grid=(b, n_i),
        in_specs=in_specs,
        out_specs=out_specs,
        out_shape=out_shape,
        compiler_params=_params("arbitrary", "arbitrary"),
        name=name if proj is None else name + "_" + proj_kind + "_proj",
    )(*args)
    n_cast = len(casts)
    return outs[0], tuple(outs[1:1 + n_cast]), tuple(outs[1 + n_cast:])


def _diff_attn_kernel(qt_ref, k_ref, vt_ref, lam_ref, sg_ref, o_ref, m_ref, l_ref, acc_ref,
                      s0_ref, s1_ref, cm0_ref, cm1_ref, *, lambda_init, t):
    def stacked(part):
        qt = qt_ref[0, :, part * t:(part + 1) * t].astype(F32)
        feat = lax.broadcasted_iota(jnp.int32, qt.shape, 0)
        return jnp.concatenate([jnp.where(feat < DIFF_HEAD_DIM, qt, 0.0),
                                jnp.where(feat >= DIFF_HEAD_DIM, qt, 0.0)], axis=1).astype(BF16)

    qs = [stacked(part) for part in range(ATTN_TILES_PER_STEP)]
    for part in range(ATTN_TILES_PER_STEP):
        rows = slice(part * t, (part + 1) * t)
        qs_next = qs[part + 1] if part + 1 < ATTN_TILES_PER_STEP and part % 2 == 0 else None
        _diff_attn_query_tile(pl.program_id(2) * ATTN_TILES_PER_STEP + part, qs[part], qs_next,
                              part % 2 == 1, k_ref, vt_ref, lam_ref, sg_ref, o_ref.at[0, rows, :],
                              m_ref, l_ref, acc_ref, (s0_ref, s1_ref), (cm0_ref, cm1_ref),
                              lambda_init, t)


def _diff_attn_query_tile(i, qs, qs_next, first_scores_ready, k_ref, vt_ref, lam_ref, sg_ref, o_ref,
                          m_ref, l_ref, acc_ref, s_refs, cm_refs, lambda_init, t):
    m_ref[...] = jnp.full_like(m_ref, -jnp.inf)
    l_ref[...] = jnp.zeros_like(l_ref)
    acc_ref[...] = jnp.zeros_like(acc_ref)
    groups = [slice(c, c + ATTN_COL_GROUP) for c in range(0, 2 * t, ATTN_COL_GROUP)]

    def produce(j, slot, cols, queries=qs):
        kb = k_ref[0, pl.ds(pl.multiple_of(j * t, t), t), :]
        s = _dot(kb, queries[:, cols])
        s_refs[slot][:, cols] = s
        cm_refs[slot][:, cols] = jnp.max(s, axis=0, keepdims=True)

    def consume(j, slot, cols, diagonal):
        if diagonal:
            first_q = cols.start % t
            n_keys = first_q + ATTN_COL_GROUP
            s = s_refs[slot][:n_keys, cols]
            key = lax.broadcasted_iota(jnp.int32, s.shape, 0)
            qry = lax.broadcasted_iota(jnp.int32, s.shape, 1) + first_q
            s = jnp.where(key <= qry, s, -jnp.inf)
            cm = jnp.max(s, axis=0, keepdims=True)
            vt = vt_ref[0, j][:, :n_keys]
        else:
            s = s_refs[slot][:, cols]
            cm = cm_refs[slot][:, cols]
            vt = vt_ref[0, j]
        m = m_ref[:, cols]
        m_new = jnp.maximum(m, cm)
        alpha = jnp.exp2(m - m_new)
        p = jnp.exp2(s - m_new)
        l_ref[:, cols] = alpha * l_ref[:, cols] + jnp.sum(p, axis=0, keepdims=True)
        acc_ref[:, cols] = alpha * acc_ref[:, cols] + _dot(vt, p.astype(BF16))
        m_ref[:, cols] = m_new

    def advance(j, slot):
        for cols in groups:
            produce(j + 1, 1 - slot, cols)
            consume(j, slot, cols, False)

    rest = i % ATTN_UNROLL
    for r in range(ATTN_UNROLL):
        @pl.when(rest == r)
        def _(r=r):
            if not first_scores_ready:
                for cols in groups:
                    produce(0, r % 2, cols)
            for j in range(r):
                advance(j, (j - r) % 2)

    def body(jj, carry):
        for d in range(ATTN_UNROLL):
            advance(rest + ATTN_UNROLL * jj + d, d % 2)
        return carry

    lax.fori_loop(0, i // ATTN_UNROLL, body, 0)
    for cols in groups:
        if qs_next is not None:
            produce(0, 1, cols, qs_next)
        consume(i, 0, cols, True)

    o = acc_ref[...] / l_ref[...]
    lp = lam_ref[...]
    lam = (jnp.exp(jnp.sum(lp[0:1] * lp[1:2], axis=1, keepdims=True))
           - jnp.exp(jnp.sum(lp[2:3] * lp[3:4], axis=1, keepdims=True)) + lambda_init)
    o = o[:, :t] - lam * o[:, t:]
    o = o * lax.rsqrt(jnp.mean(o * o, axis=0, keepdims=True) + EPS) * sg_ref[...]
    o_ref[...] = (o * (1.0 - lambda_init)).T.astype(o_ref.dtype)


def _diff_attention(qt, k, vt, lam_params, subln_g, lambda_init):
    b, s, _ = k.shape
    hd = 2 * DIFF_HEAD_DIM
    t = ATTN_ROWS
    tq = ATTN_TILES_PER_STEP * t
    assert s % tq == 0 and vt.shape[-1] == t and (2 * t) % ATTN_COL_GROUP == 0
    assert ATTN_TILES_PER_STEP % 2 == 0 and ATTN_UNROLL % 2 == 0
    return pl.pallas_call(
        functools.partial(_diff_attn_kernel, lambda_init=lambda_init, t=t),
        grid=(b, DIFF_HEADS, s // tq),
        in_specs=[pl.BlockSpec((1, hd, tq), lambda bi, h, i: (bi, h, i)),
                  pl.BlockSpec((1, s, hd), lambda bi, h, i: (bi, 0, h)),
                  pl.BlockSpec((1, s // t, hd, t), lambda bi, h, i: (bi, 0, h, 0)),
                  pl.BlockSpec(lam_params.shape, lambda bi, h, i: (0, 0)),
                  pl.BlockSpec((hd, 1), lambda bi, h, i: (0, 0))],
        out_specs=pl.BlockSpec((1, tq, hd), lambda bi, h, i: (bi, i, h)),
        out_shape=jax.ShapeDtypeStruct((b, s, DIFF_HEADS * hd), BF16),
        scratch_shapes=[pltpu.VMEM((1, 2 * t), F32), pltpu.VMEM((1, 2 * t), F32),
                        pltpu.VMEM((hd, 2 * t), F32),
                        pltpu.VMEM((t, 2 * t), F32), pltpu.VMEM((t, 2 * t), F32),
                        pltpu.VMEM((1, 2 * t), F32), pltpu.VMEM((1, 2 * t), F32)],
        compiler_params=_params("arbitrary", "arbitrary", "arbitrary"),
        name="diff_attention",
    )(qt, k, vt, lam_params, subln_g.reshape(hd, 1))


def _log_sigmoid(x):
    return jnp.minimum(x, 0.0) - jnp.log1p(jnp.exp(-jnp.abs(x)))


def _lane_scan(x, combine, fill):
    lane = lax.broadcasted_iota(jnp.int32, x.shape, 1)
    shift = 1
    while shift < x.shape[-1]:
        x = combine(x, jnp.where(lane >= shift, pltpu.roll(x, shift, 1), fill))
        shift *= 2
    return x


def _mlstm_kernel(bg_ref, q_ref, k_ref, v_ref, og_ref, ig_ref, fg_ref, ig_next_ref, fg_next_ref,
                  ng_ref, y_ref, c_ref, m_ref, cols_ref, src_ref, w_ref, sold_ref, *, chunks):
    head = pl.program_id(1)
    step = pl.program_id(2)
    L = MLSTM_CHUNK
    G = chunks
    wraps = step == pl.num_programs(2) - 1
    next_head = jnp.where(wraps, (head + 1) % MLSTM_HEADS, head)

    def store_gate_stats(i_ref, f_ref, hd, fresh):
        i_all = i_ref[0, 0] + bg_ref[hd]
        logf = _log_sigmoid(f_ref[0, 0] + bg_ref[MLSTM_HEADS + hd])
        b_all = _lane_scan(logf, jnp.add, 0.0)
        b_last = b_all[:, L - 1:L]
        a_all = b_last - b_all + i_all
        m_loc = jnp.max(a_all, axis=1, keepdims=True)

        m_in = []
        m = jnp.where(fresh, 0.0, m_ref[...])
        for g in range(G):
            m_in.append(m)
            m = jnp.maximum(b_last[g:g + 1] + m, m_loc[g:g + 1])
        m_out = m_in[1:] + [m]
        m_ref[...] = m
        m_prev = jnp.concatenate(m_in, axis=0)
        m_next = jnp.concatenate(m_out, axis=0)
        sold_ref[...] = jnp.exp(b_last + m_prev - m_next)
        w_ref[...] = jnp.exp(a_all - m_next)

        src = i_all - b_all
        src_ref[...] = src
        inter_log = b_all + m_prev
        m_t = jnp.maximum(inter_log, b_all + _lane_scan(src, jnp.maximum, -jnp.inf))
        per_row = jnp.concatenate([b_all - m_t, jnp.exp(inter_log - m_t), jnp.exp(-m_t)], axis=0)
        per_row = jnp.concatenate([per_row, jnp.zeros((cols_ref.shape[1] - 3 * G, L), F32)], axis=0)
        cols_ref[...] = per_row.T

    @pl.when(step == 0)
    def _():
        c_ref[...] = jnp.zeros_like(c_ref)

    @pl.when((pl.program_id(0) == 0) & (head == 0) & (step == 0))
    def _():
        m_ref[...] = jnp.zeros_like(m_ref)
        store_gate_stats(ig_ref, fg_ref, head, True)

    cols = cols_ref[...]
    src = src_ref[...]
    w_all = w_ref[...]
    s_old = sold_ref[...]
    store_gate_stats(ig_next_ref, fg_next_ref, next_head, wraps)

    causal = (lax.broadcasted_iota(jnp.int32, (L, L), 1) <= lax.broadcasted_iota(jnp.int32, (L, L), 0))
    dv = v_ref.shape[-1]
    ones_col = (lax.broadcasted_iota(jnp.int32, (L, LANES), 1) == 0).astype(BF16)
    state = c_ref[...]
    for g in range(G):
        rows = slice(g * L, (g + 1) * L)
        q = q_ref[0, rows, :]
        k = k_ref[0, rows, :]
        v_ext = jnp.concatenate([v_ref[0, rows, :], ones_col], axis=1)
        dest = cols[:, g:g + 1]
        inter_w = cols[:, G + g:G + g + 1]
        floor = cols[:, 2 * G + g:2 * G + g + 1]

        dw = jnp.exp(jnp.where(causal, dest + src[g:g + 1], -jnp.inf))
        s = lax.dot_general(q, k, (((1,), (1,)), ((), ())), preferred_element_type=F32) * dw
        q_w = (inter_w * q.astype(F32)).astype(BF16)
        nd = _dot(s.astype(BF16), v_ext) + _dot(q_w, state.astype(BF16))
        num = nd[:, :dv]
        den = nd[:, dv:dv + 1]
        r = 1.0 / jnp.maximum(jnp.abs(den), floor)
        scale = r * lax.rsqrt(r * r * jnp.mean(num * num, axis=-1, keepdims=True) + EPS)
        hn = num * scale * ng_ref[...]
        y_ref[0, rows, :] = (og_ref[0, rows, :].astype(F32) * hn).astype(y_ref.dtype)

        kw_t = (k.astype(F32).T * w_all[g:g + 1]).astype(BF16)
        state = s_old[g:g + 1] * state + _dot(kw_t, v_ext)
    c_ref[...] = state


def _mlstm(q, k, v, o, gates, b_gates, norm_g):
    b, s, _ = q.shape
    dk, dv, L = MLSTM_QK_DIM, MLSTM_V_DIM, MLSTM_CHUNK
    nc = s // L
    g = MLSTM_CHUNKS_PER_STEP
    rows = g * L
    gates = gates.reshape(b, 2 * MLSTM_HEADS, nc, L)
    steps = nc // g
    blocks = b * MLSTM_HEADS * steps

    def gate_spec(first, ahead):
        def index(bi, h, c):
            flat = jnp.minimum((bi * MLSTM_HEADS + h) * steps + c + ahead, blocks - 1)
            return (flat // (MLSTM_HEADS * steps), first + (flat // steps) % MLSTM_HEADS, flat % steps, 0)
        return pl.BlockSpec((1, 1, g, L), index)

    return pl.pallas_call(
        functools.partial(_mlstm_kernel, chunks=g),
        grid=(b, MLSTM_HEADS, nc // g),
        in_specs=[pl.BlockSpec(memory_space=pltpu.SMEM),
                  pl.BlockSpec((1, rows, dk), lambda bi, h, c: (bi, c, h)),
                  pl.BlockSpec((1, rows, dk), lambda bi, h, c: (bi, c, h)),
                  pl.BlockSpec((1, rows, dv), lambda bi, h, c: (bi, c, h)),
                  pl.BlockSpec((1, rows, dv), lambda bi, h, c: (bi, c, h)),
                  gate_spec(0, 0), gate_spec(MLSTM_HEADS, 0),
                  gate_spec(0, 1), gate_spec(MLSTM_HEADS, 1),
                  pl.BlockSpec((1, dv), lambda bi, h, c: (0, h))],
        out_specs=pl.BlockSpec((1, rows, dv), lambda bi, h, c: (bi, c, h)),
        out_shape=jax.ShapeDtypeStruct((b, s, MLSTM_HEADS * dv), BF16),
        scratch_shapes=[pltpu.VMEM((dk, dv + LANES), F32), pltpu.VMEM((1, 1), F32),
                        pltpu.VMEM((L, -(-3 * g // LANES) * LANES), F32),
                        pltpu.VMEM((g, L), F32), pltpu.VMEM((g, L), F32),
                        pltpu.VMEM((g, 1), F32)],
        compiler_params=_params("arbitrary", "arbitrary", "arbitrary"),
        name="mlstm",
    )(b_gates.reshape(-1), q, k, v, o, gates, gates, gates, gates, norm_g.reshape(1, -1))


def kernel(x, c, w_mod, b_mod, norm_g, w_ffn_gate, w_ffn_up, w_ffn_down, w_in_even, w_pool, pool_scale, diff_lambda, diff_subln_g, w_out_even, w_in_odd, b_gates_odd, mlstm_norm_g, w_out_odd, final_g):
    depth = w_mod.shape[0]
    mod = _modulation(c, w_mod, b_mod)
    pool_width = pool_scale.shape[-1]
    qk_width = DIFF_HEADS * 2 * DIFF_HEAD_DIM
    ffn_stacked = (w_ffn_gate, w_ffn_up, w_ffn_down)
    ffn_w = tuple(w[0, 0].astype(BF16) for w in ffn_stacked)
    w_in_odd_bf16 = None
    for l in range(depth):
        if l % 2 == 0:
            proj = ("even", norm_g[l, 1], w_in_even[l // 2], pool_width, qk_width)
        else:
            proj = ("odd", norm_g[l, 1], w_in_odd_bf16,
                    MLSTM_HEADS * MLSTM_QK_DIM, MLSTM_HEADS * MLSTM_V_DIM)
        x, cast, mixer_in = _ffn(x, mod, l, 0, norm_g[l, 0], ffn_w, final_g, False,
                                 casts=[(w, (l, 1)) for w in ffn_stacked], proj=proj)
        ffn_w = cast[:3]
        if l % 2 == 0:
            e = l // 2
            lambda_init = 0.8 - 0.6 * math.exp(-0.3 * l)
            u, k, qt, vt = mixer_in
            y_diff = _diff_attention(qt, k, vt, diff_lambda[e], diff_subln_g[e], lambda_init)
            mix = ("even", u, y_diff, w_pool[e], pool_scale[e], w_out_even[e])
        else:
            o = l // 2
            q, k, v, og, gates = mixer_in
            y = _mlstm(q, k, v, og, gates, b_gates_odd[o], mlstm_norm_g[o])
            mix = ("odd", y, w_out_odd[o])
        last = l == depth - 1
        casts = [] if last else [(w, (l + 1, 0)) for w in ffn_stacked]
        if not last and (l + 1) % 2 == 1:
            casts.append((w_in_odd, ((l + 1) // 2,)))
        x, cast, _ = _ffn(x, mod, l, 2, norm_g[l, 2], ffn_w, final_g, last, mix, casts=casts)
        ffn_w = cast[:3]
        w_in_odd_bf16 = cast[3] if len(cast) > 3 else None
    return x
```

```python
import functools
import math

import jax
import jax.numpy as jnp
from jax import lax
from jax.experimental import pallas as pl
from jax.experimental.pallas import tpu as pltpu

F32 = jnp.float32
BF16 = jnp.bfloat16

EPS = 1e-6
N_SUB = 3
POOL_WINDOWS = (2, 4, 8, 16)
POOL_HALO = 16
DIFF_HEADS = 4
DIFF_HEAD_DIM = 64
MLSTM_HEADS = 4
MLSTM_QK_DIM = 128
MLSTM_V_DIM = 256
MLSTM_CHUNK = 128
LANES = 128
BF16_SUBLANES = 16

VMEM_LIMIT_BYTES = 56 * 1024 * 1024

MOD_ROWS = 256
FFN_ROWS = 512
FFN_CAST_CHUNKS = 16
FFN_SUB_ROWS = 256
ATTN_ROWS = 512
ATTN_COL_GROUP = 256
ATTN_TILES_PER_STEP = 4
ATTN_UNROLL = 4
LOG2_E = math.log2(math.e)
MLSTM_CHUNKS_PER_STEP = 64


def _params(*semantics):
    return pltpu.CompilerParams(dimension_semantics=semantics,
                                vmem_limit_bytes=VMEM_LIMIT_BYTES)


def _resident(block_shape, index_map):
    return pl.BlockSpec(block_shape, index_map, pipeline_mode=pl.Buffered(1))


def _dot(a, b):
    return jnp.dot(a, b, preferred_element_type=F32)


def _modnorm(x, g, mod, sub):
    shift = mod[3 * sub:3 * sub + 1]
    scale = mod[3 * sub + 1:3 * sub + 2]
    y = x * lax.rsqrt(jnp.mean(x * x, axis=-1, keepdims=True) + EPS)
    return (y * g) * (1.0 + scale) + shift


def _gate(mod, sub):
    return mod[3 * sub + 2:3 * sub + 3]


def _mod_kernel(c_ref, w_ref, b_ref, o_ref):
    @pl.when(pl.program_id(1) == 0)
    def _():
        o_ref[0] = jnp.broadcast_to(b_ref[0], o_ref.shape[1:])

    c = c_ref[...]
    c_act = c * jax.nn.sigmoid(c)
    w = w_ref[0]
    c_hi = c_act.astype(BF16)
    c_lo = (c_act - c_hi.astype(F32)).astype(BF16)
    w_hi = w.astype(BF16)
    w_lo = (w - w_hi.astype(F32)).astype(BF16)
    o_ref[0] += _dot(c_hi, w_hi) + (_dot(c_hi, w_lo) + _dot(c_lo, w_hi))


def _modulation(c, w_mod, b_mod):
    depth, d, n = w_mod.shape
    b = c.shape[0]
    tk = MOD_ROWS
    out = pl.pallas_call(
        _mod_kernel,
        grid=(depth, d // tk),
        in_specs=[pl.BlockSpec((b, tk), lambda l, j: (0, j)),
                  pl.BlockSpec((1, tk, n), lambda l, j: (l, j, 0)),
                  pl.BlockSpec((1, 1, n), lambda l, j: (l, 0, 0))],
        out_specs=pl.BlockSpec((1, b, n), lambda l, j: (l, 0, 0)),
        out_shape=jax.ShapeDtypeStruct((depth, b, n), F32),
        compiler_params=_params("arbitrary", "arbitrary"),
        name="modulation",
    )(c, w_mod, b_mod.reshape(depth, 1, n))
    return out.reshape(depth, b, 3 * N_SUB, d)


def _swiglu_residual(x, mod, sub, g, wg_ref, wu_ref, wd_ref, fg, final_norm):
    outs = []
    for r in range(0, x.shape[0], FFN_SUB_ROWS):
        xs = x[r:r + FFN_SUB_ROWS]
        h = _modnorm(xs, g, mod, sub).astype(BF16)
        gate_act = _dot(h, wg_ref[...])
        up = _dot(h, wu_ref[...])
        a = (gate_act * jax.nn.sigmoid(gate_act) * up).astype(BF16)
        out = xs + (0.5 * _gate(mod, sub)) * _dot(a, wd_ref[...])
        if final_norm:
            out = out * lax.rsqrt(jnp.mean(out * out, axis=-1, keepdims=True) + EPS) * fg
        outs.append(out)
    return jnp.concatenate(outs, axis=0)


def _pool_mixer(u, halo, first_row, wp_ref, scale):
    tm = u.shape[0]
    ext = jnp.concatenate([halo, u], axis=0)
    pos = first_row + lax.broadcasted_iota(jnp.int32, (tm, 1), 0)
    gw = wp_ref.shape[-1]
    sums = ext
    shift = 1
    parts = []
    for g, w in enumerate(POOL_WINDOWS):
        while shift < w:
            sums = sums + pltpu.roll(sums, shift, 0)
            shift *= 2
        total = sums[POOL_HALO:, g * gw:(g + 1) * gw]
        count = jnp.minimum(pos + 1, w).astype(F32)
        pooled = total / count - u[:, g * gw:(g + 1) * gw]
        parts.append(_dot(pooled.astype(BF16), wp_ref[g]))
    return (jnp.concatenate(parts, axis=1) * scale).astype(BF16)


def _mixed_input(mix, x_ref, mod, mix_refs):
    if mix is None:
        return x_ref[0]
    if mix == "odd":
        y_ref, wo_ref = mix_refs
        return x_ref[0] + _gate(mod, 1) * _dot(y_ref[0], wo_ref[...])
    u_ref, halo_ref, yd_ref, wp_ref, ps_ref, wo_ref = mix_refs
    i = pl.program_id(1)
    tm = u_ref.shape[1]
    halo = halo_ref[0] * (i > 0).astype(F32)
    y_pool = _pool_mixer(u_ref[0], halo, i * tm, wp_ref, ps_ref[...])
    pw = y_pool.shape[1]
    y = _dot(y_pool, wo_ref[:pw, :]) + _dot(yd_ref[0], wo_ref[pw:, :])
    return x_ref[0] + _gate(mod, 1) * y


_N_MIX_REFS = {None: 0, "odd": 2, "even": 6}


def _even_projection(h, w_ref, wt_ref, u_ref, k_ref, qt_ref, vt_ref):
    p = _dot(h, w_ref[...])
    pw = u_ref.shape[-1]
    u_ref[0] = p[:, :pw]
    k_ref[0] = p[:, pw:].astype(BF16)
    pt = lax.dot_general(wt_ref[...], h, (((1,), (1,)), ((), ())), preferred_element_type=F32)
    qw = qt_ref.shape[1]
    qt_ref[0] = (pt[:qw] * (DIFF_HEAD_DIM ** -0.5 * LOG2_E)).astype(BF16)
    vt_ref[0, 0] = pt[qw:].astype(BF16)


def _odd_projection(h, w_ref, wgate_ref, q_ref, k_ref, v_ref, o_ref, gates_ref):
    p = _dot(h, w_ref[...])
    nq = q_ref.shape[-1]
    nv = v_ref.shape[-1]
    q_ref[0] = p[:, :nq].astype(BF16)
    k_ref[0] = (p[:, nq:2 * nq] * (MLSTM_QK_DIM ** -0.5)).astype(BF16)
    v_ref[0] = p[:, 2 * nq:2 * nq + nv].astype(BF16)
    o_ref[0] = jax.nn.sigmoid(p[:, 2 * nq + nv:]).astype(BF16)
    gates_ref[0] = _dot(h, wgate_ref[...]).T[:gates_ref.shape[1]]


def _ffn_kernel(*refs, sub, final_norm, mix, n_cast, proj):
    x_ref, mod_ref, g_ref, wg_ref, wu_ref, wd_ref, fg_ref = refs[:7]
    pos = 7
    mix_refs = refs[pos:pos + _N_MIX_REFS[mix]]
    pos += len(mix_refs)
    cast_in = refs[pos:pos + n_cast]
    pos += n_cast
    proj_in = refs[pos:pos + (3 if proj else 0)]
    pos += len(proj_in)
    o_ref = refs[pos]
    cast_out = refs[pos + 1:pos + 1 + n_cast]
    proj_out = refs[pos + 1 + n_cast:]
    mod = mod_ref[0, 0]
    x = _mixed_input(mix, x_ref, mod, mix_refs)
    out = _swiglu_residual(x, mod, sub, g_ref[...], wg_ref, wu_ref, wd_ref, fg_ref[...], final_norm)
    o_ref[0] = out
    for src_ref, dst_ref in zip(cast_in, cast_out):
        dst_ref[...] = src_ref[(0,) * (len(src_ref.shape) - 2)].astype(BF16)
    if proj:
        g2_ref, w_a_ref, w_b_ref = proj_in
        h = _modnorm(out, g2_ref[...], mod, 1).astype(BF16)
        (_even_projection if proj == "even" else _odd_projection)(h, w_a_ref, w_b_ref, *proj_out)


def _ffn(x, mod, layer, sub, g, weights, final_g, final_norm, mix=None, casts=(), proj=None):
    b, s, d = x.shape
    f = weights[0].shape[-1]
    tm = FFN_ROWS
    n_i = s // tm
    row = lambda bi, i: (bi, i, 0)
    const2 = lambda bi, i: (0, 0)
    in_specs = [pl.BlockSpec((1, tm, d), row),
                pl.BlockSpec((1, 1, 3 * N_SUB, d), lambda bi, i: (layer, bi, 0, 0)),
                pl.BlockSpec((1, d), const2),
                _resident((d, f), const2),
                _resident((d, f), const2),
                _resident((f, d), const2),
                pl.BlockSpec((1, d), const2)]
    args = [x, mod, g.reshape(1, d), *weights, final_g.reshape(1, d)]
    kind = None if mix is None else mix[0]
    if kind == "odd":
        _, y, w_out = mix
        in_specs += [pl.BlockSpec((1, tm, y.shape[-1]), row), _resident(w_out.shape, const2)]
        args += [y, w_out.astype(BF16)]
    elif kind == "even":
        _, u, y_diff, w_pool, pool_scale, w_out = mix
        pw = u.shape[-1]
        halo_blocks = tm // POOL_HALO
        in_specs += [pl.BlockSpec((1, tm, pw), row),
                     pl.BlockSpec((1, POOL_HALO, pw),
                                  lambda bi, i: (bi, jnp.maximum(i * halo_blocks - 1, 0), 0)),
                     pl.BlockSpec((1, tm, y_diff.shape[-1]), row),
                     _resident(w_pool.shape, lambda bi, i: (0, 0, 0)),
                     pl.BlockSpec((1, pw), const2),
                     _resident(w_out.shape, const2)]
        args += [u, u, y_diff, w_pool.astype(BF16), pool_scale.reshape(1, pw), w_out.astype(BF16)]
    out_specs = [pl.BlockSpec((1, tm, d), row)]
    out_shape = [jax.ShapeDtypeStruct((b, s, d), F32)]
    steps = b * n_i
    assert steps % FFN_CAST_CHUNKS == 0
    chunk_of = lambda bi, i: (bi * n_i + i) // (steps // FFN_CAST_CHUNKS)
    for w, lead in casts:
        rows, cols = w.shape[-2:]
        assert rows % (FFN_CAST_CHUNKS * BF16_SUBLANES) == 0
        chunk = rows // FFN_CAST_CHUNKS
        in_specs.append(pl.BlockSpec((1,) * len(lead) + (chunk, cols),
                                     lambda bi, i, lead=lead: (*lead, chunk_of(bi, i), 0)))
        args.append(w)
        out_specs.append(pl.BlockSpec((chunk, cols), lambda bi, i: (chunk_of(bi, i), 0)))
        out_shape.append(jax.ShapeDtypeStruct((rows, cols), BF16))
    proj_kind = None if proj is None else proj[0]
    if proj_kind == "even":
        _, g2, w_in, pool_width, qk_width = proj
        assert tm == ATTN_ROWS
        v_width = w_in.shape[1] - pool_width - 2 * qk_width
        k0 = pool_width + qk_width
        w_uk = jnp.concatenate([w_in[:, :pool_width], w_in[:, k0:k0 + qk_width]], axis=1).astype(BF16)
        w_qv_t = jnp.concatenate([w_in[:, pool_width:k0], w_in[:, k0 + qk_width:]], axis=1).T.astype(BF16)
        in_specs += [pl.BlockSpec((1, d), const2), _resident(w_uk.shape, const2),
                     _resident(w_qv_t.shape, const2)]
        args += [g2.reshape(1, d), w_uk, w_qv_t]
        out_specs += [pl.BlockSpec((1, tm, pool_width), row),
                      pl.BlockSpec((1, tm, qk_width), row),
                      pl.BlockSpec((1, qk_width, tm), lambda bi, i: (bi, 0, i)),
                      pl.BlockSpec((1, 1, v_width, tm), lambda bi, i: (bi, i, 0, 0))]
        out_shape += [jax.ShapeDtypeStruct((b, s, pool_width), F32),
                      jax.ShapeDtypeStruct((b, s, qk_width), BF16),
                      jax.ShapeDtypeStruct((b, qk_width, s), BF16),
                      jax.ShapeDtypeStruct((b, s // tm, v_width, tm), BF16)]
    elif proj_kind == "odd":
        _, g2, w_in, nq, nv = proj
        n_main = 2 * nq + 2 * nv
        n_gates = w_in.shape[1] - n_main
        w_gate = jnp.pad(w_in[:, n_main:], ((0, 0), (0, LANES - n_gates)))
        in_specs += [pl.BlockSpec((1, d), const2), _resident((d, n_main), const2),
                     _resident((d, LANES), const2)]
        args += [g2.reshape(1, d), w_in, w_gate]
        out_specs += [pl.BlockSpec((1, tm, nq), row),
                      pl.BlockSpec((1, tm, nq), row),
                      pl.BlockSpec((1, tm, nv), row),
                      pl.BlockSpec((1, tm, nv), row),
                      pl.BlockSpec((1, n_gates, tm), lambda bi, i: (bi, 0, i))]
        out_shape += [jax.ShapeDtypeStruct((b, s, nq), BF16),
                      jax.ShapeDtypeStruct((b, s, nq), BF16),
                      jax.ShapeDtypeStruct((b, s, nv), BF16),
                      jax.ShapeDtypeStruct((b, s, nv), BF16),
                      jax.ShapeDtypeStruct((b, n_gates, s), F32)]
    name = "ffn" if kind is None else kind + "_mix_ffn"
    outs = pl.pallas_call(
        functools.partial(_ffn_kernel, sub=sub, final_norm=final_norm, mix=kind,
                          n_cast=len(casts), proj=proj_kind),
        grid=(b, n_i),
        in_specs=in_specs,
        out_specs=out_specs,
        out_shape=out_shape,
        compiler_params=_params("arbitrary", "arbitrary"),
        name=name if proj is None else name + "_" + proj_kind + "_proj",
    )(*args)
    n_cast = len(casts)
    return outs[0], tuple(outs[1:1 + n_cast]), tuple(outs[1 + n_cast:])


def _diff_attn_kernel(qt_ref, k_ref, vt_ref, lam_ref, sg_ref, o_ref, m_ref, l_ref, acc_ref,
                      s0_ref, s1_ref, cm0_ref, cm1_ref, *, lambda_init, t):
    def stacked(part):
        qt = qt_ref[0, :, part * t:(part + 1) * t].astype(F32)
        feat = lax.broadcasted_iota(jnp.int32, qt.shape, 0)
        return jnp.concatenate([jnp.where(feat < DIFF_HEAD_DIM, qt, 0.0),
                                jnp.where(feat >= DIFF_HEAD_DIM, qt, 0.0)], axis=1).astype(BF16)

    qs = [stacked(part) for part in range(ATTN_TILES_PER_STEP)]
    for part in range(ATTN_TILES_PER_STEP):
        rows = slice(part * t, (part + 1) * t)
        qs_next = qs[part + 1] if part + 1 < ATTN_TILES_PER_STEP and part % 2 == 0 else None
        _diff_attn_query_tile(pl.program_id(2) * ATTN_TILES_PER_STEP + part, qs[part], qs_next,
                              part % 2 == 1, k_ref, vt_ref, lam_ref, sg_ref, o_ref.at[0, rows, :],
                              m_ref, l_ref, acc_ref, (s0_ref, s1_ref), (cm0_ref, cm1_ref),
                              lambda_init, t)


def _diff_attn_query_tile(i, qs, qs_next, first_scores_ready, k_ref, vt_ref, lam_ref, sg_ref, o_ref,
                          m_ref, l_ref, acc_ref, s_refs, cm_refs, lambda_init, t):
    m_ref[...] = jnp.full_like(m_ref, -jnp.inf)
    l_ref[...] = jnp.zeros_like(l_ref)
    acc_ref[...] = jnp.zeros_like(acc_ref)
    groups = [slice(c, c + ATTN_COL_GROUP) for c in range(0, 2 * t, ATTN_COL_GROUP)]

    def produce(j, slot, cols, queries=qs):
        kb = k_ref[0, pl.ds(pl.multiple_of(j * t, t), t), :]
        s = _dot(kb, queries[:, cols])
        s_refs[slot][:, cols] = s
        cm_refs[slot][:, cols] = jnp.max(s, axis=0, keepdims=True)

    def consume(j, slot, cols, diagonal):
        if diagonal:
            first_q = cols.start % t
            n_keys = first_q + ATTN_COL_GROUP
            s = s_refs[slot][:n_keys, cols]
            key = lax.broadcasted_iota(jnp.int32, s.shape, 0)
            qry = lax.broadcasted_iota(jnp.int32, s.shape, 1) + first_q
            s = jnp.where(key <= qry, s, -jnp.inf)
            cm = jnp.max(s, axis=0, keepdims=True)
            vt = vt_ref[0, j][:, :n_keys]
        else:
            s = s_refs[slot][:, cols]
            cm = cm_refs[slot][:, cols]
            vt = vt_ref[0, j]
        m = m_ref[:, cols]
        m_new = jnp.maximum(m, cm)
        alpha = jnp.exp2(m - m_new)
        p = jnp.exp2(s - m_new)
        l_ref[:, cols] = alpha * l_ref[:, cols] + jnp.sum(p, axis=0, keepdims=True)
        acc_ref[:, cols] = alpha * acc_ref[:, cols] + _dot(vt, p.astype(BF16))
        m_ref[:, cols] = m_new

    def advance(j, slot):
        for cols in groups:
            produce(j + 1, 1 - slot, cols)
            consume(j, slot, cols, False)

    rest = i % ATTN_UNROLL
    for r in range(ATTN_UNROLL):
        @pl.when(rest == r)
        def _(r=r):
            if not first_scores_ready:
                for cols in groups:
                    produce(0, r % 2, cols)
            for j in range(r):
                advance(j, (j - r) % 2)

    def body(jj, carry):
        for d in range(ATTN_UNROLL):
            advance(rest + ATTN_UNROLL * jj + d, d % 2)
        return carry

    lax.fori_loop(0, i // ATTN_UNROLL, body, 0)
    for cols in groups:
        if qs_next is not None:
            produce(0, 1, cols, qs_next)
        consume(i, 0, cols, True)

    o = acc_ref[...] / l_ref[...]
    lp = lam_ref[...]
    lam = (jnp.exp(jnp.sum(lp[0:1] * lp[1:2], axis=1, keepdims=True))
           - jnp.exp(jnp.sum(lp[2:3] * lp[3:4], axis=1, keepdims=True)) + lambda_init)
    o = o[:, :t] - lam * o[:, t:]
    o = o * lax.rsqrt(jnp.mean(o * o, axis=0, keepdims=True) + EPS) * sg_ref[...]
    o_ref[...] = (o * (1.0 - lambda_init)).T.astype(o_ref.dtype)


def _diff_attention(qt, k, vt, lam_params, subln_g, lambda_init):
    b, s, _ = k.shape
    hd = 2 * DIFF_HEAD_DIM
    t = ATTN_ROWS
    tq = ATTN_TILES_PER_STEP * t
    assert s % tq == 0 and vt.shape[-1] == t and (2 * t) % ATTN_COL_GROUP == 0
    assert ATTN_TILES_PER_STEP % 2 == 0 and ATTN_UNROLL % 2 == 0
    return pl.pallas_call(
        functools.partial(_diff_attn_kernel, lambda_init=lambda_init, t=t),
        grid=(b, DIFF_HEADS, s // tq),
        in_specs=[pl.BlockSpec((1, hd, tq), lambda bi, h, i: (bi, h, i)),
                  pl.BlockSpec((1, s, hd), lambda bi, h, i: (bi, 0, h)),
                  pl.BlockSpec((1, s // t, hd, t), lambda bi, h, i: (bi, 0, h, 0)),
                  pl.BlockSpec(lam_params.shape, lambda bi, h, i: (0, 0)),
                  pl.BlockSpec((hd, 1), lambda bi, h, i: (0, 0))],
        out_specs=pl.BlockSpec((1, tq, hd), lambda bi, h, i: (bi, i, h)),
        out_shape=jax.ShapeDtypeStruct((b, s, DIFF_HEADS * hd), BF16),
        scratch_shapes=[pltpu.VMEM((1, 2 * t), F32), pltpu.VMEM((1, 2 * t), F32),
                        pltpu.VMEM((hd, 2 * t), F32),
                        pltpu.VMEM((t, 2 * t), F32), pltpu.VMEM((t, 2 * t), F32),
                        pltpu.VMEM((1, 2 * t), F32), pltpu.VMEM((1, 2 * t), F32)],
        compiler_params=_params("arbitrary", "arbitrary", "arbitrary"),
        name="diff_attention",
    )(qt, k, vt, lam_params, subln_g.reshape(hd, 1))


def _log_sigmoid(x):
    return jnp.minimum(x, 0.0) - jnp.log1p(jnp.exp(-jnp.abs(x)))


def _lane_scan(x, combine, fill):
    lane = lax.broadcasted_iota(jnp.int32, x.shape, 1)
    shift = 1
    while shift < x.shape[-1]:
        x = combine(x, jnp.where(lane >= shift, pltpu.roll(x, shift, 1), fill))
        shift *= 2
    return x


def _mlstm_kernel(bg_ref, q_ref, k_ref, v_ref, og_ref, ig_ref, fg_ref, ig_next_ref, fg_next_ref,
                  ng_ref, y_ref, c_ref, m_ref, cols_ref, src_ref, w_ref, sold_ref, *, chunks):
    head = pl.program_id(1)
    step = pl.program_id(2)
    L = MLSTM_CHUNK
    G = chunks
    wraps = step == pl.num_programs(2) - 1
    next_head = jnp.where(wraps, (head + 1) % MLSTM_HEADS, head)

    def store_gate_stats(i_ref, f_ref, hd, fresh):
        i_all = i_ref[0, 0] + bg_ref[hd]
        logf = _log_sigmoid(f_ref[0, 0] + bg_ref[MLSTM_HEADS + hd])
        b_all = _lane_scan(logf, jnp.add, 0.0)
        b_last = b_all[:, L - 1:L]
        a_all = b_last - b_all + i_all
        m_loc = jnp.max(a_all, axis=1, keepdims=True)

        m_in = []
        m = jnp.where(fresh, 0.0, m_ref[...])
        for g in range(G):
            m_in.append(m)
            m = jnp.maximum(b_last[g:g + 1] + m, m_loc[g:g + 1])
        m_out = m_in[1:] + [m]
        m_ref[...] = m
        m_prev = jnp.concatenate(m_in, axis=0)
        m_next = jnp.concatenate(m_out, axis=0)
        sold_ref[...] = jnp.exp(b_last + m_prev - m_next)
        w_ref[...] = jnp.exp(a_all - m_next)

        src = i_all - b_all
        src_ref[...] = src
        inter_log = b_all + m_prev
        m_t = jnp.maximum(inter_log, b_all + _lane_scan(src, jnp.maximum, -jnp.inf))
        per_row = jnp.concatenate([b_all - m_t, jnp.exp(inter_log - m_t), jnp.exp(-m_t)], axis=0)
        per_row = jnp.concatenate([per_row, jnp.zeros((cols_ref.shape[1] - 3 * G, L), F32)], axis=0)
        cols_ref[...] = per_row.T

    @pl.when(step == 0)
    def _():
        c_ref[...] = jnp.zeros_like(c_ref)

    @pl.when((pl.program_id(0) == 0) & (head == 0) & (step == 0))
    def _():
        m_ref[...] = jnp.zeros_like(m_ref)
        store_gate_stats(ig_ref, fg_ref, head, True)

    cols = cols_ref[...]
    src = src_ref[...]
    w_all = w_ref[...]
    s_old = sold_ref[...]
    store_gate_stats(ig_next_ref, fg_next_ref, next_head, wraps)

    causal = (lax.broadcasted_iota(jnp.int32, (L, L), 1) <= lax.broadcasted_iota(jnp.int32, (L, L), 0))
    dv = v_ref.shape[-1]
    ones_col = (lax.broadcasted_iota(jnp.int32, (L, LANES), 1) == 0).astype(BF16)
    state = c_ref[...]
    for g in range(G):
        rows = slice(g * L, (g + 1) * L)
        q = q_ref[0, rows, :]
        k = k_ref[0, rows, :]
        v_ext = jnp.concatenate([v_ref[0, rows, :], ones_col], axis=1)
        dest = cols[:, g:g + 1]
        inter_w = cols[:, G + g:G + g + 1]
        floor = cols[:, 2 * G + g:2 * G + g + 1]

        dw = jnp.exp(jnp.where(causal, dest + src[g:g + 1], -jnp.inf))
        s = lax.dot_general(q, k, (((1,), (1,)), ((), ())), preferred_element_type=F32) * dw
        q_w = (inter_w * q.astype(F32)).astype(BF16)
        nd = _dot(s.astype(BF16), v_ext) + _dot(q_w, state.astype(BF16))
        num = nd[:, :dv]
        den = nd[:, dv:dv + 1]
        r = 1.0 / jnp.maximum(jnp.abs(den), floor)
        scale = r * lax.rsqrt(r * r * jnp.mean(num * num, axis=-1, keepdims=True) + EPS)
        hn = num * scale * ng_ref[...]
        y_ref[0, rows, :] = (og_ref[0, rows, :].astype(F32) * hn).astype(y_ref.dtype)

        kw_t = (k.astype(F32).T * w_all[g:g + 1]).astype(BF16)
        state = s_old[g:g + 1] * state + _dot(kw_t, v_ext)
    c_ref[...] = state


def _mlstm(q, k, v, o, gates, b_gates, norm_g):
    b, s, _ = q.shape
    dk, dv, L = MLSTM_QK_DIM, MLSTM_V_DIM, MLSTM_CHUNK
    nc = s // L
    g = MLSTM_CHUNKS_PER_STEP
    rows = g * L
    gates = gates.reshape(b, 2 * MLSTM_HEADS, nc, L)
    steps = nc // g
    blocks = b * MLSTM_HEADS * steps

    def gate_spec(first, ahead):
        def index(bi, h, c):
            flat = jnp.minimum((bi * MLSTM_HEADS + h) * steps + c + ahead, blocks - 1)
            return (flat // (MLSTM_HEADS * steps), first + (flat // steps) % MLSTM_HEADS, flat % steps, 0)
        return pl.BlockSpec((1, 1, g, L), index)

    return pl.pallas_call(
        functools.partial(_mlstm_kernel, chunks=g),
        grid=(b, MLSTM_HEADS, nc // g),
        in_specs=[pl.BlockSpec(memory_space=pltpu.SMEM),
                  pl.BlockSpec((1, rows, dk), lambda bi, h, c: (bi, c, h)),
                  pl.BlockSpec((1, rows, dk), lambda bi, h, c: (bi, c, h)),
                  pl.BlockSpec((1, rows, dv), lambda bi, h, c: (bi, c, h)),
                  pl.BlockSpec((1, rows, dv), lambda bi, h, c: (bi, c, h)),
                  gate_spec(0, 0), gate_spec(MLSTM_HEADS, 0),
                  gate_spec(0, 1), gate_spec(MLSTM_HEADS, 1),
                  pl.BlockSpec((1, dv), lambda bi, h, c: (0, h))],
        out_specs=pl.BlockSpec((1, rows, dv), lambda bi, h, c: (bi, c, h)),
        out_shape=jax.ShapeDtypeStruct((b, s, MLSTM_HEADS * dv), BF16),
        scratch_shapes=[pltpu.VMEM((dk, dv + LANES), F32), pltpu.VMEM((1, 1), F32),
                        pltpu.VMEM((L, -(-3 * g // LANES) * LANES), F32),
                        pltpu.VMEM((g, L), F32), pltpu.VMEM((g, L), F32),
                        pltpu.VMEM((g, 1), F32)],
        compiler_params=_params("arbitrary", "arbitrary", "arbitrary"),
        name="mlstm",
    )(b_gates.reshape(-1), q, k, v, o, gates, gates, gates, gates, norm_g.reshape(1, -1))


def kernel(x, c, w_mod, b_mod, norm_g, w_ffn_gate, w_ffn_up, w_ffn_down, w_in_even, w_pool, pool_scale, diff_lambda, diff_subln_g, w_out_even, w_in_odd, b_gates_odd, mlstm_norm_g, w_out_odd, final_g):
    depth = w_mod.shape[0]
    mod = _modulation(c, w_mod, b_mod)
    pool_width = pool_scale.shape[-1]
    qk_width = DIFF_HEADS * 2 * DIFF_HEAD_DIM
    ffn_stacked = (w_ffn_gate, w_ffn_up, w_ffn_down)
    ffn_w = tuple(w[0, 0].astype(BF16) for w in ffn_stacked)
    w_in_odd_bf16 = None
    for l in range(depth):
        if l % 2 == 0:
            proj = ("even", norm_g[l, 1], w_in_even[l // 2], pool_width, qk_width)
        else:
            proj = ("odd", norm_g[l, 1], w_in_odd_bf16,
                    MLSTM_HEADS * MLSTM_QK_DIM, MLSTM_HEADS * MLSTM_V_DIM)
        x, cast, mixer_in = _ffn(x, mod, l, 0, norm_g[l, 0], ffn_w, final_g, False,
                                 casts=[(w, (l, 1)) for w in ffn_stacked], proj=proj)
        ffn_w = cast[:3]
        if l % 2 == 0:
            e = l // 2
            lambda_init = 0.8 - 0.6 * math.exp(-0.3 * l)
            u, k, qt, vt = mixer_in
            y_diff = _diff_attention(qt, k, vt, diff_lambda[e], diff_subln_g[e], lambda_init)
            mix = ("even", u, y_diff, w_pool[e], pool_scale[e], w_out_even[e])
        else:
            o = l // 2
            q, k, v, og, gates = mixer_in
            y = _mlstm(q, k, v, og, gates, b_gates_odd[o], mlstm_norm_g[o])
            mix = ("odd", y, w_out_odd[o])
        last = l == depth - 1
        casts = [] if last else [(w, (l + 1, 0)) for w in ffn_stacked]
        if not last and (l + 1) % 2 == 1:
            casts.append((w_in_odd, ((l + 1) // 2,)))
        x, cast, _ = _ffn(x, mod, l, 2, norm_g[l, 2], ffn_w, final_g, last, mix, casts=casts)
        ffn_w = cast[:3]
        w_in_odd_bf16 = cast[3] if len(cast) > 3 else None
    return x
```
